```python
import jax
import jax.numpy as jnp
from jax import lax
import numpy as np


D_MODEL = 2048
BATCH = 4
SEQ = 4096
DEPTH = 2

N_EVEN = (DEPTH + 1) // 2
N_ODD = DEPTH // 2
ALPHA = (2 * DEPTH) ** 0.25
BETA = (8 * DEPTH) ** -0.25
GRID_W = 64
PLE_DIM = 256
LN_EPS = 1e-5
RMS_EPS = 1e-6

MLSTM_HEADS = 4
MLSTM_DV = D_MODEL // 2 // MLSTM_HEADS
MLSTM_DK = MLSTM_DV // 2
MLSTM_CHUNK = 64
GATE_CAP = 15.0

HGRN_DK = 128
HGRN_HEADS = D_MODEL // 2 // HGRN_DK
HGRN_DV = HGRN_DK
HGRN_CHUNK = 16

A_WIDTH = MLSTM_HEADS * MLSTM_DV
B_WIDTH = HGRN_HEADS * HGRN_DV
REC_SPLITS = (MLSTM_HEADS * MLSTM_DK, MLSTM_HEADS * MLSTM_DK, A_WIDTH, A_WIDTH, 4 * MLSTM_HEADS, HGRN_HEADS * HGRN_DK, HGRN_HEADS * HGRN_DK, HGRN_HEADS * HGRN_DK, B_WIDTH, B_WIDTH)
REC_IN = sum(REC_SPLITS)

NA_DH = 128
NA_HEADS = D_MODEL // NA_DH
NA_WIDTH = NA_HEADS * NA_DH
NA_KH = 8
NA_KW = 16

FFN_DIM = ((8 * D_MODEL // 3 + 255) // 256) * 256
N_EXPERTS = 8
TOP_K = 2
MOE_DIM = 7 * D_MODEL // 2
MOE_BLOCK = 512

kernel_name = 'hybrid_mlstm_hgrn2_natten_moe_encoder'


def layer_norm(x, w, b):
    xf = x.astype(jnp.float32)
    mu = jnp.mean(xf, axis=-1, keepdims=True)
    xc = xf - mu
    var = jnp.mean(xc * xc, axis=-1, keepdims=True)
    return (xc * lax.rsqrt(var + LN_EPS) * w.astype(jnp.float32) + b.astype(jnp.float32)).astype(x.dtype)


def head_rms_norm(h, w):
    wh = w.astype(jnp.float32).reshape(h.shape[1], 1, h.shape[-1])
    return h * lax.rsqrt(jnp.mean(h * h, axis=-1, keepdims=True) + RMS_EPS) * wh


def to_heads(z, nh):
    bsz, t, _ = z.shape
    return z.reshape(bsz, t, nh, -1).transpose(0, 2, 1, 3).astype(jnp.float32)


def from_heads(h):
    bsz, nh, t, d = h.shape
    return h.transpose(0, 2, 1, 3).reshape(bsz, t, nh * d)


def flip_t(z):
    return jnp.flip(z, axis=2)


def mlstm_chunkwise(q, k, v, log_i, log_f):
    bsz, nh, t, dk = q.shape
    dv = v.shape[-1]
    L = MLSTM_CHUNK
    nc = t // L
    q = q.reshape(bsz, nh, nc, L, dk)
    k = k.reshape(bsz, nh, nc, L, dk)
    v = v.reshape(bsz, nh, nc, L, dv)
    log_i = log_i.reshape(bsz, nh, nc, L)
    b = jnp.cumsum(log_f.reshape(bsz, nh, nc, L), axis=-1)
    g = b[..., -1]
    a = g[..., None] - b + log_i
    m_loc = jnp.max(a, axis=-1)
    kw = k * jnp.exp(a - m_loc[..., None])[..., None]
    c_loc = jnp.einsum('bhcsd,bhcse->bhcde', kw, v)
    n_loc = jnp.sum(kw, axis=-2)

    def step(carry, inp):
        c, n, m = carry
        c_l, n_l, m_l, g_c, q_c = inp
        y = jnp.einsum('bhtd,bhde->bhte', q_c, c)
        d = jnp.einsum('bhtd,bhd->bht', q_c, n)
        m_new = jnp.maximum(g_c + m, m_l)
        decay = jnp.exp(g_c + m - m_new)
        inj = jnp.exp(m_l - m_new)
        c = decay[..., None, None] * c + inj[..., None, None] * c_l
        n = decay[..., None] * n + inj[..., None] * n_l
        return (c, n, m_new), (y, d, m)

    init = (jnp.zeros((bsz, nh, dk, dv), jnp.float32), jnp.zeros((bsz, nh, dk), jnp.float32), jnp.zeros((bsz, nh), jnp.float32))
    xs = (jnp.moveaxis(c_loc, 2, 0), jnp.moveaxis(n_loc, 2, 0), jnp.moveaxis(m_loc, 2, 0), jnp.moveaxis(g, 2, 0), jnp.moveaxis(q, 2, 0))
    _, (y_inter, d_inter, m_prev) = lax.scan(step, init, xs)
    y_inter = jnp.moveaxis(y_inter, 0, 2)
    d_inter = jnp.moveaxis(d_inter, 0, 2)
    m_prev = jnp.moveaxis(m_prev, 0, 2)

    causal = jnp.tril(jnp.ones((L, L), dtype=bool))
    log_d = jnp.where(causal, b[..., :, None] - b[..., None, :] + log_i[..., None, :], -jnp.inf)
    inter_log = b + m_prev[..., None]
    m_t = jnp.maximum(jnp.max(log_d, axis=-1), inter_log)
    s = jnp.einsum('bhctd,bhcsd->bhcts', q, k) * jnp.exp(log_d - m_t[..., None])
    inter_w = jnp.exp(inter_log - m_t)
    num = jnp.einsum('bhcts,bhcse->bhcte', s, v) + inter_w[..., None] * y_inter
    den = jnp.sum(s, axis=-1) + inter_w * d_inter
    h = num / jnp.maximum(jnp.abs(den), jnp.exp(-m_t))[..., None]
    return h.reshape(bsz, nh, t, dv)


def hgrn2_chunkwise(q, k, v, log_f):
    bsz, nh, t, dk = q.shape
    dv = v.shape[-1]
    L = HGRN_CHUNK
    nc = t // L
    q = q.reshape(bsz, nh, nc, L, dk)
    k = k.reshape(bsz, nh, nc, L, dk)
    v = v.reshape(bsz, nh, nc, L, dv)
    b = jnp.cumsum(log_f.reshape(bsz, nh, nc, L, dk), axis=-2)
    g = b[..., -1, :]
    q_in = q * jnp.exp(b)
    k_in = k * jnp.exp(-b)
    k_end = k * jnp.exp(g[..., None, :] - b)
    causal = jnp.tril(jnp.ones((L, L), dtype=bool))
    a = jnp.where(causal, jnp.einsum('bhctd,bhcsd->bhcts', q_in, k_in), 0.0)
    o_intra = jnp.einsum('bhcts,bhcse->bhcte', a, v)

    def step(s, inp):
        k_c, v_c, g_c, q_c = inp
        y = jnp.einsum('bhtd,bhde->bhte', q_c, s)
        s = jnp.exp(g_c)[..., None] * s + jnp.einsum('bhsd,bhse->bhde', k_c, v_c)
        return s, y

    xs = (jnp.moveaxis(k_end, 2, 0), jnp.moveaxis(v, 2, 0), jnp.moveaxis(g, 2, 0), jnp.moveaxis(q_in, 2, 0))
    _, y = lax.scan(step, jnp.zeros((bsz, nh, dk, dv), jnp.float32), xs)
    o = o_intra + jnp.moveaxis(y, 0, 2)
    return o.reshape(bsz, nh, t, dv)


def recurrent_mixer(x, w_in, gate_bias, mlstm_norm_w, lower_bound, hgrn_norm_w, w_out):
    bsz, t, _ = x.shape
    u = x @ w_in
    q_a, k_a, v_a, o_a, gates_a, q_b, f_fw, f_bw, i_b, g_b = jnp.split(u, [int(c) for c in np.cumsum(REC_SPLITS)[:-1]], axis=-1)

    qa = to_heads(q_a, MLSTM_HEADS) * (MLSTM_DK ** -0.5)
    ka = to_heads(k_a, MLSTM_HEADS)
    va = to_heads(v_a, MLSTM_HEADS)
    gates = gates_a.astype(jnp.float32).reshape(bsz, t, 4, MLSTM_HEADS) + gate_bias.astype(jnp.float32)
    gates = (GATE_CAP * jnp.tanh(gates / GATE_CAP)).transpose(2, 0, 3, 1)
    h_fw = mlstm_chunkwise(qa, ka, va, gates[0], jax.nn.log_sigmoid(gates[1]))
    h_bw = flip_t(mlstm_chunkwise(flip_t(qa), flip_t(ka), flip_t(va), flip_t(gates[2]), flip_t(jax.nn.log_sigmoid(gates[3]))))
    h_a = from_heads(head_rms_norm(h_fw + h_bw, mlstm_norm_w))
    y_a = h_a * jax.nn.sigmoid(o_a.astype(jnp.float32))

    qb = jax.nn.silu(to_heads(q_b, HGRN_HEADS))
    vb = to_heads(i_b, HGRN_HEADS)
    lb = lower_bound.astype(jnp.float32).reshape(2, HGRN_HEADS, 1, HGRN_DK)
    f_f = lb[0] + (1.0 - lb[0]) * jax.nn.sigmoid(to_heads(f_fw, HGRN_HEADS))
    f_b = lb[1] + (1.0 - lb[1]) * jax.nn.sigmoid(to_heads(f_bw, HGRN_HEADS))
    o_fw = hgrn2_chunkwise(qb, 1.0 - f_f, vb, jnp.log(f_f))
    o_bw = flip_t(hgrn2_chunkwise(flip_t(qb), flip_t(1.0 - f_b), flip_t(vb), flip_t(jnp.log(f_b))))
    h_b = from_heads(head_rms_norm(o_fw + o_bw, hgrn_norm_w))
    y_b = h_b * jax.nn.silu(g_b.astype(jnp.float32))

    y = jnp.concatenate([y_a, y_b], axis=-1).astype(x.dtype)
    return y @ w_out


def neighborhood_attention(x, w_qkv, rpb, w_out):
    bsz, t, _ = x.shape
    rows = t // GRID_W
    kh = min(NA_KH, rows)
    qkv = (x @ w_qkv).reshape(bsz, rows, GRID_W, 3, NA_HEADS, NA_DH)
    q = qkv[:, :, :, 0].transpose(1, 0, 3, 2, 4)
    k = qkv[:, :, :, 1].transpose(0, 3, 1, 2, 4)
    v = qkv[:, :, :, 2].transpose(0, 3, 1, 2, 4)
    row_start = jnp.clip(jnp.arange(rows) - kh // 2, 0, rows - kh)
    col = jnp.arange(GRID_W)
    col_start = jnp.clip(col - NA_KW // 2, 0, GRID_W - NA_KW)
    col_mask = (col[None, :] >= col_start[:, None]) & (col[None, :] < col_start[:, None] + NA_KW)
    dc_idx = jnp.clip(col[None, :] - col[:, None] + NA_KW - 1, 0, 2 * NA_KW - 2)
    rpb_cols = rpb[:, :, dc_idx]
    scale = NA_DH ** -0.5

    def attend_row(args):
        q_r, r, rs = args
        k_r = lax.dynamic_slice_in_dim(k, rs, kh, axis=2)
        v_r = lax.dynamic_slice_in_dim(v, rs, kh, axis=2)
        dr_idx = rs + jnp.arange(kh) - r + NA_KH - 1
        bias = rpb_cols[:, dr_idx].transpose(0, 2, 1, 3).astype(jnp.float32)
        s = jnp.einsum('bhqd,bhjkd->bhqjk', q_r, k_r).astype(jnp.float32) * scale + bias[None]
        s = jnp.where(col_mask[None, None, :, None, :], s, -jnp.inf)
        pr = jax.nn.softmax(s.reshape(bsz, NA_HEADS, GRID_W, kh * GRID_W), axis=-1).reshape(s.shape).astype(v.dtype)
        return jnp.einsum('bhqjk,bhjkd->bhqd', pr, v_r)

    out = lax.map(attend_row, (q, jnp.arange(rows), row_start))
    out = out.transpose(1, 0, 3, 2, 4).reshape(bsz, t, NA_WIDTH)
    return out @ w_out


def swiglu(x, w_gate, w_up, w_down):
    return (jax.nn.silu(x @ w_gate) * (x @ w_up)) @ w_down


def moe_swiglu(x, w_router, b_router, w_gate, w_up, w_down):
    bsz, t, d = x.shape
    xt = x.reshape(-1, d)
    n = xt.shape[0]
    nk = n * TOP_K
    logits = (xt @ w_router).astype(jnp.float32) + b_router.astype(jnp.float32)
    top_logit, top_e = lax.top_k(logits, TOP_K)
    gate = jax.nn.softmax(top_logit, axis=-1)
    e_flat = top_e.reshape(-1).astype(jnp.int32)
    slot = jnp.arange(nk, dtype=jnp.int32)
    order = jnp.argsort(e_flat * nk + slot)
    e_sorted = e_flat[order]
    tok_sorted = order // TOP_K
    g_sorted = gate.reshape(-1)[order]
    counts = jnp.bincount(e_flat, length=N_EXPERTS)
    padded = (counts + MOE_BLOCK - 1) // MOE_BLOCK * MOE_BLOCK
    pad_end = jnp.cumsum(padded)
    pad_start = pad_end - padded
    start = jnp.cumsum(counts) - counts
    dest = pad_start[e_sorted] + slot - start[e_sorted]
    n_blocks = -(-(nk + N_EXPERTS * (MOE_BLOCK - 1)) // MOE_BLOCK)
    xs = jnp.zeros((n_blocks * MOE_BLOCK, d), x.dtype).at[dest].set(xt[tok_sorted])
    block_e = jnp.minimum(jnp.searchsorted(pad_end, jnp.arange(n_blocks) * MOE_BLOCK, side='right'), N_EXPERTS - 1)

    def run_block(args):
        xb, e = args
        return swiglu(xb, w_gate[e], w_up[e], w_down[e])

    ys = lax.map(run_block, (xs.reshape(n_blocks, MOE_BLOCK, d), block_e)).reshape(-1, d)
    out = jnp.zeros_like(xt).at[tok_sorted].add(ys[dest] * g_sorted[:, None].astype(x.dtype))
    return out.reshape(bsz, t, d)


def setup_inputs(seed: int = 0) -> dict:
    key = jax.random.key(seed)
    ks = jax.random.split(key, 24)
    f32 = jnp.float32

    def nrm(k, shape, scale):
        return jax.random.normal(k, shape, f32) * scale

    gate_offset = jnp.stack([jnp.full((MLSTM_HEADS,), -1.0, f32), jnp.linspace(3.0, 6.0, MLSTM_HEADS, dtype=f32), jnp.full((MLSTM_HEADS,), -1.0, f32), jnp.linspace(3.0, 6.0, MLSTM_HEADS, dtype=f32)])
    return {
        'x': nrm(ks[0], (BATCH, SEQ, D_MODEL), 1.0),
        'p': nrm(ks[1], (DEPTH, BATCH, SEQ, PLE_DIM), 1.0),
        'ln_w': 1.0 + nrm(ks[2], (DEPTH, 2, D_MODEL), 0.02),
        'ln_b': nrm(ks[3], (DEPTH, 2, D_MODEL), 0.02),
        'rec_w_in': nrm(ks[4], (N_EVEN, D_MODEL, REC_IN), D_MODEL ** -0.5),
        'mlstm_gate_bias': gate_offset[None] + nrm(ks[5], (N_EVEN, 4, MLSTM_HEADS), 0.1),
        'mlstm_norm_w': 1.0 + nrm(ks[6], (N_EVEN, A_WIDTH), 0.02),
        'hgrn_lb_logits': nrm(ks[7], (2, N_EVEN + 1, HGRN_HEADS * HGRN_DK), 0.1),
        'hgrn_norm_w': 1.0 + nrm(ks[8], (N_EVEN, B_WIDTH), 0.02),
        'rec_w_out': nrm(ks[9], (N_EVEN, A_WIDTH + B_WIDTH, D_MODEL), BETA * (A_WIDTH + B_WIDTH) ** -0.5),
        'ffn_w_gate': nrm(ks[10], (N_EVEN, D_MODEL, FFN_DIM), D_MODEL ** -0.5),
        'ffn_w_up': nrm(ks[11], (N_EVEN, D_MODEL, FFN_DIM), D_MODEL ** -0.5),
        'ffn_w_down': nrm(ks[12], (N_EVEN, FFN_DIM, D_MODEL), BETA * FFN_DIM ** -0.5),
        'na_w_qkv': nrm(ks[13], (N_ODD, D_MODEL, 3 * NA_WIDTH), D_MODEL ** -0.5),
        'na_rpb': nrm(ks[14], (N_ODD, NA_HEADS, 2 * NA_KH - 1, 2 * NA_KW - 1), 0.1),
        'na_w_out': nrm(ks[15], (N_ODD, NA_WIDTH, D_MODEL), BETA * NA_WIDTH ** -0.5),
        'moe_w_router': nrm(ks[16], (N_ODD, D_MODEL, N_EXPERTS), D_MODEL ** -0.5),
        'moe_b_router': nrm(ks[17], (N_ODD, N_EXPERTS), 0.01),
        'moe_w_gate': nrm(ks[18], (N_ODD, N_EXPERTS, D_MODEL, MOE_DIM), D_MODEL ** -0.5),
        'moe_w_up': nrm(ks[19], (N_ODD, N_EXPERTS, D_MODEL, MOE_DIM), D_MODEL ** -0.5),
        'moe_w_down': nrm(ks[20], (N_ODD, N_EXPERTS, MOE_DIM, D_MODEL), BETA * MOE_DIM ** -0.5),
        'ple_w_gate': nrm(ks[21], (DEPTH, D_MODEL, D_MODEL), D_MODEL ** -0.5),
        'ple_w_proj': nrm(ks[22], (DEPTH, PLE_DIM, D_MODEL), PLE_DIM ** -0.5),
    }


def reference(x, p, ln_w, ln_b, rec_w_in, mlstm_gate_bias, mlstm_norm_w, hgrn_lb_logits, hgrn_norm_w, rec_w_out, ffn_w_gate, ffn_w_up, ffn_w_down, na_w_qkv, na_rpb, na_w_out, moe_w_router, moe_b_router, moe_w_gate, moe_w_up, moe_w_down, ple_w_gate, ple_w_proj):
    lower_bounds = jnp.cumsum(jax.nn.softmax(hgrn_lb_logits.astype(jnp.float32), axis=1), axis=1)
    for i in range(DEPTH):
        j = i // 2
        if i % 2 == 0:
            mix = recurrent_mixer(x, rec_w_in[j], mlstm_gate_bias[j], mlstm_norm_w[j], lower_bounds[:, j], hgrn_norm_w[j], rec_w_out[j])
        else:
            mix = neighborhood_attention(x, na_w_qkv[j], na_rpb[j], na_w_out[j])
        x = layer_norm(ALPHA * x + mix, ln_w[i, 0], ln_b[i, 0])
        if i % 2 == 0:
            ffn = swiglu(x, ffn_w_gate[j], ffn_w_up[j], ffn_w_down[j])
        else:
            ffn = moe_swiglu(x, moe_w_router[j], moe_b_router[j], moe_w_gate[j], moe_w_up[j], moe_w_down[j])
        x = layer_norm(ALPHA * x + ffn, ln_w[i, 1], ln_b[i, 1])
        x = x + jax.nn.sigmoid(x @ ple_w_gate[i]) * (p[i] @ ple_w_proj[i])
    return x
```

```python
import functools

import jax
import jax.numpy as jnp
from jax import lax
from jax.experimental import pallas as pl
from jax.experimental.pallas import tpu as pltpu

F32 = jnp.float32
BF16 = jnp.bfloat16
HIGHEST = lax.Precision.HIGHEST

DEPTH = 2
ALPHA = (2 * DEPTH) ** 0.25
LN_EPS = 1e-5
RMS_EPS = 1e-6
GRID_W = 64

MLSTM_HEADS = 4
MLSTM_DK = 128
MLSTM_DV = 256
GATE_CAP = 15.0
HGRN_HEADS = 8
HGRN_DK = 128
A_WIDTH = MLSTM_HEADS * MLSTM_DV
B_WIDTH = HGRN_HEADS * HGRN_DK

NA_DH = 128
NA_HEADS = 16
NA_KH = 8
NA_KW = 16

N_EXPERTS = 8
TOP_K = 2

MLSTM_CHUNK = 128
HGRN_CHUNK = 32
MOE_TILE = 1024

V7X_LANES = 128
VMEM_LIMIT = 56 * 1024 * 1024
NEG_BIG = -1e30


def _params(sem):
    return pltpu.CompilerParams(dimension_semantics=sem, vmem_limit_bytes=VMEM_LIMIT)


def _dot(a, b, **kw):
    return jnp.dot(a, b, preferred_element_type=F32, **kw)


def _dot_nt(a, b):
    return lax.dot_general(a, b, (((1,), (1,)), ((), ())), preferred_element_type=F32)


def _dot_tn(a, b):
    return lax.dot_general(a, b, (((0,), (0,)), ((), ())), preferred_element_type=F32)


def _layer_norm(y, w, b):
    mu = jnp.mean(y, axis=-1, keepdims=True)
    yc = y - mu
    var = jnp.mean(yc * yc, axis=-1, keepdims=True)
    return yc * lax.rsqrt(var + LN_EPS) * w + b


def _log_sigmoid(z):
    return jnp.minimum(z, 0.0) - jnp.log(1.0 + jnp.exp(-jnp.abs(z)))


def _mm_body(a_ref, w_ref, o_ref):
    o_ref[...] = _dot(a_ref[...], w_ref[...]).astype(o_ref.dtype)


def _matmul(a, w, out_dtype, tm, tn):
    m, k = a.shape
    n = w.shape[1]
    return pl.pallas_call(
        _mm_body,
        grid=(m // tm, n // tn),
        in_specs=[pl.BlockSpec((tm, k), lambda i, j: (i, 0)), pl.BlockSpec((k, tn), lambda i, j: (0, j))],
        out_specs=pl.BlockSpec((tm, tn), lambda i, j: (i, j)),
        out_shape=jax.ShapeDtypeStruct((m, n), out_dtype),
        compiler_params=_params(("parallel", "arbitrary")),
        name="matmul",
    )(a, w)


def _mm_ln_body(a_ref, w_ref, res_ref, g_ref, b_ref, o_ref, ob_ref, acc_ref, *, nk):
    kk = pl.program_id(1)
    part = _dot(a_ref[...], w_ref[...])

    @pl.when(kk == 0)
    def _():
        acc_ref[...] = part

    @pl.when(kk > 0)
    def _():
        acc_ref[...] += part

    @pl.when(kk == nk - 1)
    def _():
        out = _layer_norm(ALPHA * res_ref[...] + acc_ref[...], g_ref[...], b_ref[...])
        o_ref[...] = out
        ob_ref[...] = out.astype(BF16)


def _matmul_ln(a, w, res, ln_w, ln_b, tm, tk):
    m, k = a.shape
    n = w.shape[1]
    nk = k // tk
    return pl.pallas_call(
        functools.partial(_mm_ln_body, nk=nk),
        grid=(m // tm, nk),
        in_specs=[
            pl.BlockSpec((tm, tk), lambda i, j: (i, j)),
            pl.BlockSpec((tk, n), lambda i, j: (j, 0)),
            pl.BlockSpec((tm, n), lambda i, j: (i, 0)),
            pl.BlockSpec((1, n), lambda i, j: (0, 0)),
            pl.BlockSpec((1, n), lambda i, j: (0, 0)),
        ],
        out_specs=[pl.BlockSpec((tm, n), lambda i, j: (i, 0)), pl.BlockSpec((tm, n), lambda i, j: (i, 0))],
        out_shape=[jax.ShapeDtypeStruct((m, n), F32), jax.ShapeDtypeStruct((m, n), BF16)],
        scratch_shapes=[pltpu.VMEM((tm, n), F32)],
        compiler_params=_params(("parallel", "arbitrary")),
        name="matmul_ln",
    )(a, w, res, ln_w.reshape(1, n), ln_b.reshape(1, n))


def _swiglu_up_body(a_ref, wg_ref, wu_ref, o_ref):
    a = a_ref[...]
    g = _dot(a, wg_ref[...])
    u = _dot(a, wu_ref[...])
    o_ref[...] = (g * jax.nn.sigmoid(g) * u).astype(o_ref.dtype)


def _swiglu_up(a, wg, wu, tm, tf):
    m, k = a.shape
    f = wg.shape[1]
    return pl.pallas_call(
        _swiglu_up_body,
        grid=(m // tm, f // tf),
        in_specs=[
            pl.BlockSpec((tm, k), lambda i, j: (i, 0)),
            pl.BlockSpec((k, tf), lambda i, j: (0, j)),
            pl.BlockSpec((k, tf), lambda i, j: (0, j)),
        ],
        out_specs=pl.BlockSpec((tm, tf), lambda i, j: (i, j)),
        out_shape=jax.ShapeDtypeStruct((m, f), BF16),
        compiler_params=_params(("parallel", "arbitrary")),
        name="swiglu_up",
    )(a, wg, wu)


def _ple_body(xb_ref, wg_ref, p_ref, wp_ref, x_ref, o_ref, ob_ref):
    gate = jax.nn.sigmoid(_dot(xb_ref[...], wg_ref[...]))
    proj = _dot(p_ref[...].astype(BF16), wp_ref[...])
    out = x_ref[...] + gate * proj
    o_ref[...] = out
    ob_ref[...] = out.astype(BF16)


def _ple(x, xb, p, wg, wp, tm, tn):
    m, d = x.shape
    pd = p.shape[1]
    return pl.pallas_call(
        _ple_body,
        grid=(m // tm, d // tn),
        in_specs=[
            pl.BlockSpec((tm, d), lambda i, j: (i, 0)),
            pl.BlockSpec((d, tn), lambda i, j: (0, j)),
            pl.BlockSpec((tm, pd), lambda i, j: (i, 0)),
            pl.BlockSpec((pd, tn), lambda i, j: (0, j)),
            pl.BlockSpec((tm, tn), lambda i, j: (i, j)),
        ],
        out_specs=[pl.BlockSpec((tm, tn), lambda i, j: (i, j)), pl.BlockSpec((tm, tn), lambda i, j: (i, j))],
        out_shape=[jax.ShapeDtypeStruct((m, d), F32), jax.ShapeDtypeStruct((m, d), BF16)],
        compiler_params=_params(("parallel", "arbitrary")),
        name="ple",
    )(xb, wg, p, wp, x)


def _mlstm_body(q_ref, k_ref, v_ref, og_ref, gc_ref, gr_ref, bc_ref, br_ref, nw_ref, y_ref, hs_ref, c_ref, *, seq):
    L = MLSTM_CHUNK
    nc = seq // L
    row = lax.broadcasted_iota(jnp.int32, (L, L), 0)
    col = lax.broadcasted_iota(jnp.int32, (L, L), 1)
    lower = row >= col
    upper = row <= col
    tril = lower.astype(F32)
    triu = upper.astype(F32)
    hs_ref[...] = jnp.zeros_like(hs_ref)
    c_ref[...] = jnp.zeros_like(c_ref)

    def cap(z):
        return GATE_CAP * jnp.tanh(z / GATE_CAP)

    def one_dir(cidx, d, n, m):
        rows = pl.ds(pl.multiple_of(cidx * L, L), L)
        gcol = cap(gc_ref[cidx] + bc_ref[...])
        grow = cap(gr_ref[cidx] + br_ref[...])
        lcol = _log_sigmoid(gcol)
        lrow = _log_sigmoid(grow)
        if d == 0:
            bcol = _dot(tril, lcol, precision=HIGHEST)
            brow = _dot(lrow, triu, precision=HIGHEST)
            mask = lower
        else:
            bcol = _dot(triu, lcol, precision=HIGHEST)
            brow = _dot(lrow, tril, precision=HIGHEST)
            mask = upper
        li_col = gcol[:, 2 * d:2 * d + 1]
        b_col = bcol[:, 2 * d + 1:2 * d + 2]
        li_row = grow[2 * d:2 * d + 1, :]
        b_row = brow[2 * d + 1:2 * d + 2, :]
        g = b_col[L - 1:L, :] if d == 0 else b_col[0:1, :]

        q = q_ref[rows, :] * (MLSTM_DK ** -0.5)
        k = k_ref[rows, :]
        qb = q.astype(BF16)
        kb = k.astype(BF16)
        vb = v_ref[rows, :].astype(BF16)

        logd = jnp.where(mask, b_col + (li_row - b_row), -jnp.inf)
        m_t = jnp.maximum(jnp.max(logd, axis=1, keepdims=True), b_col + m)
        s = _dot_nt(qb, kb) * jnp.exp(logd - m_t)
        inter_w = jnp.exp(b_col + m - m_t)
        c_old = c_ref[d]
        num = _dot(s.astype(BF16), vb) + inter_w * _dot(qb, c_old.astype(BF16))
        den = jnp.sum(s, axis=1, keepdims=True) + inter_w * jnp.sum(q * n, axis=1, keepdims=True)
        h = num / jnp.maximum(jnp.abs(den), jnp.exp(-m_t))
        hs_ref[rows, :] += h

        a_col = g - b_col + li_col
        m_loc = jnp.max(a_col, axis=0, keepdims=True)
        kw = k * jnp.exp(a_col - m_loc)
        c_loc = _dot_tn(kw.astype(BF16), vb)
        n_loc = jnp.sum(kw, axis=0, keepdims=True)
        m_new = jnp.maximum(g + m, m_loc)
        decay = jnp.exp(g + m - m_new)
        inj = jnp.exp(m_loc - m_new)
        c_ref[d] = decay * c_old + inj * c_loc
        return decay * n + inj * n_loc, m_new

    def body(i, carry):
        nf, mf, nb, mb = carry
        nf, mf = one_dir(i, 0, nf, mf)
        nb, mb = one_dir(nc - 1 - i, 1, nb, mb)
        return nf, mf, nb, mb

    zn = jnp.zeros((1, MLSTM_DK), F32)
    zm = jnp.zeros((1, 1), F32)
    lax.fori_loop(0, nc, body, (zn, zm, zn, zm))

    def fin(i, carry):
        rows = pl.ds(pl.multiple_of(i * L, L), L)
        hh = hs_ref[rows, :]
        r = lax.rsqrt(jnp.mean(hh * hh, axis=-1, keepdims=True) + RMS_EPS)
        y_ref[rows, :] = (hh * r * nw_ref[...] * jax.nn.sigmoid(og_ref[rows, :])).astype(y_ref.dtype)
        return carry

    lax.fori_loop(0, nc, fin, 0)


def _mlstm(u, gc, gr, bias_c, bias_r, norm_w):
    bsz, seq, _ = u.shape
    L = MLSTM_CHUNK
    nc = seq // L
    H, dk, dv = MLSTM_HEADS, MLSTM_DK, MLSTM_DV
    k_off = H * dk // dk
    v_off = 2 * H * dk // dv
    o_off = v_off + H
    return pl.pallas_call(
        functools.partial(_mlstm_body, seq=seq),
        grid=(bsz, H),
        in_specs=[
            pl.BlockSpec((None, seq, dk), lambda b, h: (b, 0, h)),
            pl.BlockSpec((None, seq, dk), lambda b, h: (b, 0, k_off + h)),
            pl.BlockSpec((None, seq, dv), lambda b, h: (b, 0, v_off + h)),
            pl.BlockSpec((None, seq, dv), lambda b, h: (b, 0, o_off + h)),
            pl.BlockSpec((None, None, nc, L, 4), lambda b, h: (b, h, 0, 0, 0)),
            pl.BlockSpec((None, None, nc, 4, L), lambda b, h: (b, h, 0, 0, 0)),
            pl.BlockSpec((None, 1, 4), lambda b, h: (h, 0, 0)),
            pl.BlockSpec((None, 4, 1), lambda b, h: (h, 0, 0)),
            pl.BlockSpec((1, dv), lambda b, h: (0, h)),
        ],
        out_specs=pl.BlockSpec((None, seq, dv), lambda b, h: (b, 0, h)),
        out_shape=jax.ShapeDtypeStruct((bsz, seq, H * dv), BF16),
        scratch_shapes=[pltpu.VMEM((seq, dv), F32), pltpu.VMEM((2, dk, dv), F32)],
        compiler_params=_params(("parallel", "arbitrary")),
        name="mlstm",
    )(u, u, u, u, gc, gr, bias_c, bias_r, norm_w.reshape(1, H * dv))


def _hgrn_body(q_ref, ff_ref, fb_ref, i_ref, g_ref, lbl_ref, nw_ref, y_ref, os_ref, st_ref, *, seq, layer):
    L = HGRN_CHUNK
    nc = seq // L
    row = lax.broadcasted_iota(jnp.int32, (L, L), 0)
    col = lax.broadcasted_iota(jnp.int32, (L, L), 1)
    lower = row >= col
    upper = row <= col
    tril = lower.astype(F32)
    triu = upper.astype(F32)
    os_ref[...] = jnp.zeros_like(os_ref)
    st_ref[...] = jnp.zeros_like(st_ref)

    def lower_bound(d):
        lg = lbl_ref[d]
        e = jnp.exp(lg - jnp.max(lg, axis=0, keepdims=True))
        sm = e / jnp.sum(e, axis=0, keepdims=True)
        return jnp.sum(sm[:layer + 1, :], axis=0, keepdims=True)

    lbs = (lower_bound(0), lower_bound(1))

    def one_dir(cidx, d):
        rows = pl.ds(pl.multiple_of(cidx * L, L), L)
        qr = q_ref[rows, :]
        q = qr * jax.nn.sigmoid(qr)
        vb = i_ref[rows, :].astype(BF16)
        fr = (ff_ref if d == 0 else fb_ref)[rows, :]
        lb = lbs[d]
        f = lb + (1.0 - lb) * jax.nn.sigmoid(fr)
        k = 1.0 - f
        lf = jnp.log(f)
        if d == 0:
            b = _dot(tril, lf, precision=HIGHEST)
            g = b[L - 1:L, :]
            mask = lower
        else:
            b = _dot(triu, lf, precision=HIGHEST)
            g = b[0:1, :]
            mask = upper
        bm = b[L // 2:L // 2 + 1, :]
        qi = (q * jnp.exp(b)).astype(BF16)
        qm = (q * jnp.exp(b - bm)).astype(BF16)
        km = (k * jnp.exp(bm - b)).astype(BF16)
        ke = (k * jnp.exp(g - b)).astype(BF16)
        a = jnp.where(mask, _dot_nt(qm, km), 0.0)
        st = st_ref[d]
        o = _dot(a.astype(BF16), vb) + _dot_nt(qi, st.astype(BF16))
        st_ref[d] = st * jnp.exp(g) + _dot_tn(vb, ke)
        os_ref[rows, :] += o

    def body(i, carry):
        one_dir(i, 0)
        one_dir(nc - 1 - i, 1)
        return carry

    lax.fori_loop(0, nc, body, 0)

    FL = 256

    def fin(i, carry):
        rows = pl.ds(pl.multiple_of(i * FL, FL), FL)
        hh = os_ref[rows, :]
        r = lax.rsqrt(jnp.mean(hh * hh, axis=-1, keepdims=True) + RMS_EPS)
        gg = g_ref[rows, :]
        y_ref[rows, :] = (hh * r * nw_ref[...] * (gg * jax.nn.sigmoid(gg))).astype(y_ref.dtype)
        return carry

    lax.fori_loop(0, seq // FL, fin, 0)


def _hgrn(u, lb_logits, norm_w, layer):
    bsz, seq, _ = u.shape
    H, dk = HGRN_HEADS, HGRN_DK
    base = (2 * MLSTM_HEADS * MLSTM_DK + 2 * A_WIDTH) // dk
    slots = lb_logits.shape[1]
    return pl.pallas_call(
        functools.partial(_hgrn_body, seq=seq, layer=layer),
        grid=(bsz, H),
        in_specs=[
            pl.BlockSpec((None, seq, dk), lambda b, h: (b, 0, base + h)),
            pl.BlockSpec((None, seq, dk), lambda b, h: (b, 0, base + H + h)),
            pl.BlockSpec((None, seq, dk), lambda b, h: (b, 0, base + 2 * H + h)),
            pl.BlockSpec((None, seq, dk), lambda b, h: (b, 0, base + 3 * H + h)),
            pl.BlockSpec((None, seq, dk), lambda b, h: (b, 0, base + 4 * H + h)),
            pl.BlockSpec((2, slots, dk), lambda b, h: (0, 0, h)),
            pl.BlockSpec((1, dk), lambda b, h: (0, h)),
        ],
        out_specs=pl.BlockSpec((None, seq, dk), lambda b, h: (b, 0, h)),
        out_shape=jax.ShapeDtypeStruct((bsz, seq, H * dk), BF16),
        scratch_shapes=[pltpu.VMEM((seq, dk), F32), pltpu.VMEM((2, dk, dk), F32)],
        compiler_params=_params(("parallel", "arbitrary")),
        name="hgrn2",
    )(u, u, u, u, u, lb_logits, norm_w.reshape(1, H * dk))


def _na_body(q_ref, k_ref, v_ref, bias_ref, o_ref, *, rows):
    W = GRID_W
    kh = NA_KH
    scale = NA_DH ** -0.5

    def body(r, carry):
        rs = jnp.clip(r - kh // 2, 0, rows - kh)
        d0 = rs - r + (NA_KH - 1)
        q = q_ref[pl.ds(pl.multiple_of(r * W, W), W), :]
        kwin = pl.ds(pl.multiple_of(rs * W, W), kh * W)
        s = _dot_nt(q, k_ref[kwin, :]) * scale + bias_ref[d0]
        m = jnp.max(s, axis=-1, keepdims=True)
        p = jnp.exp(s - m)
        den = jnp.sum(p, axis=-1, keepdims=True)
        o = _dot(p.astype(BF16), v_ref[kwin, :]) / den
        o_ref[pl.ds(pl.multiple_of(r * W, W), W), :] = o.astype(o_ref.dtype)
        return carry

    lax.fori_loop(0, rows, body, 0)


def _na_bias_table(rpb):
    W = GRID_W
    colv = jnp.arange(W)
    col_start = jnp.clip(colv - NA_KW // 2, 0, W - NA_KW)
    col_mask = (colv[None, :] >= col_start[:, None]) & (colv[None, :] < col_start[:, None] + NA_KW)
    dc_idx = jnp.clip(colv[None, :] - colv[:, None] + NA_KW - 1, 0, 2 * NA_KW - 2)
    rpb_cols = jnp.where(col_mask[None, None], rpb.astype(F32)[:, :, dc_idx], NEG_BIG)
    dr = jnp.arange(NA_KH)[:, None] + jnp.arange(NA_KH)[None, :]
    tbl = rpb_cols[:, dr]
    return tbl.transpose(0, 1, 3, 2, 4).reshape(NA_HEADS, NA_KH, W, NA_KH * W)


def _na(qkv, bias_tbl):
    bsz, seq, _ = qkv.shape
    rows = seq // GRID_W
    H, dh = NA_HEADS, NA_DH
    return pl.pallas_call(
        functools.partial(_na_body, rows=rows),
        grid=(bsz, H),
        in_specs=[
            pl.BlockSpec((None, seq, dh), lambda b, h: (b, 0, h)),
            pl.BlockSpec((None, seq, dh), lambda b, h: (b, 0, H + h)),
            pl.BlockSpec((None, seq, dh), lambda b, h: (b, 0, 2 * H + h)),
            pl.BlockSpec((None, NA_KH, GRID_W, NA_KH * GRID_W), lambda b, h: (h, 0, 0, 0)),
        ],
        out_specs=pl.BlockSpec((None, seq, dh), lambda b, h: (b, 0, h)),
        out_shape=jax.ShapeDtypeStruct((bsz, seq, H * dh), BF16),
        compiler_params=_params(("parallel", "arbitrary")),
        name="natten",
    )(qkv, qkv, qkv, bias_tbl)


def _router_body(x_ref, w_ref, b_ref, e_ref, g_ref):
    logits = _dot(x_ref[...], w_ref[...], precision=HIGHEST) + b_ref[...]
    lane = lax.broadcasted_iota(jnp.int32, logits.shape, 1)
    nl = logits.shape[1]
    m1 = jnp.max(logits, axis=-1, keepdims=True)
    i1 = jnp.min(jnp.where(logits == m1, lane, nl), axis=-1, keepdims=True)
    rest = jnp.where(lane == i1, -jnp.inf, logits)
    m2 = jnp.max(rest, axis=-1, keepdims=True)
    i2 = jnp.min(jnp.where(rest == m2, lane, nl), axis=-1, keepdims=True)
    ex = jnp.exp(m2 - m1)
    g1 = 1.0 / (1.0 + ex)
    g2 = ex / (1.0 + ex)
    e_ref[...] = jnp.where(lane == 0, i1, jnp.where(lane == 1, i2, 0))
    g_ref[...] = jnp.where(lane == 0, g1, jnp.where(lane == 1, g2, 0.0))


def _router(x, w_router, b_router, tm):
    n, d = x.shape
    ne = w_router.shape[1]
    w = jnp.zeros((d, V7X_LANES), F32).at[:, :ne].set(w_router.astype(F32))
    b = jnp.full((1, V7X_LANES), NEG_BIG, F32).at[0, :ne].set(b_router.astype(F32))
    return pl.pallas_call(
        _router_body,
        grid=(n // tm,),
        in_specs=[
            pl.BlockSpec((tm, d), lambda i: (i, 0)),
            pl.BlockSpec((d, V7X_LANES), lambda i: (0, 0)),
            pl.BlockSpec((1, V7X_LANES), lambda i: (0, 0)),
        ],
        out_specs=[pl.BlockSpec((tm, V7X_LANES), lambda i: (i, 0)), pl.BlockSpec((tm, V7X_LANES), lambda i: (i, 0))],
        out_shape=[jax.ShapeDtypeStruct((n, V7X_LANES), jnp.int32), jax.ShapeDtypeStruct((n, V7X_LANES), F32)],
        compiler_params=_params(("parallel",)),
        name="router_top2",
    )(x, w, b)


def _moe_body(te_ref, na_ref, x_ref, wg_ref, wu_ref, wd_ref, o_ref, *, nf):
    t = pl.program_id(0)
    f = pl.program_id(1)
    active = t < na_ref[0]

    @pl.when(active)
    def _():
        x = x_ref[...]
        g = _dot(x, wg_ref[...])
        u = _dot(x, wu_ref[...])
        h = (g * jax.nn.sigmoid(g) * u).astype(BF16)
        part = _dot(h, wd_ref[...])

        @pl.when(f == 0)
        def _():
            o_ref[...] = part

        @pl.when(f > 0)
        def _():
            o_ref[...] += part

    @pl.when(jnp.logical_and(jnp.logical_not(active), f == nf - 1))
    def _():
        o_ref[...] = jnp.zeros_like(o_ref)


def _moe_experts(xs, tile_expert, n_active, wg, wu, wd, tm, tf):
    rows, d = xs.shape
    n_tiles = rows // tm
    fdim = wg.shape[2]
    nf = fdim // tf

    def fidx(t, f, na):
        return jnp.where(t < na[0], f, nf - 1)

    return pl.pallas_call(
        functools.partial(_moe_body, nf=nf),
        grid_spec=pltpu.PrefetchScalarGridSpec(
            num_scalar_prefetch=2,
            grid=(n_tiles, nf),
            in_specs=[
                pl.BlockSpec((tm, d), lambda t, f, te, na: (t, 0)),
                pl.BlockSpec((None, d, tf), lambda t, f, te, na: (te[t], 0, fidx(t, f, na))),
                pl.BlockSpec((None, d, tf), lambda t, f, te, na: (te[t], 0, fidx(t, f, na))),
                pl.BlockSpec((None, tf, d), lambda t, f, te, na: (te[t], fidx(t, f, na), 0)),
            ],
            out_specs=pl.BlockSpec((tm, d), lambda t, f, te, na: (t, 0)),
        ),
        out_shape=jax.ShapeDtypeStruct((rows, d), F32),
        compiler_params=_params(("arbitrary", "arbitrary")),
        name="moe_experts",
    )(tile_expert, n_active, xs, wg, wu, wd)


def _combine_ln_body(x_ref, y_ref, g_ref, w_ref, b_ref, o_ref, ob_ref):
    d = x_ref.shape[1]
    g = g_ref[...]
    mix = y_ref[:, :d] * g[:, 0:1] + y_ref[:, d:] * g[:, 1:2]
    out = _layer_norm(ALPHA * x_ref[...] + mix, w_ref[...], b_ref[...])
    o_ref[...] = out
    ob_ref[...] = out.astype(BF16)


def _combine_ln(x, y2, gates, ln_w, ln_b, tm):
    n, d = x.shape
    return pl.pallas_call(
        _combine_ln_body,
        grid=(n // tm,),
        in_specs=[
            pl.BlockSpec((tm, d), lambda i: (i, 0)),
            pl.BlockSpec((tm, 2 * d), lambda i: (i, 0)),
            pl.BlockSpec((tm, V7X_LANES), lambda i: (i, 0)),
            pl.BlockSpec((1, d), lambda i: (0, 0)),
            pl.BlockSpec((1, d), lambda i: (0, 0)),
        ],
        out_specs=[pl.BlockSpec((tm, d), lambda i: (i, 0)), pl.BlockSpec((tm, d), lambda i: (i, 0))],
        out_shape=[jax.ShapeDtypeStruct((n, d), F32), jax.ShapeDtypeStruct((n, d), BF16)],
        compiler_params=_params(("parallel",)),
        name="combine_ln",
    )(x, y2, gates, ln_w.reshape(1, d), ln_b.reshape(1, d))


def _moe(x, xb, w_router, b_router, wg, wu, wd, ln_w, ln_b):
    n, d = x.shape
    tm = MOE_TILE
    nk = n * TOP_K
    e_out, g_out = _router(x, w_router, b_router, 512)
    e_flat = e_out[:, :TOP_K].reshape(-1)
    onehot = (e_flat[:, None] == jnp.arange(N_EXPERTS, dtype=jnp.int32)[None, :]).astype(jnp.int32)
    csum = jnp.cumsum(onehot, axis=0)
    counts = csum[-1]
    pos = jnp.take_along_axis(csum, e_flat[:, None], axis=1)[:, 0] - 1
    padded = (counts + tm - 1) // tm * tm
    pad_end = jnp.cumsum(padded)
    pad_start = pad_end - padded
    dest = pad_start[e_flat] + pos
    n_tiles = nk // tm + N_EXPERTS
    row_tok = jnp.zeros((n_tiles * tm,), jnp.int32).at[dest].set(jnp.arange(nk, dtype=jnp.int32) // TOP_K)
    n_active = (pad_end[-1] // tm).astype(jnp.int32)
    tile_ids = jnp.minimum(jnp.arange(n_tiles, dtype=jnp.int32), n_active - 1)
    tile_expert = jnp.minimum(jnp.searchsorted(pad_end, tile_ids * tm, side='right'), N_EXPERTS - 1).astype(jnp.int32)
    xs = xb[row_tok]
    ys = _moe_experts(xs, tile_expert, n_active.reshape(1), wg, wu, wd, tm, 512)
    y2 = ys[dest].reshape(n, TOP_K * d)
    return _combine_ln(x, y2, g_out, ln_w, ln_b, 256)


def kernel(x, p, ln_w, ln_b, rec_w_in, mlstm_gate_bias, mlstm_norm_w, hgrn_lb_logits, hgrn_norm_w, rec_w_out, ffn_w_gate, ffn_w_up, ffn_w_down, na_w_qkv, na_rpb, na_w_out, moe_w_router, moe_b_router, moe_w_gate, moe_w_up, moe_w_down, ple_w_gate, ple_w_proj):
    bsz, seq, d = x.shape
    n = bsz * seq
    depth = ln_w.shape[0]
    xf = x.reshape(n, d).astype(F32)
    xb = xf.astype(BF16)
    H = MLSTM_HEADS
    gate_lo = 2 * H * MLSTM_DK + 2 * A_WIDTH
    gate_hi = gate_lo + 4 * H
    for i in range(depth):
        j = i // 2
        if i % 2 == 0:
            w_in = rec_w_in[j]
            w_main = jnp.concatenate([w_in[:, :gate_lo], w_in[:, gate_hi:]], axis=1).astype(BF16)
            w_gate = jnp.zeros((d, V7X_LANES), BF16).at[:, :4 * H].set(w_in[:, gate_lo:gate_hi].astype(BF16))
            u = _matmul(xb, w_main, F32, 1024, 512).reshape(bsz, seq, -1)
            graw = _matmul(xb, w_gate, F32, 1024, V7X_LANES)[:, :4 * H].reshape(bsz, seq, 4, H)
            L = MLSTM_CHUNK
            gc = graw.transpose(0, 3, 1, 2).reshape(bsz, H, seq // L, L, 4)
            gr = gc.transpose(0, 1, 2, 4, 3)
            bias = mlstm_gate_bias[j].astype(F32).T
            y_a = _mlstm(u, gc, gr, bias.reshape(H, 1, 4), bias.reshape(H, 4, 1), mlstm_norm_w[j].astype(F32))
            y_b = _hgrn(u, hgrn_lb_logits.astype(F32), hgrn_norm_w[j].astype(F32), j)
            mix_in = jnp.concatenate([y_a, y_b], axis=-1).reshape(n, -1)
            xf, xb = _matmul_ln(mix_in, rec_w_out[j].astype(BF16), xf, ln_w[i, 0], ln_b[i, 0], 512, 2048)
            hid = _swiglu_up(xb, ffn_w_gate[j].astype(BF16), ffn_w_up[j].astype(BF16), 1024, 512)
            xf, xb = _matmul_ln(hid, ffn_w_down[j].astype(BF16), xf, ln_w[i, 1], ln_b[i, 1], 512, 512)
        else:
            qkv = _matmul(xb, na_w_qkv[j].astype(BF16), BF16, 1024, 512).reshape(bsz, seq, -1)
            att = _na(qkv, _na_bias_table(na_rpb[j])).reshape(n, -1)
            xf, xb = _matmul_ln(att, na_w_out[j].astype(BF16), xf, ln_w[i, 0], ln_b[i, 0], 512, 2048)
            xf, xb = _moe(xf, xb, moe_w_router[j], moe_b_router[j], moe_w_gate[j].astype(BF16), moe_w_up[j].astype(BF16), moe_w_down[j].astype(BF16), ln_w[i, 1], ln_b[i, 1])
        xf, xb = _ple(xf, xb, p[i].reshape(n, -1), ple_w_gate[i].astype(BF16), ple_w_proj[i].astype(BF16), 1024, 512)
    return xf.reshape(bsz, seq, d)
```

```python
import functools

import jax
import jax.numpy as jnp
import numpy as np
from jax import lax
from jax.experimental import pallas as pl
from jax.experimental.pallas import tpu as pltpu

F32 = jnp.float32
BF16 = jnp.bfloat16
HIGHEST = lax.Precision.HIGHEST

DEPTH = 2
ALPHA = (2 * DEPTH) ** 0.25
LN_EPS = 1e-5
RMS_EPS = 1e-6
GRID_W = 64

MLSTM_HEADS = 4
MLSTM_DK = 128
MLSTM_DV = 256
GATE_CAP = 15.0
HGRN_HEADS = 8
HGRN_DK = 128
A_WIDTH = MLSTM_HEADS * MLSTM_DV
B_WIDTH = HGRN_HEADS * HGRN_DK

NA_DH = 128
NA_HEADS = 16
NA_KH = 8
NA_KW = 16

N_EXPERTS = 8
TOP_K = 2

MLSTM_CHUNK = 128
HGRN_CHUNK = 128
HGRN_SUB = 32
MOE_TILE = 1024
MOE_SUB = 256
NA_GROUP = 4
NA_UNION = NA_KH + NA_GROUP - 1

V7X_LANES = 128
VMEM_LIMIT = 56 * 1024 * 1024
MOE_VMEM_LIMIT = 60 * 1024 * 1024
NEG_BIG = -1e30


def _params(sem):
    return pltpu.CompilerParams(dimension_semantics=sem, vmem_limit_bytes=VMEM_LIMIT)


def _dot(a, b, **kw):
    return jnp.dot(a, b, preferred_element_type=F32, **kw)


def _dot_nt(a, b):
    return lax.dot_general(a, b, (((1,), (1,)), ((), ())), preferred_element_type=F32)


def _dot_tn(a, b):
    return lax.dot_general(a, b, (((0,), (0,)), ((), ())), preferred_element_type=F32)


def _layer_norm(y, w, b):
    mu = jnp.mean(y, axis=-1, keepdims=True)
    yc = y - mu
    var = jnp.mean(yc * yc, axis=-1, keepdims=True)
    return yc * lax.rsqrt(var + LN_EPS) * w + b


def _log_sigmoid(z):
    return jnp.minimum(z, 0.0) - jnp.log(1.0 + jnp.exp(-jnp.abs(z)))


def _mm_body(a_ref, w_ref, o_ref):
    o_ref[...] = _dot(a_ref[...], w_ref[...]).astype(o_ref.dtype)


def _matmul(a, w, out_dtype, tm, tn):
    m, k = a.shape
    n = w.shape[1]
    return pl.pallas_call(
        _mm_body,
        grid=(m // tm, n // tn),
        in_specs=[pl.BlockSpec((tm, k), lambda i, j: (i, 0)), pl.BlockSpec((k, tn), lambda i, j: (0, j))],
        out_specs=pl.BlockSpec((tm, tn), lambda i, j: (i, j)),
        out_shape=jax.ShapeDtypeStruct((m, n), out_dtype),
        compiler_params=_params(("parallel", "arbitrary")),
        name="matmul",
    )(a, w)


def _mm_ln_body(a_ref, w_ref, res_ref, g_ref, b_ref, o_ref, ob_ref, acc_ref, *, nk):
    kk = pl.program_id(1)
    part = _dot(a_ref[...], w_ref[...])

    @pl.when(kk == 0)
    def _():
        acc_ref[...] = part

    @pl.when(kk > 0)
    def _():
        acc_ref[...] += part

    @pl.when(kk == nk - 1)
    def _():
        out = _layer_norm(ALPHA * res_ref[...] + acc_ref[...], g_ref[...], b_ref[...])
        o_ref[...] = out
        ob_ref[...] = out.astype(BF16)


def _matmul_ln(a, w, res, ln_w, ln_b, tm, tk):
    m, k = a.shape
    n = w.shape[1]
    nk = k // tk
    return pl.pallas_call(
        functools.partial(_mm_ln_body, nk=nk),
        grid=(m // tm, nk),
        in_specs=[
            pl.BlockSpec((tm, tk), lambda i, j: (i, j)),
            pl.BlockSpec((tk, n), lambda i, j: (j, 0)),
            pl.BlockSpec((tm, n), lambda i, j: (i, 0)),
            pl.BlockSpec((1, n), lambda i, j: (0, 0)),
            pl.BlockSpec((1, n), lambda i, j: (0, 0)),
        ],
        out_specs=[pl.BlockSpec((tm, n), lambda i, j: (i, 0)), pl.BlockSpec((tm, n), lambda i, j: (i, 0))],
        out_shape=[jax.ShapeDtypeStruct((m, n), F32), jax.ShapeDtypeStruct((m, n), BF16)],
        scratch_shapes=[pltpu.VMEM((tm, n), F32)],
        compiler_params=_params(("parallel", "arbitrary")),
        name="matmul_ln",
    )(a, w, res, ln_w.reshape(1, n), ln_b.reshape(1, n))


def _swiglu_up_body(a_ref, wg_ref, wu_ref, o_ref):
    a = a_ref[...]
    g = _dot(a, wg_ref[...])
    u = _dot(a, wu_ref[...])
    o_ref[...] = (g * jax.nn.sigmoid(g) * u).astype(o_ref.dtype)


def _swiglu_up(a, wg, wu, tm, tf):
    m, k = a.shape
    f = wg.shape[1]
    return pl.pallas_call(
        _swiglu_up_body,
        grid=(m // tm, f // tf),
        in_specs=[
            pl.BlockSpec((tm, k), lambda i, j: (i, 0)),
            pl.BlockSpec((k, tf), lambda i, j: (0, j)),
            pl.BlockSpec((k, tf), lambda i, j: (0, j)),
        ],
        out_specs=pl.BlockSpec((tm, tf), lambda i, j: (i, j)),
        out_shape=jax.ShapeDtypeStruct((m, f), BF16),
        compiler_params=_params(("parallel", "arbitrary")),
        name="swiglu_up",
    )(a, wg, wu)


def _ple_body(xb_ref, wg_ref, p_ref, wp_ref, x_ref, o_ref, ob_ref):
    gate = jax.nn.sigmoid(_dot(xb_ref[...], wg_ref[...]))
    proj = _dot(p_ref[...].astype(BF16), wp_ref[...])
    out = x_ref[...] + gate * proj
    o_ref[...] = out
    ob_ref[...] = out.astype(BF16)


def _ple(x, xb, p, wg, wp, tm, tn):
    m, d = x.shape
    pd = p.shape[1]
    return pl.pallas_call(
        _ple_body,
        grid=(m // tm, d // tn),
        in_specs=[
            pl.BlockSpec((tm, d), lambda i, j: (i, 0)),
            pl.BlockSpec((d, tn), lambda i, j: (0, j)),
            pl.BlockSpec((tm, pd), lambda i, j: (i, 0)),
            pl.BlockSpec((pd, tn), lambda i, j: (0, j)),
            pl.BlockSpec((tm, tn), lambda i, j: (i, j)),
        ],
        out_specs=[pl.BlockSpec((tm, tn), lambda i, j: (i, j)), pl.BlockSpec((tm, tn), lambda i, j: (i, j))],
        out_shape=[jax.ShapeDtypeStruct((m, d), F32), jax.ShapeDtypeStruct((m, d), BF16)],
        compiler_params=_params(("parallel", "arbitrary")),
        name="ple",
    )(xb, wg, p, wp, x)


def _mlstm_body(q_ref, k_ref, v_ref, og_ref, gc_ref, gr_ref, bc_ref, br_ref, nw_ref, y_ref, hs_ref, c_ref, *, seq):
    L = MLSTM_CHUNK
    nc = seq // L
    row = lax.broadcasted_iota(jnp.int32, (L, L), 0)
    col = lax.broadcasted_iota(jnp.int32, (L, L), 1)
    lower = row >= col
    upper = row <= col
    tril = lower.astype(F32)
    triu = upper.astype(F32)
    hs_ref[...] = jnp.zeros_like(hs_ref)
    c_ref[...] = jnp.zeros_like(c_ref)

    def cap(z):
        return GATE_CAP * jnp.tanh(z / GATE_CAP)

    def one_dir(cidx, d, n, m):
        rows = pl.ds(pl.multiple_of(cidx * L, L), L)
        gcol = cap(gc_ref[cidx] + bc_ref[...])
        grow = cap(gr_ref[cidx] + br_ref[...])
        lcol = _log_sigmoid(gcol)
        lrow = _log_sigmoid(grow)
        if d == 0:
            bcol = _dot(tril, lcol, precision=HIGHEST)
            brow = _dot(lrow, triu, precision=HIGHEST)
            mask = lower
        else:
            bcol = _dot(triu, lcol, precision=HIGHEST)
            brow = _dot(lrow, tril, precision=HIGHEST)
            mask = upper
        li_col = gcol[:, 2 * d:2 * d + 1]
        b_col = bcol[:, 2 * d + 1:2 * d + 2]
        li_row = grow[2 * d:2 * d + 1, :]
        b_row = brow[2 * d + 1:2 * d + 2, :]
        g = b_col[L - 1:L, :] if d == 0 else b_col[0:1, :]

        q = q_ref[rows, :] * (MLSTM_DK ** -0.5)
        k = k_ref[rows, :]
        qb = q.astype(BF16)
        kb = k.astype(BF16)
        vb = v_ref[rows, :].astype(BF16)

        logd = jnp.where(mask, b_col + (li_row - b_row), -jnp.inf)
        m_t = jnp.maximum(jnp.max(logd, axis=1, keepdims=True), b_col + m)
        s = _dot_nt(qb, kb) * jnp.exp(logd - m_t)
        inter_w = jnp.exp(b_col + m - m_t)
        c_old = c_ref[d]
        num = _dot(s.astype(BF16), vb) + inter_w * _dot(qb, c_old.astype(BF16))
        den = jnp.sum(s, axis=1, keepdims=True) + inter_w * jnp.sum(q * n, axis=1, keepdims=True)
        h = num / jnp.maximum(jnp.abs(den), jnp.exp(-m_t))
        hs_ref[rows, :] += h

        a_col = g - b_col + li_col
        m_loc = jnp.max(a_col, axis=0, keepdims=True)
        kw = k * jnp.exp(a_col - m_loc)
        c_loc = _dot_tn(kw.astype(BF16), vb)
        n_loc = jnp.sum(kw, axis=0, keepdims=True)
        m_new = jnp.maximum(g + m, m_loc)
        decay = jnp.exp(g + m - m_new)
        inj = jnp.exp(m_loc - m_new)
        c_ref[d] = decay * c_old + inj * c_loc
        return decay * n + inj * n_loc, m_new

    def body(i, carry):
        nf, mf, nb, mb = carry
        nf, mf = one_dir(i, 0, nf, mf)
        nb, mb = one_dir(nc - 1 - i, 1, nb, mb)
        return nf, mf, nb, mb

    zn = jnp.zeros((1, MLSTM_DK), F32)
    zm = jnp.zeros((1, 1), F32)
    lax.fori_loop(0, nc, body, (zn, zm, zn, zm))

    def fin(i, carry):
        rows = pl.ds(pl.multiple_of(i * L, L), L)
        hh = hs_ref[rows, :]
        r = lax.rsqrt(jnp.mean(hh * hh, axis=-1, keepdims=True) + RMS_EPS)
        y_ref[rows, :] = (hh * r * nw_ref[...] * jax.nn.sigmoid(og_ref[rows, :])).astype(y_ref.dtype)
        return carry

    lax.fori_loop(0, nc, fin, 0)


def _mlstm(u, gc, gr, bias_c, bias_r, norm_w):
    bsz, seq, _ = u.shape
    L = MLSTM_CHUNK
    nc = seq // L
    H, dk, dv = MLSTM_HEADS, MLSTM_DK, MLSTM_DV
    k_off = H * dk // dk
    v_off = 2 * H * dk // dv
    o_off = v_off + H
    return pl.pallas_call(
        functools.partial(_mlstm_body, seq=seq),
        grid=(bsz, H),
        in_specs=[
            pl.BlockSpec((None, seq, dk), lambda b, h: (b, 0, h)),
            pl.BlockSpec((None, seq, dk), lambda b, h: (b, 0, k_off + h)),
            pl.BlockSpec((None, seq, dv), lambda b, h: (b, 0, v_off + h)),
            pl.BlockSpec((None, seq, dv), lambda b, h: (b, 0, o_off + h)),
            pl.BlockSpec((None, None, nc, L, 4), lambda b, h: (b, h, 0, 0, 0)),
            pl.BlockSpec((None, None, nc, 4, L), lambda b, h: (b, h, 0, 0, 0)),
            pl.BlockSpec((None, 1, 4), lambda b, h: (h, 0, 0)),
            pl.BlockSpec((None, 4, 1), lambda b, h: (h, 0, 0)),
            pl.BlockSpec((1, dv), lambda b, h: (0, h)),
        ],
        out_specs=pl.BlockSpec((None, seq, dv), lambda b, h: (b, 0, h)),
        out_shape=jax.ShapeDtypeStruct((bsz, seq, H * dv), BF16),
        scratch_shapes=[pltpu.VMEM((seq, dv), F32), pltpu.VMEM((2, dk, dv), F32)],
        compiler_params=_params(("parallel", "arbitrary")),
        name="mlstm",
    )(u, u, u, u, gc, gr, bias_c, bias_r, norm_w.reshape(1, H * dv))


def _cumsum_rows(x, reverse):
    n = x.shape[0]
    ridx = lax.broadcasted_iota(jnp.int32, x.shape, 0)
    s = 1
    while s < n:
        if reverse:
            x = x + jnp.where(ridx < n - s, pltpu.roll(x, n - s, axis=0), 0.0)
        else:
            x = x + jnp.where(ridx >= s, pltpu.roll(x, s, axis=0), 0.0)
        s *= 2
    return x


def _hgrn_body(q_ref, ff_ref, fb_ref, i_ref, g_ref, lbl_ref, nw_ref, y_ref, os_ref, st_ref, *, seq, layer):
    L = HGRN_CHUNK
    SB = HGRN_SUB
    nc = seq // L
    os_ref[...] = jnp.zeros_like(os_ref)
    st_ref[...] = jnp.zeros_like(st_ref)
    rowi = lax.broadcasted_iota(jnp.int32, (L, HGRN_DK), 0)

    def lower_bound(d):
        lg = lbl_ref[d]
        e = jnp.exp(lg - jnp.max(lg, axis=0, keepdims=True))
        sm = e / jnp.sum(e, axis=0, keepdims=True)
        return jnp.sum(sm[:layer + 1, :], axis=0, keepdims=True)

    lbs = (lower_bound(0), lower_bound(1))

    def one_dir(cidx, d):
        rows = pl.ds(pl.multiple_of(cidx * L, L), L)
        qr = q_ref[rows, :]
        q = qr * jax.nn.sigmoid(qr)
        vb = i_ref[rows, :].astype(BF16)
        fr = (ff_ref if d == 0 else fb_ref)[rows, :]
        lb = lbs[d]
        f = lb + (1.0 - lb) * jax.nn.sigmoid(fr)
        k = 1.0 - f
        lf = jnp.log(f)
        b = _cumsum_rows(lf, reverse=(d == 1))
        g = b[L - 1:L, :] if d == 0 else b[0:1, :]
        qi = (q * jnp.exp(b)).astype(BF16)
        ke = (k * jnp.exp(g - b)).astype(BF16)
        parts = []
        for jb in range(L // SB):
            lo, hi = jb * SB, (jb + 1) * SB
            bm = b[lo + SB // 2:lo + SB // 2 + 1, :]
            qm = (q[lo:hi, :] * jnp.exp(b[lo:hi, :] - bm)).astype(BF16)
            reach = (rowi < hi) if d == 0 else (rowi >= lo)
            km = (k * jnp.exp(jnp.where(reach, bm - b, 0.0))).astype(BF16)
            tq = lo + lax.broadcasted_iota(jnp.int32, (SB, L), 0)
            ts = lax.broadcasted_iota(jnp.int32, (SB, L), 1)
            keep = (ts <= tq) if d == 0 else (ts >= tq)
            parts.append(jnp.where(keep, _dot_nt(qm, km), 0.0))
        a = jnp.concatenate(parts, axis=0)
        st = st_ref[d]
        o = _dot(a.astype(BF16), vb) + _dot_nt(qi, st.astype(BF16))
        st_ref[d] = st * jnp.exp(g) + _dot_tn(vb, ke)
        os_ref[rows, :] += o

    def body(i, carry):
        one_dir(i, 0)
        one_dir(nc - 1 - i, 1)
        return carry

    lax.fori_loop(0, nc, body, 0)

    FL = 256

    def fin(i, carry):
        rows = pl.ds(pl.multiple_of(i * FL, FL), FL)
        hh = os_ref[rows, :]
        r = lax.rsqrt(jnp.mean(hh * hh, axis=-1, keepdims=True) + RMS_EPS)
        gg = g_ref[rows, :]
        y_ref[rows, :] = (hh * r * nw_ref[...] * (gg * jax.nn.sigmoid(gg))).astype(y_ref.dtype)
        return carry

    lax.fori_loop(0, seq // FL, fin, 0)


def _hgrn(u, lb_logits, norm_w, layer):
    bsz, seq, _ = u.shape
    H, dk = HGRN_HEADS, HGRN_DK
    base = (2 * MLSTM_HEADS * MLSTM_DK + 2 * A_WIDTH) // dk
    slots = lb_logits.shape[1]
    return pl.pallas_call(
        functools.partial(_hgrn_body, seq=seq, layer=layer),
        grid=(bsz, H),
        in_specs=[
            pl.BlockSpec((None, seq, dk), lambda b, h: (b, 0, base + h)),
            pl.BlockSpec((None, seq, dk), lambda b, h: (b, 0, base + H + h)),
            pl.BlockSpec((None, seq, dk), lambda b, h: (b, 0, base + 2 * H + h)),
            pl.BlockSpec((None, seq, dk), lambda b, h: (b, 0, base + 3 * H + h)),
            pl.BlockSpec((None, seq, dk), lambda b, h: (b, 0, base + 4 * H + h)),
            pl.BlockSpec((2, slots, dk), lambda b, h: (0, 0, h)),
            pl.BlockSpec((1, dk), lambda b, h: (0, h)),
        ],
        out_specs=pl.BlockSpec((None, seq, dk), lambda b, h: (b, 0, h)),
        out_shape=jax.ShapeDtypeStruct((bsz, seq, H * dk), BF16),
        scratch_shapes=[pltpu.VMEM((seq, dk), F32), pltpu.VMEM((2, dk, dk), F32)],
        compiler_params=_params(("parallel", "arbitrary")),
        name="hgrn2",
    )(u, u, u, u, u, lb_logits, norm_w.reshape(1, H * dk))


def _na_body(q_ref, k_ref, v_ref, bias_ref, o_ref, *, rows):
    W = GRID_W
    G, U = NA_GROUP, NA_UNION
    ng = rows // G
    scale = NA_DH ** -0.5

    def body(gi, carry):
        r0 = gi * G
        us = jnp.clip(r0 - NA_KH // 2, 0, rows - U)
        cls = jnp.where(gi == 0, 0, jnp.where(gi == ng - 1, 2, 1))
        qrows = pl.ds(pl.multiple_of(r0 * W, G * W), G * W)
        kwin = pl.ds(pl.multiple_of(us * W, W), U * W)
        s = _dot_nt(q_ref[qrows, :], k_ref[kwin, :]) * scale + bias_ref[cls]
        m = jnp.max(s, axis=-1, keepdims=True)
        p = jnp.exp(s - m)
        den = jnp.sum(p, axis=-1, keepdims=True)
        o = _dot(p.astype(BF16), v_ref[kwin, :]) / den
        o_ref[qrows, :] = o.astype(o_ref.dtype)
        return carry

    lax.fori_loop(0, ng, body, 0)


def _na_group_classes(rows):
    G, U, kh = NA_GROUP, NA_UNION, NA_KH

    def info(r0):
        us = min(max(r0 - kh // 2, 0), rows - U)
        return us - r0, tuple(min(max(r0 + i - kh // 2, 0), rows - kh) - us for i in range(G))

    ng = rows // G
    infos = [info(G * g) for g in range(ng)]
    classes = [infos[0], infos[1], infos[-1]]
    assert rows % G == 0 and ng >= 3 and all(infos[g] == classes[1] for g in range(1, ng - 1))
    assert all(0 <= o and o + kh <= U for c in classes for o in c[1])
    return classes


def _na_bias_table(rpb, rows):
    W = GRID_W
    G, U, kh = NA_GROUP, NA_UNION, NA_KH
    colv = jnp.arange(W)
    col_start = jnp.clip(colv - NA_KW // 2, 0, W - NA_KW)
    col_mask = (colv[None, :] >= col_start[:, None]) & (colv[None, :] < col_start[:, None] + NA_KW)
    dc_idx = jnp.clip(colv[None, :] - colv[:, None] + NA_KW - 1, 0, 2 * NA_KW - 2)
    rpb_cols = jnp.where(col_mask[None, None], rpb.astype(F32)[:, :, dc_idx], NEG_BIG)
    dr = np.zeros((3, G, U), np.int32)
    valid = np.zeros((3, G, U), bool)
    for c, (delta, offs) in enumerate(_na_group_classes(rows)):
        for i in range(G):
            for ku in range(U):
                valid[c, i, ku] = offs[i] <= ku < offs[i] + kh
                dr[c, i, ku] = min(max(delta + ku - i + kh - 1, 0), 2 * kh - 2)
    tbl = jnp.where(jnp.asarray(valid)[None, :, :, :, None, None], rpb_cols[:, jnp.asarray(dr)], NEG_BIG)
    return tbl.transpose(0, 1, 2, 4, 3, 5).reshape(rpb.shape[0], 3, G * W, U * W)


def _na(qkv, bias_tbl):
    bsz, seq, _ = qkv.shape
    rows = seq // GRID_W
    H, dh = NA_HEADS, NA_DH
    return pl.pallas_call(
        functools.partial(_na_body, rows=rows),
        grid=(H, bsz),
        in_specs=[
            pl.BlockSpec((None, seq, dh), lambda h, b: (b, 0, h)),
            pl.BlockSpec((None, seq, dh), lambda h, b: (b, 0, H + h)),
            pl.BlockSpec((None, seq, dh), lambda h, b: (b, 0, 2 * H + h)),
            pl.BlockSpec((None, 3, NA_GROUP * GRID_W, NA_UNION * GRID_W), lambda h, b: (h, 0, 0, 0)),
        ],
        out_specs=pl.BlockSpec((None, seq, dh), lambda h, b: (b, 0, h)),
        out_shape=jax.ShapeDtypeStruct((bsz, seq, H * dh), BF16),
        compiler_params=_params(("parallel", "arbitrary")),
        name="natten",
    )(qkv, qkv, qkv, bias_tbl)


def _router_body(x_ref, w_ref, b_ref, e_ref, g_ref):
    logits = _dot(x_ref[...], w_ref[...], precision=HIGHEST) + b_ref[...]
    lane = lax.broadcasted_iota(jnp.int32, logits.shape, 1)
    nl = logits.shape[1]
    m1 = jnp.max(logits, axis=-1, keepdims=True)
    i1 = jnp.min(jnp.where(logits == m1, lane, nl), axis=-1, keepdims=True)
    rest = jnp.where(lane == i1, -jnp.inf, logits)
    m2 = jnp.max(rest, axis=-1, keepdims=True)
    i2 = jnp.min(jnp.where(rest == m2, lane, nl), axis=-1, keepdims=True)
    ex = jnp.exp(m2 - m1)
    g1 = 1.0 / (1.0 + ex)
    g2 = ex / (1.0 + ex)
    e_ref[...] = jnp.where(lane == 0, i1, jnp.where(lane == 1, i2, 0))
    g_ref[...] = jnp.where(lane == 0, g1, jnp.where(lane == 1, g2, 0.0))


def _router(x, w_router, b_router, tm):
    n, d = x.shape
    ne = w_router.shape[1]
    w = jnp.zeros((d, V7X_LANES), F32).at[:, :ne].set(w_router.astype(F32))
    b = jnp.full((1, V7X_LANES), NEG_BIG, F32).at[0, :ne].set(b_router.astype(F32))
    return pl.pallas_call(
        _router_body,
        grid=(n // tm,),
        in_specs=[
            pl.BlockSpec((tm, d), lambda i: (i, 0)),
            pl.BlockSpec((d, V7X_LANES), lambda i: (0, 0)),
            pl.BlockSpec((1, V7X_LANES), lambda i: (0, 0)),
        ],
        out_specs=[pl.BlockSpec((tm, V7X_LANES), lambda i: (i, 0)), pl.BlockSpec((tm, V7X_LANES), lambda i: (i, 0))],
        out_shape=[jax.ShapeDtypeStruct((n, V7X_LANES), jnp.int32), jax.ShapeDtypeStruct((n, V7X_LANES), F32)],
        compiler_params=_params(("parallel",)),
        name="router_top2",
    )(x, w, b)


def _moe_body(te_ref, nv_ref, x_ref, wg_ref, wu_ref, wd_ref, o_ref, wgb_ref, wub_ref, wdb_ref, *, sub):
    t = pl.program_id(0)
    f = pl.program_id(1)
    nv = nv_ref[t]
    tm = x_ref.shape[0]

    @pl.when(nv > 0)
    def _():
        wgb_ref[...] = wg_ref[...].astype(BF16)
        wub_ref[...] = wu_ref[...].astype(BF16)
        wdb_ref[...] = wd_ref[...].astype(BF16)

    for sb in range(tm // sub):
        rows = pl.ds(sb * sub, sub)

        @pl.when(sb * sub < nv)
        def _():
            x = x_ref[rows, :]
            g = _dot(x, wgb_ref[...])
            u = _dot(x, wub_ref[...])
            h = (g * jax.nn.sigmoid(g) * u).astype(BF16)
            part = _dot(h, wdb_ref[...])

            @pl.when(f == 0)
            def _():
                o_ref[rows, :] = part

            @pl.when(f > 0)
            def _():
                o_ref[rows, :] += part

        @pl.when(jnp.logical_and(sb * sub >= nv, f == 0))
        def _():
            o_ref[rows, :] = jnp.zeros((sub, o_ref.shape[1]), o_ref.dtype)


def _moe_experts(xs, tile_expert, tile_valid, wg, wu, wd, tm, tf, sub):
    rows, d = xs.shape
    n_tiles = rows // tm
    fdim = wg.shape[2]
    nf = fdim // tf

    def fidx(t, f, nv):
        return jnp.where(nv[t] > 0, f, nf - 1)

    return pl.pallas_call(
        functools.partial(_moe_body, sub=sub),
        grid_spec=pltpu.PrefetchScalarGridSpec(
            num_scalar_prefetch=2,
            grid=(n_tiles, nf),
            in_specs=[
                pl.BlockSpec((tm, d), lambda t, f, te, nv: (t, 0), pipeline_mode=pl.Buffered(1)),
                pl.BlockSpec((None, d, tf), lambda t, f, te, nv: (te[t], 0, fidx(t, f, nv))),
                pl.BlockSpec((None, d, tf), lambda t, f, te, nv: (te[t], 0, fidx(t, f, nv))),
                pl.BlockSpec((None, tf, d), lambda t, f, te, nv: (te[t], fidx(t, f, nv), 0)),
            ],
            out_specs=pl.BlockSpec((tm, d), lambda t, f, te, nv: (t, 0)),
            scratch_shapes=[pltpu.VMEM((d, tf), BF16), pltpu.VMEM((d, tf), BF16), pltpu.VMEM((tf, d), BF16)],
        ),
        out_shape=jax.ShapeDtypeStruct((rows, d), F32),
        compiler_params=pltpu.CompilerParams(dimension_semantics=("arbitrary", "arbitrary"), vmem_limit_bytes=MOE_VMEM_LIMIT),
        name="moe_experts",
    )(tile_expert, tile_valid, xs, wg, wu, wd)


def _combine_ln_body(x_ref, y_ref, g_ref, w_ref, b_ref, o_ref, ob_ref):
    d = x_ref.shape[1]
    g = g_ref[...]
    mix = y_ref[:, :d] * g[:, 0:1] + y_ref[:, d:] * g[:, 1:2]
    out = _layer_norm(ALPHA * x_ref[...] + mix, w_ref[...], b_ref[...])
    o_ref[...] = out
    ob_ref[...] = out.astype(BF16)


def _combine_ln(x, y2, gates, ln_w, ln_b, tm):
    n, d = x.shape
    return pl.pallas_call(
        _combine_ln_body,
        grid=(n // tm,),
        in_specs=[
            pl.BlockSpec((tm, d), lambda i: (i, 0)),
            pl.BlockSpec((tm, 2 * d), lambda i: (i, 0)),
            pl.BlockSpec((tm, V7X_LANES), lambda i: (i, 0)),
            pl.BlockSpec((1, d), lambda i: (0, 0)),
            pl.BlockSpec((1, d), lambda i: (0, 0)),
        ],
        out_specs=[pl.BlockSpec((tm, d), lambda i: (i, 0)), pl.BlockSpec((tm, d), lambda i: (i, 0))],
        out_shape=[jax.ShapeDtypeStruct((n, d), F32), jax.ShapeDtypeStruct((n, d), BF16)],
        compiler_params=_params(("parallel",)),
        name="combine_ln",
    )(x, y2, gates, ln_w.reshape(1, d), ln_b.reshape(1, d))


def _moe(x, xb, w_router, b_router, wg, wu, wd, ln_w, ln_b):
    n, d = x.shape
    tm = MOE_TILE
    nk = n * TOP_K
    e_out, g_out = _router(x, w_router, b_router, 512)
    e_flat = e_out[:, :TOP_K].reshape(-1)
    onehot = (e_flat[:, None] == jnp.arange(N_EXPERTS, dtype=jnp.int32)[None, :]).astype(jnp.int32)
    csum = jnp.cumsum(onehot, axis=0)
    counts = csum[-1]
    pos = jnp.take_along_axis(csum, e_flat[:, None], axis=1)[:, 0] - 1
    padded = (counts + tm - 1) // tm * tm
    pad_end = jnp.cumsum(padded)
    pad_start = pad_end - padded
    dest = pad_start[e_flat] + pos
    n_tiles = nk // tm + N_EXPERTS
    row_tok = jnp.zeros((n_tiles * tm,), jnp.int32).at[dest].set(jnp.arange(nk, dtype=jnp.int32) // TOP_K)
    n_active = (pad_end[-1] // tm).astype(jnp.int32)
    tile_all = jnp.arange(n_tiles, dtype=jnp.int32)
    tile_ids = jnp.minimum(tile_all, n_active - 1)
    tile_expert = jnp.minimum(jnp.searchsorted(pad_end, tile_ids * tm, side='right'), N_EXPERTS - 1).astype(jnp.int32)
    tile_valid = jnp.clip((pad_start + counts)[tile_expert] - tile_ids * tm, 0, tm)
    tile_valid = jnp.where(tile_all < n_active, tile_valid, 0).astype(jnp.int32)
    xs = xb[row_tok]
    ys = _moe_experts(xs, tile_expert, tile_valid, wg, wu, wd, tm, 512, MOE_SUB)
    y2 = ys[dest].reshape(n, TOP_K * d)
    return _combine_ln(x, y2, g_out, ln_w, ln_b, 256)


def kernel(x, p, ln_w, ln_b, rec_w_in, mlstm_gate_bias, mlstm_norm_w, hgrn_lb_logits, hgrn_norm_w, rec_w_out, ffn_w_gate, ffn_w_up, ffn_w_down, na_w_qkv, na_rpb, na_w_out, moe_w_router, moe_b_router, moe_w_gate, moe_w_up, moe_w_down, ple_w_gate, ple_w_proj):
    bsz, seq, d = x.shape
    n = bsz * seq
    depth = ln_w.shape[0]
    xf = x.reshape(n, d).astype(F32)
    xb = xf.astype(BF16)
    H = MLSTM_HEADS
    gate_lo = 2 * H * MLSTM_DK + 2 * A_WIDTH
    gate_hi = gate_lo + 4 * H
    for i in range(depth):
        j = i // 2
        if i % 2 == 0:
            w_in = rec_w_in[j]
            w_main = jnp.concatenate([w_in[:, :gate_lo], w_in[:, gate_hi:]], axis=1).astype(BF16)
            w_gate = jnp.zeros((d, V7X_LANES), BF16).at[:, :4 * H].set(w_in[:, gate_lo:gate_hi].astype(BF16))
            u = _matmul(xb, w_main, F32, 1024, 512).reshape(bsz, seq, -1)
            graw = _matmul(xb, w_gate, F32, 1024, V7X_LANES)[:, :4 * H].reshape(bsz, seq, 4, H)
            L = MLSTM_CHUNK
            gc = graw.transpose(0, 3, 1, 2).reshape(bsz, H, seq // L, L, 4)
            gr = gc.transpose(0, 1, 2, 4, 3)
            bias = mlstm_gate_bias[j].astype(F32).T
            y_a = _mlstm(u, gc, gr, bias.reshape(H, 1, 4), bias.reshape(H, 4, 1), mlstm_norm_w[j].astype(F32))
            y_b = _hgrn(u, hgrn_lb_logits.astype(F32), hgrn_norm_w[j].astype(F32), j)
            mix_in = jnp.concatenate([y_a, y_b], axis=-1).reshape(n, -1)
            xf, xb = _matmul_ln(mix_in, rec_w_out[j].astype(BF16), xf, ln_w[i, 0], ln_b[i, 0], 512, 2048)
            hid = _swiglu_up(xb, ffn_w_gate[j].astype(BF16), ffn_w_up[j].astype(BF16), 1024, 512)
            xf, xb = _matmul_ln(hid, ffn_w_down[j].astype(BF16), xf, ln_w[i, 1], ln_b[i, 1], 512, 512)
        else:
            qkv = _matmul(xb, na_w_qkv[j].astype(BF16), BF16, 1024, 512).reshape(bsz, seq, -1)
            att = _na(qkv, _na_bias_table(na_rpb[j], seq // GRID_W)).reshape(n, -1)
            xf, xb = _matmul_ln(att, na_w_out[j].astype(BF16), xf, ln_w[i, 0], ln_b[i, 0], 512, 2048)
            xf, xb = _moe(xf, xb, moe_w_router[j], moe_b_router[j], moe_w_gate[j], moe_w_up[j], moe_w_down[j], ln_w[i, 1], ln_b[i, 1])
        xf, xb = _ple(xf, xb, p[i].reshape(n, -1), ple_w_gate[i].astype(BF16), ple_w_proj[i].astype(BF16), 1024, 512)
    return xf.reshape(bsz, seq, d)
```

```python
import functools

import jax
import jax.numpy as jnp
import numpy as np
from jax import lax
from jax.experimental import pallas as pl
from jax.experimental.pallas import tpu as pltpu

F32 = jnp.float32
BF16 = jnp.bfloat16
HIGHEST = lax.Precision.HIGHEST

DEPTH = 2
ALPHA = (2 * DEPTH) ** 0.25
LN_EPS = 1e-5
RMS_EPS = 1e-6
GRID_W = 64

MLSTM_HEADS = 4
MLSTM_DK = 128
MLSTM_DV = 256
GATE_CAP = 15.0
HGRN_HEADS = 8
HGRN_DK = 128
A_WIDTH = MLSTM_HEADS * MLSTM_DV
B_WIDTH = HGRN_HEADS * HGRN_DK

NA_DH = 128
NA_HEADS = 16
NA_KH = 8
NA_KW = 16

N_EXPERTS = 8
TOP_K = 2

MLSTM_CHUNK = 128
HGRN_CHUNK = 128
HGRN_SUB = 32
MOE_TILE = 1024
MOE_SUB = 256
LN_ROW_GROUP = 128
NA_GROUP = 4
NA_UNION = NA_KH + NA_GROUP - 1

V7X_LANES = 128
MXU_COLS = 256
VMEM_LIMIT = 56 * 1024 * 1024
MOE_VMEM_LIMIT = 60 * 1024 * 1024
NEG_BIG = -1e30


def _params(sem):
    return pltpu.CompilerParams(dimension_semantics=sem, vmem_limit_bytes=VMEM_LIMIT)


def _dot(a, b, **kw):
    return jnp.dot(a, b, preferred_element_type=F32, **kw)


def _dot_nt(a, b):
    return lax.dot_general(a, b, (((1,), (1,)), ((), ())), preferred_element_type=F32)


def _dot_tn(a, b):
    return lax.dot_general(a, b, (((0,), (0,)), ((), ())), preferred_element_type=F32)


def _layer_norm(y, w, b):
    mu = jnp.mean(y, axis=-1, keepdims=True)
    yc = y - mu
    var = jnp.mean(yc * yc, axis=-1, keepdims=True)
    return yc * lax.rsqrt(var + LN_EPS) * w + b


def _log_sigmoid(z):
    return jnp.minimum(z, 0.0) - jnp.log(1.0 + jnp.exp(-jnp.abs(z)))


def _mm_body(a_ref, w_ref, o_ref):
    o_ref[...] = _dot(a_ref[...], w_ref[...]).astype(o_ref.dtype)


def _matmul(a, w, out_dtype, tm, tn):
    m, k = a.shape
    n = w.shape[1]
    return pl.pallas_call(
        _mm_body,
        grid=(m // tm, n // tn),
        in_specs=[pl.BlockSpec((tm, k), lambda i, j: (i, 0)), pl.BlockSpec((k, tn), lambda i, j: (0, j))],
        out_specs=pl.BlockSpec((tm, tn), lambda i, j: (i, j)),
        out_shape=jax.ShapeDtypeStruct((m, n), out_dtype),
        compiler_params=_params(("parallel", "arbitrary")),
        name="matmul",
    )(a, w)


def _mm_ln_body(a_ref, w_ref, res_ref, g_ref, b_ref, o_ref, ob_ref):
    part = LN_ROW_GROUP
    for r in range(a_ref.shape[0] // part):
        rows = pl.ds(r * part, part)
        y = ALPHA * res_ref[rows, :] + _dot(a_ref[rows, :], w_ref[...])
        out = _layer_norm(y, g_ref[...], b_ref[...])
        o_ref[rows, :] = out
        ob_ref[rows, :] = out.astype(BF16)


def _matmul_ln(a, w, res, ln_w, ln_b, tm):
    m, k = a.shape
    n = w.shape[1]
    row_spec = pl.BlockSpec((tm, n), lambda i: (i, 0))
    vec_spec = pl.BlockSpec((1, n), lambda i: (0, 0))
    return pl.pallas_call(
        _mm_ln_body,
        grid=(m // tm,),
        in_specs=[
            pl.BlockSpec((tm, k), lambda i: (i, 0)),
            pl.BlockSpec((k, n), lambda i: (0, 0), pipeline_mode=pl.Buffered(1)),
            row_spec,
            vec_spec,
            vec_spec,
        ],
        out_specs=[row_spec, row_spec],
        out_shape=[jax.ShapeDtypeStruct((m, n), F32), jax.ShapeDtypeStruct((m, n), BF16)],
        compiler_params=_params(("parallel",)),
        name="matmul_ln",
    )(a, w, res, ln_w.reshape(1, n), ln_b.reshape(1, n))


def _swiglu_up_body(a_ref, wg_ref, wu_ref, o_ref):
    for c in range(o_ref.shape[1] // MXU_COLS):
        cols = pl.ds(c * MXU_COLS, MXU_COLS)
        g = _dot(a_ref[...], wg_ref[:, cols])
        u = _dot(a_ref[...], wu_ref[:, cols])
        o_ref[:, cols] = (g * jax.nn.sigmoid(g) * u).astype(o_ref.dtype)


def _swiglu_up(a, wg, wu, tm, tf):
    m, k = a.shape
    f = wg.shape[1]
    return pl.pallas_call(
        _swiglu_up_body,
        grid=(m // tm, f // tf),
        in_specs=[
            pl.BlockSpec((tm, k), lambda i, j: (i, 0)),
            pl.BlockSpec((k, tf), lambda i, j: (0, j)),
            pl.BlockSpec((k, tf), lambda i, j: (0, j)),
        ],
        out_specs=pl.BlockSpec((tm, tf), lambda i, j: (i, j)),
        out_shape=jax.ShapeDtypeStruct((m, f), BF16),
        compiler_params=_params(("parallel", "arbitrary")),
        name="swiglu_up",
    )(a, wg, wu)


def _ple_body(xb_ref, wg_ref, p_ref, wp_ref, x_ref, o_ref, ob_ref):
    for c in range(o_ref.shape[1] // MXU_COLS):
        cols = pl.ds(c * MXU_COLS, MXU_COLS)
        gate = jax.nn.sigmoid(_dot(xb_ref[...], wg_ref[:, cols]))
        proj = _dot(p_ref[...].astype(BF16), wp_ref[:, cols])
        out = x_ref[:, cols] + gate * proj
        o_ref[:, cols] = out
        ob_ref[:, cols] = out.astype(BF16)


def _ple(x, xb, p, wg, wp, tm, tn):
    m, d = x.shape
    pd = p.shape[1]
    return pl.pallas_call(
        _ple_body,
        grid=(m // tm, d // tn),
        in_specs=[
            pl.BlockSpec((tm, d), lambda i, j: (i, 0)),
            pl.BlockSpec((d, tn), lambda i, j: (0, j)),
            pl.BlockSpec((tm, pd), lambda i, j: (i, 0)),
            pl.BlockSpec((pd, tn), lambda i, j: (0, j)),
            pl.BlockSpec((tm, tn), lambda i, j: (i, j)),
        ],
        out_specs=[pl.BlockSpec((tm, tn), lambda i, j: (i, j)), pl.BlockSpec((tm, tn), lambda i, j: (i, j))],
        out_shape=[jax.ShapeDtypeStruct((m, d), F32), jax.ShapeDtypeStruct((m, d), BF16)],
        compiler_params=_params(("parallel", "arbitrary")),
        name="ple",
    )(xb, wg, p, wp, x)


def _mlstm_body(q_ref, k_ref, v_ref, og_ref, gc_ref, gr_ref, bc_ref, br_ref, nw_ref, y_ref, hs_ref, c_ref, *, seq):
    L = MLSTM_CHUNK
    nc = seq // L
    row = lax.broadcasted_iota(jnp.int32, (L, L), 0)
    col = lax.broadcasted_iota(jnp.int32, (L, L), 1)
    lower = row >= col
    upper = row <= col
    tril = lower.astype(F32)
    triu = upper.astype(F32)
    hs_ref[...] = jnp.zeros_like(hs_ref)
    c_ref[...] = jnp.zeros_like(c_ref)

    def cap(z):
        return GATE_CAP * jnp.tanh(z / GATE_CAP)

    def one_dir(cidx, d, n, m):
        rows = pl.ds(pl.multiple_of(cidx * L, L), L)
        gcol = cap(gc_ref[cidx] + bc_ref[...])
        grow = cap(gr_ref[cidx] + br_ref[...])
        lcol = _log_sigmoid(gcol)
        lrow = _log_sigmoid(grow)
        if d == 0:
            bcol = _dot(tril, lcol, precision=HIGHEST)
            brow = _dot(lrow, triu, precision=HIGHEST)
            mask = lower
        else:
            bcol = _dot(triu, lcol, precision=HIGHEST)
            brow = _dot(lrow, tril, precision=HIGHEST)
            mask = upper
        li_col = gcol[:, 2 * d:2 * d + 1]
        b_col = bcol[:, 2 * d + 1:2 * d + 2]
        li_row = grow[2 * d:2 * d + 1, :]
        b_row = brow[2 * d + 1:2 * d + 2, :]
        g = b_col[L - 1:L, :] if d == 0 else b_col[0:1, :]

        q = q_ref[rows, :] * (MLSTM_DK ** -0.5)
        k = k_ref[rows, :]
        qb = q.astype(BF16)
        kb = k.astype(BF16)
        vb = v_ref[rows, :].astype(BF16)

        logd = jnp.where(mask, b_col + (li_row - b_row), -jnp.inf)
        m_t = jnp.maximum(jnp.max(logd, axis=1, keepdims=True), b_col + m)
        s = _dot_nt(qb, kb) * jnp.exp(logd - m_t)
        inter_w = jnp.exp(b_col + m - m_t)
        c_old = c_ref[d]
        num = _dot(s.astype(BF16), vb) + inter_w * _dot(qb, c_old.astype(BF16))
        den = jnp.sum(s, axis=1, keepdims=True) + inter_w * jnp.sum(q * n, axis=1, keepdims=True)
        h = num / jnp.maximum(jnp.abs(den), jnp.exp(-m_t))
        hs_ref[rows, :] += h

        a_col = g - b_col + li_col
        m_loc = jnp.max(a_col, axis=0, keepdims=True)
        kw = k * jnp.exp(a_col - m_loc)
        c_loc = _dot_tn(kw.astype(BF16), vb)
        n_loc = jnp.sum(kw, axis=0, keepdims=True)
        m_new = jnp.maximum(g + m, m_loc)
        decay = jnp.exp(g + m - m_new)
        inj = jnp.exp(m_loc - m_new)
        c_ref[d] = decay * c_old + inj * c_loc
        return decay * n + inj * n_loc, m_new

    def body(i, carry):
        nf, mf, nb, mb = carry
        nf, mf = one_dir(i, 0, nf, mf)
        nb, mb = one_dir(nc - 1 - i, 1, nb, mb)
        return nf, mf, nb, mb

    zn = jnp.zeros((1, MLSTM_DK), F32)
    zm = jnp.zeros((1, 1), F32)
    lax.fori_loop(0, nc, body, (zn, zm, zn, zm))

    def fin(i, carry):
        rows = pl.ds(pl.multiple_of(i * L, L), L)
        hh = hs_ref[rows, :]
        r = lax.rsqrt(jnp.mean(hh * hh, axis=-1, keepdims=True) + RMS_EPS)
        y_ref[rows, :] = (hh * r * nw_ref[...] * jax.nn.sigmoid(og_ref[rows, :])).astype(y_ref.dtype)
        return carry

    lax.fori_loop(0, nc, fin, 0)


def _mlstm(u, gc, gr, bias_c, bias_r, norm_w):
    bsz, seq, _ = u.shape
    L = MLSTM_CHUNK
    nc = seq // L
    H, dk, dv = MLSTM_HEADS, MLSTM_DK, MLSTM_DV
    k_off = H * dk // dk
    v_off = 2 * H * dk // dv
    o_off = v_off + H
    return pl.pallas_call(
        functools.partial(_mlstm_body, seq=seq),
        grid=(bsz, H),
        in_specs=[
            pl.BlockSpec((None, seq, dk), lambda b, h: (b, 0, h)),
            pl.BlockSpec((None, seq, dk), lambda b, h: (b, 0, k_off + h)),
            pl.BlockSpec((None, seq, dv), lambda b, h: (b, 0, v_off + h)),
            pl.BlockSpec((None, seq, dv), lambda b, h: (b, 0, o_off + h)),
            pl.BlockSpec((None, None, nc, L, 4), lambda b, h: (b, h, 0, 0, 0)),
            pl.BlockSpec((None, None, nc, 4, L), lambda b, h: (b, h, 0, 0, 0)),
            pl.BlockSpec((None, 1, 4), lambda b, h: (h, 0, 0)),
            pl.BlockSpec((None, 4, 1), lambda b, h: (h, 0, 0)),
            pl.BlockSpec((1, dv), lambda b, h: (0, h)),
        ],
        out_specs=pl.BlockSpec((None, seq, dv), lambda b, h: (b, 0, h)),
        out_shape=jax.ShapeDtypeStruct((bsz, seq, H * dv), BF16),
        scratch_shapes=[pltpu.VMEM((seq, dv), F32), pltpu.VMEM((2, dk, dv), F32)],
        compiler_params=_params(("parallel", "arbitrary")),
        name="mlstm",
    )(u, u, u, u, gc, gr, bias_c, bias_r, norm_w.reshape(1, H * dv))


def _cumsum_rows(x, reverse):
    n = x.shape[0]
    ridx = lax.broadcasted_iota(jnp.int32, x.shape, 0)
    s = 1
    while s < n:
        if reverse:
            x = x + jnp.where(ridx < n - s, pltpu.roll(x, n - s, axis=0), 0.0)
        else:
            x = x + jnp.where(ridx >= s, pltpu.roll(x, s, axis=0), 0.0)
        s *= 2
    return x


def _hgrn_body(q_ref, ff_ref, fb_ref, i_ref, g_ref, lbl_ref, nw_ref, y_ref, os_ref, st_ref, *, seq, layer):
    L = HGRN_CHUNK
    SB = HGRN_SUB
    nc = seq // L
    os_ref[...] = jnp.zeros_like(os_ref)
    st_ref[...] = jnp.zeros_like(st_ref)
    rowi = lax.broadcasted_iota(jnp.int32, (L, HGRN_DK), 0)

    def lower_bound(d):
        lg = lbl_ref[d]
        e = jnp.exp(lg - jnp.max(lg, axis=0, keepdims=True))
        sm = e / jnp.sum(e, axis=0, keepdims=True)
        return jnp.sum(sm[:layer + 1, :], axis=0, keepdims=True)

    lbs = (lower_bound(0), lower_bound(1))

    def one_dir(cidx, d):
        rows = pl.ds(pl.multiple_of(cidx * L, L), L)
        qr = q_ref[rows, :]
        q = qr * jax.nn.sigmoid(qr)
        vb = i_ref[rows, :].astype(BF16)
        fr = (ff_ref if d == 0 else fb_ref)[rows, :]
        lb = lbs[d]
        f = lb + (1.0 - lb) * jax.nn.sigmoid(fr)
        k = 1.0 - f
        lf = jnp.log(f)
        b = _cumsum_rows(lf, reverse=(d == 1))
        g = b[L - 1:L, :] if d == 0 else b[0:1, :]
        qi = (q * jnp.exp(b)).astype(BF16)
        ke = (k * jnp.exp(g - b)).astype(BF16)
        parts = []
        for jb in range(L // SB):
            lo, hi = jb * SB, (jb + 1) * SB
            bm = b[lo + SB // 2:lo + SB // 2 + 1, :]
            qm = (q[lo:hi, :] * jnp.exp(b[lo:hi, :] - bm)).astype(BF16)
            reach = (rowi < hi) if d == 0 else (rowi >= lo)
            km = (k * jnp.exp(jnp.where(reach, bm - b, 0.0))).astype(BF16)
            tq = lo + lax.broadcasted_iota(jnp.int32, (SB, L), 0)
            ts = lax.broadcasted_iota(jnp.int32, (SB, L), 1)
            keep = (ts <= tq) if d == 0 else (ts >= tq)
            parts.append(jnp.where(keep, _dot_nt(qm, km), 0.0))
        a = jnp.concatenate(parts, axis=0)
        st = st_ref[d]
        o = _dot(a.astype(BF16), vb) + _dot_nt(qi, st.astype(BF16))
        st_ref[d] = st * jnp.exp(g) + _dot_tn(vb, ke)
        os_ref[rows, :] += o

    def body(i, carry):
        one_dir(i, 0)
        one_dir(nc - 1 - i, 1)
        return carry

    lax.fori_loop(0, nc, body, 0)

    FL = 256

    def fin(i, carry):
        rows = pl.ds(pl.multiple_of(i * FL, FL), FL)
        hh = os_ref[rows, :]
        r = lax.rsqrt(jnp.mean(hh * hh, axis=-1, keepdims=True) + RMS_EPS)
        gg = g_ref[rows, :]
        y_ref[rows, :] = (hh * r * nw_ref[...] * (gg * jax.nn.sigmoid(gg))).astype(y_ref.dtype)
        return carry

    lax.fori_loop(0, seq // FL, fin, 0)


def _hgrn(u, lb_logits, norm_w, layer):
    bsz, seq, _ = u.shape
    H, dk = HGRN_HEADS, HGRN_DK
    base = (2 * MLSTM_HEADS * MLSTM_DK + 2 * A_WIDTH) // dk
    slots = lb_logits.shape[1]
    return pl.pallas_call(
        functools.partial(_hgrn_body, seq=seq, layer=layer),
        grid=(bsz, H),
        in_specs=[
            pl.BlockSpec((None, seq, dk), lambda b, h: (b, 0, base + h)),
            pl.BlockSpec((None, seq, dk), lambda b, h: (b, 0, base + H + h)),
            pl.BlockSpec((None, seq, dk), lambda b, h: (b, 0, base + 2 * H + h)),
            pl.BlockSpec((None, seq, dk), lambda b, h: (b, 0, base + 3 * H + h)),
            pl.BlockSpec((None, seq, dk), lambda b, h: (b, 0, base + 4 * H + h)),
            pl.BlockSpec((2, slots, dk), lambda b, h: (0, 0, h)),
            pl.BlockSpec((1, dk), lambda b, h: (0, h)),
        ],
        out_specs=pl.BlockSpec((None, seq, dk), lambda b, h: (b, 0, h)),
        out_shape=jax.ShapeDtypeStruct((bsz, seq, H * dk), BF16),
        scratch_shapes=[pltpu.VMEM((seq, dk), F32), pltpu.VMEM((2, dk, dk), F32)],
        compiler_params=_params(("parallel", "arbitrary")),
        name="hgrn2",
    )(u, u, u, u, u, lb_logits, norm_w.reshape(1, H * dk))


def _na_body(q_ref, k_ref, v_ref, bias_ref, o_ref, *, rows):
    W = GRID_W
    G, U = NA_GROUP, NA_UNION
    ng = rows // G
    scale = NA_DH ** -0.5

    def body(gi, carry):
        r0 = gi * G
        us = jnp.clip(r0 - NA_KH // 2, 0, rows - U)
        cls = jnp.where(gi == 0, 0, jnp.where(gi == ng - 1, 2, 1))
        qrows = pl.ds(pl.multiple_of(r0 * W, G * W), G * W)
        kwin = pl.ds(pl.multiple_of(us * W, W), U * W)
        s = _dot_nt(q_ref[qrows, :], k_ref[kwin, :]) * scale + bias_ref[cls]
        m = jnp.max(s, axis=-1, keepdims=True)
        p = jnp.exp(s - m)
        den = jnp.sum(p, axis=-1, keepdims=True)
        o = _dot(p.astype(BF16), v_ref[kwin, :]) / den
        o_ref[qrows, :] = o.astype(o_ref.dtype)
        return carry

    lax.fori_loop(0, ng, body, 0, unroll=2)


def _na_group_classes(rows):
    G, U, kh = NA_GROUP, NA_UNION, NA_KH

    def info(r0):
        us = min(max(r0 - kh // 2, 0), rows - U)
        return us - r0, tuple(min(max(r0 + i - kh // 2, 0), rows - kh) - us for i in range(G))

    ng = rows // G
    infos = [info(G * g) for g in range(ng)]
    classes = [infos[0], infos[1], infos[-1]]
    assert rows % G == 0 and ng >= 3 and all(infos[g] == classes[1] for g in range(1, ng - 1))
    assert all(0 <= o and o + kh <= U for c in classes for o in c[1])
    return classes


def _na_bias_table(rpb, rows):
    W = GRID_W
    G, U, kh = NA_GROUP, NA_UNION, NA_KH
    colv = jnp.arange(W)
    col_start = jnp.clip(colv - NA_KW // 2, 0, W - NA_KW)
    col_mask = (colv[None, :] >= col_start[:, None]) & (colv[None, :] < col_start[:, None] + NA_KW)
    dc_idx = jnp.clip(colv[None, :] - colv[:, None] + NA_KW - 1, 0, 2 * NA_KW - 2)
    rpb_cols = jnp.where(col_mask[None, None], rpb.astype(F32)[:, :, dc_idx], NEG_BIG)
    nh = rpb.shape[0]
    by_qc = rpb_cols.transpose(0, 2, 1, 3)
    blocks = []
    for delta, offs in _na_group_classes(rows):
        for i in range(G):
            dr0 = delta + offs[i] - i + kh - 1
            inside = by_qc[:, :, dr0:dr0 + kh, :].reshape(nh, W, kh * W)
            before = jnp.full((nh, W, offs[i] * W), NEG_BIG, F32)
            after = jnp.full((nh, W, (U - kh - offs[i]) * W), NEG_BIG, F32)
            blocks.append(jnp.concatenate([before, inside, after], axis=-1))
    return jnp.stack(blocks, axis=1).reshape(nh, 3, G * W, U * W)


def _na(qkv, bias_tbl):
    bsz, seq, _ = qkv.shape
    rows = seq // GRID_W
    H, dh = NA_HEADS, NA_DH
    return pl.pallas_call(
        functools.partial(_na_body, rows=rows),
        grid=(H, bsz),
        in_specs=[
            pl.BlockSpec((None, seq, dh), lambda h, b: (b, 0, h)),
            pl.BlockSpec((None, seq, dh), lambda h, b: (b, 0, H + h)),
            pl.BlockSpec((None, seq, dh), lambda h, b: (b, 0, 2 * H + h)),
            pl.BlockSpec((None, 3, NA_GROUP * GRID_W, NA_UNION * GRID_W), lambda h, b: (h, 0, 0, 0)),
        ],
        out_specs=pl.BlockSpec((None, seq, dh), lambda h, b: (b, 0, h)),
        out_shape=jax.ShapeDtypeStruct((bsz, seq, H * dh), BF16),
        compiler_params=_params(("parallel", "arbitrary")),
        name="natten",
    )(qkv, qkv, qkv, bias_tbl)


def _router_body(x_ref, w_ref, b_ref, e_ref, g_ref):
    logits = _dot(x_ref[...], w_ref[...], precision=HIGHEST) + b_ref[...]
    lane = lax.broadcasted_iota(jnp.int32, logits.shape, 1)
    nl = logits.shape[1]
    m1 = jnp.max(logits, axis=-1, keepdims=True)
    i1 = jnp.min(jnp.where(logits == m1, lane, nl), axis=-1, keepdims=True)
    rest = jnp.where(lane == i1, -jnp.inf, logits)
    m2 = jnp.max(rest, axis=-1, keepdims=True)
    i2 = jnp.min(jnp.where(rest == m2, lane, nl), axis=-1, keepdims=True)
    ex = jnp.exp(m2 - m1)
    g1 = 1.0 / (1.0 + ex)
    g2 = ex / (1.0 + ex)
    e_ref[...] = jnp.where(lane == 0, i1, jnp.where(lane == 1, i2, 0))
    g_ref[...] = jnp.where(lane == 0, g1, jnp.where(lane == 1, g2, 0.0))


def _router(x, w_router, b_router, tm):
    n, d = x.shape
    ne = w_router.shape[1]
    w = jnp.zeros((d, V7X_LANES), F32).at[:, :ne].set(w_router.astype(F32))
    b = jnp.full((1, V7X_LANES), NEG_BIG, F32).at[0, :ne].set(b_router.astype(F32))
    return pl.pallas_call(
        _router_body,
        grid=(n // tm,),
        in_specs=[
            pl.BlockSpec((tm, d), lambda i: (i, 0)),
            pl.BlockSpec((d, V7X_LANES), lambda i: (0, 0)),
            pl.BlockSpec((1, V7X_LANES), lambda i: (0, 0)),
        ],
        out_specs=[pl.BlockSpec((tm, V7X_LANES), lambda i: (i, 0)), pl.BlockSpec((tm, V7X_LANES), lambda i: (i, 0))],
        out_shape=[jax.ShapeDtypeStruct((n, V7X_LANES), jnp.int32), jax.ShapeDtypeStruct((n, V7X_LANES), F32)],
        compiler_params=_params(("parallel",)),
        name="router_top2",
    )(x, w, b)


def _moe_body(te_ref, nv_ref, x_ref, wg_ref, wu_ref, wd_ref, o_ref, wgb_ref, wub_ref, wdb_ref, *, sub):
    t = pl.program_id(0)
    f = pl.program_id(1)
    nv = nv_ref[t]
    tm = x_ref.shape[0]

    tf = wg_ref.shape[1]
    d = o_ref.shape[1]
    nc = MXU_COLS

    @pl.when(f == 0)
    def _():
        o_ref[...] = jnp.zeros_like(o_ref)

    for sb in range(tm // sub):
        rows = pl.ds(sb * sub, sub)

        @pl.when(sb * sub < nv)
        def _():
            hs = []
            for c in range(tf // nc):
                cols = pl.ds(c * nc, nc)
                if sb == 0:
                    wg = wg_ref[:, cols].astype(BF16)
                    wu = wu_ref[:, cols].astype(BF16)
                    wgb_ref[:, cols] = wg
                    wub_ref[:, cols] = wu
                else:
                    wg = wgb_ref[:, cols]
                    wu = wub_ref[:, cols]
                g = _dot(x_ref[rows, :], wg)
                u = _dot(x_ref[rows, :], wu)
                hs.append((g * jax.nn.sigmoid(g) * u).astype(BF16))
            h = jnp.concatenate(hs, axis=1)
            for c in range(d // nc):
                cols = pl.ds(c * nc, nc)
                if sb == 0:
                    wd = wd_ref[:, cols].astype(BF16)
                    wdb_ref[:, cols] = wd
                else:
                    wd = wdb_ref[:, cols]
                o_ref[rows, cols] += _dot(h, wd)


def _moe_experts(xs, tile_expert, tile_valid, wg, wu, wd, tm, tf, sub):
    rows, d = xs.shape
    n_tiles = rows // tm
    fdim = wg.shape[2]
    nf = fdim // tf

    def fidx(t, f, nv):
        return jnp.where(nv[t] > 0, f, nf - 1)

    return pl.pallas_call(
        functools.partial(_moe_body, sub=sub),
        grid_spec=pltpu.PrefetchScalarGridSpec(
            num_scalar_prefetch=2,
            grid=(n_tiles, nf),
            in_specs=[
                pl.BlockSpec((tm, d), lambda t, f, te, nv: (t, 0), pipeline_mode=pl.Buffered(1)),
                pl.BlockSpec((None, d, tf), lambda t, f, te, nv: (te[t], 0, fidx(t, f, nv))),
                pl.BlockSpec((None, d, tf), lambda t, f, te, nv: (te[t], 0, fidx(t, f, nv))),
                pl.BlockSpec((None, tf, d), lambda t, f, te, nv: (te[t], fidx(t, f, nv), 0)),
            ],
            out_specs=pl.BlockSpec((tm, d), lambda t, f, te, nv: (t, 0)),
            scratch_shapes=[pltpu.VMEM((d, tf), BF16), pltpu.VMEM((d, tf), BF16), pltpu.VMEM((tf, d), BF16)],
        ),
        out_shape=jax.ShapeDtypeStruct((rows, d), F32),
        compiler_params=pltpu.CompilerParams(dimension_semantics=("arbitrary", "arbitrary"), vmem_limit_bytes=MOE_VMEM_LIMIT),
        name="moe_experts",
    )(tile_expert, tile_valid, xs, wg, wu, wd)


def _combine_ln_body(x_ref, y0_ref, y1_ref, g_ref, w_ref, b_ref, o_ref, ob_ref):
    g = g_ref[...]
    mix = y0_ref[...] * g[:, 0:1] + y1_ref[...] * g[:, 1:2]
    out = _layer_norm(ALPHA * x_ref[...] + mix, w_ref[...], b_ref[...])
    o_ref[...] = out
    ob_ref[...] = out.astype(BF16)


def _combine_ln(x, y0, y1, gates, ln_w, ln_b, tm):
    n, d = x.shape
    row_spec = pl.BlockSpec((tm, d), lambda i: (i, 0))
    vec_spec = pl.BlockSpec((1, d), lambda i: (0, 0))
    return pl.pallas_call(
        _combine_ln_body,
        grid=(n // tm,),
        in_specs=[row_spec, row_spec, row_spec, pl.BlockSpec((tm, V7X_LANES), lambda i: (i, 0)), vec_spec, vec_spec],
        out_specs=[row_spec, row_spec],
        out_shape=[jax.ShapeDtypeStruct((n, d), F32), jax.ShapeDtypeStruct((n, d), BF16)],
        compiler_params=_params(("parallel",)),
        name="combine_ln",
    )(x, y0, y1, gates, ln_w.reshape(1, d), ln_b.reshape(1, d))


def _moe(x, xb, w_router, b_router, wg, wu, wd, ln_w, ln_b):
    n, d = x.shape
    tm = MOE_TILE
    nk = n * TOP_K
    e_out, g_out = _router(x, w_router, b_router, 512)
    e_flat = e_out[:, :TOP_K].reshape(-1)
    onehot = (e_flat[:, None] == jnp.arange(N_EXPERTS, dtype=jnp.int32)[None, :]).astype(jnp.int32)
    csum = jnp.cumsum(onehot, axis=0)
    counts = csum[-1]
    pos = jnp.take_along_axis(csum, e_flat[:, None], axis=1)[:, 0] - 1
    padded = (counts + tm - 1) // tm * tm
    pad_end = jnp.cumsum(padded)
    pad_start = pad_end - padded
    dest = pad_start[e_flat] + pos
    n_tiles = nk // tm + N_EXPERTS
    row_tok = jnp.zeros((n_tiles * tm,), jnp.int32).at[dest].set(jnp.arange(nk, dtype=jnp.int32) // TOP_K)
    n_active = (pad_end[-1] // tm).astype(jnp.int32)
    tile_all = jnp.arange(n_tiles, dtype=jnp.int32)
    tile_ids = jnp.minimum(tile_all, n_active - 1)
    tile_expert = jnp.minimum(jnp.searchsorted(pad_end, tile_ids * tm, side='right'), N_EXPERTS - 1).astype(jnp.int32)
    tile_valid = jnp.clip((pad_start + counts)[tile_expert] - tile_ids * tm, 0, tm)
    tile_valid = jnp.where(tile_all < n_active, tile_valid, 0).astype(jnp.int32)
    xs = xb[row_tok]
    ys = _moe_experts(xs, tile_expert, tile_valid, wg, wu, wd, tm, 512, MOE_SUB)
    dest2 = dest.reshape(n, TOP_K)
    return _combine_ln(x, ys[dest2[:, 0]], ys[dest2[:, 1]], g_out, ln_w, ln_b, 256)


def kernel(x, p, ln_w, ln_b, rec_w_in, mlstm_gate_bias, mlstm_norm_w, hgrn_lb_logits, hgrn_norm_w, rec_w_out, ffn_w_gate, ffn_w_up, ffn_w_down, na_w_qkv, na_rpb, na_w_out, moe_w_router, moe_b_router, moe_w_gate, moe_w_up, moe_w_down, ple_w_gate, ple_w_proj):
    bsz, seq, d = x.shape
    n = bsz * seq
    depth = ln_w.shape[0]
    xf = x.reshape(n, d).astype(F32)
    xb = xf.astype(BF16)
    H = MLSTM_HEADS
    gate_lo = 2 * H * MLSTM_DK + 2 * A_WIDTH
    gate_hi = gate_lo + 4 * H
    for i in range(depth):
        j = i // 2
        if i % 2 == 0:
            w_in = rec_w_in[j]
            w_main = jnp.concatenate([w_in[:, :gate_lo], w_in[:, gate_hi:]], axis=1).astype(BF16)
            w_gate = jnp.zeros((d, V7X_LANES), BF16).at[:, :4 * H].set(w_in[:, gate_lo:gate_hi].astype(BF16))
            u = _matmul(xb, w_main, F32, 1024, 1024).reshape(bsz, seq, -1)
            graw = _matmul(xb, w_gate, F32, 1024, V7X_LANES)[:, :4 * H].reshape(bsz, seq, 4, H)
            L = MLSTM_CHUNK
            gc = graw.transpose(0, 3, 1, 2).reshape(bsz, H, seq // L, L, 4)
            gr = gc.transpose(0, 1, 2, 4, 3)
            bias = mlstm_gate_bias[j].astype(F32).T
            y_a = _mlstm(u, gc, gr, bias.reshape(H, 1, 4), bias.reshape(H, 4, 1), mlstm_norm_w[j].astype(F32))
            y_b = _hgrn(u, hgrn_lb_logits.astype(F32), hgrn_norm_w[j].astype(F32), j)
            mix_in = jnp.concatenate([y_a, y_b], axis=-1).reshape(n, -1)
            xf, xb = _matmul_ln(mix_in, rec_w_out[j].astype(BF16), xf, ln_w[i, 0], ln_b[i, 0], 512)
            hid = _swiglu_up(xb, ffn_w_gate[j].astype(BF16), ffn_w_up[j].astype(BF16), 1024, 512)
            xf, xb = _matmul_ln(hid, ffn_w_down[j].astype(BF16), xf, ln_w[i, 1], ln_b[i, 1], 256)
        else:
            qkv = _matmul(xb, na_w_qkv[j].astype(BF16), BF16, 1024, 1024).reshape(bsz, seq, -1)
            att = _na(qkv, _na_bias_table(na_rpb[j], seq // GRID_W)).reshape(n, -1)
            xf, xb = _matmul_ln(att, na_w_out[j].astype(BF16), xf, ln_w[i, 0], ln_b[i, 0], 512)
            xf, xb = _moe(xf, xb, moe_w_router[j], moe_b_router[j], moe_w_gate[j], moe_w_up[j], moe_w_down[j], ln_w[i, 1], ln_b[i, 1])
        xf, xb = _ple(xf, xb, p[i].reshape(n, -1), ple_w_gate[i].astype(BF16), ple_w_proj[i].astype(BF16), 1024, 1024)
    return xf.reshape(bsz, seq, d)
```

```python
import functools

import jax
import jax.numpy as jnp
import numpy as np
from jax import lax
from jax.experimental import pallas as pl
from jax.experimental.pallas import tpu as pltpu

F32 = jnp.float32
BF16 = jnp.bfloat16
HIGHEST = lax.Precision.HIGHEST

DEPTH = 2
ALPHA = (2 * DEPTH) ** 0.25
LN_EPS = 1e-5
RMS_EPS = 1e-6
GRID_W = 64

MLSTM_HEADS = 4
MLSTM_DK = 128
MLSTM_DV = 256
GATE_CAP = 15.0
HGRN_HEADS = 8
HGRN_DK = 128
A_WIDTH = MLSTM_HEADS * MLSTM_DV
B_WIDTH = HGRN_HEADS * HGRN_DK

NA_DH = 128
NA_HEADS = 16
NA_KH = 8
NA_KW = 16

N_EXPERTS = 8
TOP_K = 2

MLSTM_CHUNK = 128
HGRN_CHUNK = 128
HGRN_SUB = 32
MOE_TILE = 1536
MOE_SUB = 256
LN_ROW_GROUP = 128
NA_GROUP = 4
NA_UNION = NA_KH + NA_GROUP - 1

V7X_LANES = 128
MXU_COLS = 256
VMEM_LIMIT = 56 * 1024 * 1024
MOE_VMEM_LIMIT = 60 * 1024 * 1024
NEG_BIG = -1e30


def _params(sem):
    return pltpu.CompilerParams(dimension_semantics=sem, vmem_limit_bytes=VMEM_LIMIT)


def _dot(a, b, **kw):
    return jnp.dot(a, b, preferred_element_type=F32, **kw)


def _dot_nt(a, b):
    return lax.dot_general(a, b, (((1,), (1,)), ((), ())), preferred_element_type=F32)


def _dot_tn(a, b):
    return lax.dot_general(a, b, (((0,), (0,)), ((), ())), preferred_element_type=F32)


def _layer_norm(y, w, b):
    mu = jnp.mean(y, axis=-1, keepdims=True)
    yc = y - mu
    var = jnp.mean(yc * yc, axis=-1, keepdims=True)
    return yc * lax.rsqrt(var + LN_EPS) * w + b


def _log_sigmoid(z):
    return jnp.minimum(z, 0.0) - jnp.log(1.0 + jnp.exp(-jnp.abs(z)))


def _mm_body(a_ref, w_ref, o_ref):
    o_ref[...] = _dot(a_ref[...], w_ref[...]).astype(o_ref.dtype)


def _matmul(a, w, out_dtype, tm, tn):
    m, k = a.shape
    n = w.shape[1]
    return pl.pallas_call(
        _mm_body,
        grid=(m // tm, n // tn),
        in_specs=[pl.BlockSpec((tm, k), lambda i, j: (i, 0)), pl.BlockSpec((k, tn), lambda i, j: (0, j))],
        out_specs=pl.BlockSpec((tm, tn), lambda i, j: (i, j)),
        out_shape=jax.ShapeDtypeStruct((m, n), out_dtype),
        compiler_params=_params(("parallel", "arbitrary")),
        name="matmul",
    )(a, w)


def _mm_ln_body(a_ref, w_ref, res_ref, g_ref, b_ref, o_ref, ob_ref):
    part = LN_ROW_GROUP
    for r in range(a_ref.shape[0] // part):
        rows = pl.ds(r * part, part)
        y = ALPHA * res_ref[rows, :] + _dot(a_ref[rows, :], w_ref[...])
        out = _layer_norm(y, g_ref[...], b_ref[...])
        o_ref[rows, :] = out
        ob_ref[rows, :] = out.astype(BF16)


def _matmul_ln(a, w, res, ln_w, ln_b, tm):
    m, k = a.shape
    n = w.shape[1]
    row_spec = pl.BlockSpec((tm, n), lambda i: (i, 0))
    vec_spec = pl.BlockSpec((1, n), lambda i: (0, 0))
    return pl.pallas_call(
        _mm_ln_body,
        grid=(m // tm,),
        in_specs=[
            pl.BlockSpec((tm, k), lambda i: (i, 0)),
            pl.BlockSpec((k, n), lambda i: (0, 0), pipeline_mode=pl.Buffered(1)),
            row_spec,
            vec_spec,
            vec_spec,
        ],
        out_specs=[row_spec, row_spec],
        out_shape=[jax.ShapeDtypeStruct((m, n), F32), jax.ShapeDtypeStruct((m, n), BF16)],
        compiler_params=_params(("parallel",)),
        name="matmul_ln",
    )(a, w, res, ln_w.reshape(1, n), ln_b.reshape(1, n))


def _swiglu_up_body(a_ref, wg_ref, wu_ref, o_ref):
    for c in range(o_ref.shape[1] // MXU_COLS):
        cols = pl.ds(c * MXU_COLS, MXU_COLS)
        g = _dot(a_ref[...], wg_ref[:, cols])
        u = _dot(a_ref[...], wu_ref[:, cols])
        o_ref[:, cols] = (g * jax.nn.sigmoid(g) * u).astype(o_ref.dtype)


def _swiglu_up(a, wg, wu, tm, tf):
    m, k = a.shape
    f = wg.shape[1]
    return pl.pallas_call(
        _swiglu_up_body,
        grid=(m // tm, f // tf),
        in_specs=[
            pl.BlockSpec((tm, k), lambda i, j: (i, 0)),
            pl.BlockSpec((k, tf), lambda i, j: (0, j)),
            pl.BlockSpec((k, tf), lambda i, j: (0, j)),
        ],
        out_specs=pl.BlockSpec((tm, tf), lambda i, j: (i, j)),
        out_shape=jax.ShapeDtypeStruct((m, f), BF16),
        compiler_params=_params(("parallel", "arbitrary")),
        name="swiglu_up",
    )(a, wg, wu)


def _ple_body(xb_ref, wg_ref, p_ref, wp_ref, x_ref, o_ref, ob_ref):
    for c in range(o_ref.shape[1] // MXU_COLS):
        cols = pl.ds(c * MXU_COLS, MXU_COLS)
        gate = jax.nn.sigmoid(_dot(xb_ref[...], wg_ref[:, cols]))
        proj = _dot(p_ref[...].astype(BF16), wp_ref[:, cols])
        out = x_ref[:, cols] + gate * proj
        o_ref[:, cols] = out
        ob_ref[:, cols] = out.astype(BF16)


def _ple(x, xb, p, wg, wp, tm, tn):
    m, d = x.shape
    pd = p.shape[1]
    return pl.pallas_call(
        _ple_body,
        grid=(m // tm, d // tn),
        in_specs=[
            pl.BlockSpec((tm, d), lambda i, j: (i, 0)),
            pl.BlockSpec((d, tn), lambda i, j: (0, j)),
            pl.BlockSpec((tm, pd), lambda i, j: (i, 0)),
            pl.BlockSpec((pd, tn), lambda i, j: (0, j)),
            pl.BlockSpec((tm, tn), lambda i, j: (i, j)),
        ],
        out_specs=[pl.BlockSpec((tm, tn), lambda i, j: (i, j)), pl.BlockSpec((tm, tn), lambda i, j: (i, j))],
        out_shape=[jax.ShapeDtypeStruct((m, d), F32), jax.ShapeDtypeStruct((m, d), BF16)],
        compiler_params=_params(("parallel", "arbitrary")),
        name="ple",
    )(xb, wg, p, wp, x)


def _mlstm_body(q_ref, k_ref, v_ref, og_ref, gc_ref, gr_ref, bc_ref, br_ref, nw_ref, y_ref, hs_ref, c_ref, *, seq):
    L = MLSTM_CHUNK
    nc = seq // L
    row = lax.broadcasted_iota(jnp.int32, (L, L), 0)
    col = lax.broadcasted_iota(jnp.int32, (L, L), 1)
    lower = row >= col
    upper = row <= col
    tril = lower.astype(F32)
    triu = upper.astype(F32)
    hs_ref[...] = jnp.zeros_like(hs_ref)
    c_ref[...] = jnp.zeros_like(c_ref)

    def cap(z):
        return GATE_CAP * jnp.tanh(z / GATE_CAP)

    def one_dir(cidx, d, n, m):
        rows = pl.ds(pl.multiple_of(cidx * L, L), L)
        gcol = cap(gc_ref[cidx] + bc_ref[...])
        grow = cap(gr_ref[cidx] + br_ref[...])
        lcol = _log_sigmoid(gcol)
        lrow = _log_sigmoid(grow)
        if d == 0:
            bcol = _dot(tril, lcol, precision=HIGHEST)
            brow = _dot(lrow, triu, precision=HIGHEST)
            mask = lower
        else:
            bcol = _dot(triu, lcol, precision=HIGHEST)
            brow = _dot(lrow, tril, precision=HIGHEST)
            mask = upper
        li_col = gcol[:, 2 * d:2 * d + 1]
        b_col = bcol[:, 2 * d + 1:2 * d + 2]
        li_row = grow[2 * d:2 * d + 1, :]
        b_row = brow[2 * d + 1:2 * d + 2, :]
        g = b_col[L - 1:L, :] if d == 0 else b_col[0:1, :]

        q = q_ref[rows, :] * (MLSTM_DK ** -0.5)
        k = k_ref[rows, :]
        qb = q.astype(BF16)
        kb = k.astype(BF16)
        vb = v_ref[rows, :].astype(BF16)

        logd = jnp.where(mask, b_col + (li_row - b_row), -jnp.inf)
        m_t = jnp.maximum(jnp.max(logd, axis=1, keepdims=True), b_col + m)
        s = _dot_nt(qb, kb) * jnp.exp(logd - m_t)
        inter_w = jnp.exp(b_col + m - m_t)
        c_old = c_ref[d]
        num = _dot(s.astype(BF16), vb) + inter_w * _dot(qb, c_old.astype(BF16))
        den = jnp.sum(s, axis=1, keepdims=True) + inter_w * jnp.sum(q * n, axis=1, keepdims=True)
        h = num / jnp.maximum(jnp.abs(den), jnp.exp(-m_t))
        hs_ref[rows, :] += h

        a_col = g - b_col + li_col
        m_loc = jnp.max(a_col, axis=0, keepdims=True)
        kw = k * jnp.exp(a_col - m_loc)
        c_loc = _dot_tn(kw.astype(BF16), vb)
        n_loc = jnp.sum(kw, axis=0, keepdims=True)
        m_new = jnp.maximum(g + m, m_loc)
        decay = jnp.exp(g + m - m_new)
        inj = jnp.exp(m_loc - m_new)
        c_ref[d] = decay * c_old + inj * c_loc
        return decay * n + inj * n_loc, m_new

    def body(i, carry):
        nf, mf, nb, mb = carry
        nf, mf = one_dir(i, 0, nf, mf)
        nb, mb = one_dir(nc - 1 - i, 1, nb, mb)
        return nf, mf, nb, mb

    zn = jnp.zeros((1, MLSTM_DK), F32)
    zm = jnp.zeros((1, 1), F32)
    lax.fori_loop(0, nc, body, (zn, zm, zn, zm), unroll=4)

    def fin(i, carry):
        rows = pl.ds(pl.multiple_of(i * L, L), L)
        hh = hs_ref[rows, :]
        r = lax.rsqrt(jnp.mean(hh * hh, axis=-1, keepdims=True) + RMS_EPS)
        y_ref[rows, :] = (hh * r * nw_ref[...] * jax.nn.sigmoid(og_ref[rows, :])).astype(y_ref.dtype)
        return carry

    lax.fori_loop(0, nc, fin, 0)


def _mlstm(u, gc, gr, bias_c, bias_r, norm_w):
    bsz, seq, _ = u.shape
    L = MLSTM_CHUNK
    nc = seq // L
    H, dk, dv = MLSTM_HEADS, MLSTM_DK, MLSTM_DV
    k_off = H * dk // dk
    v_off = 2 * H * dk // dv
    o_off = v_off + H
    return pl.pallas_call(
        functools.partial(_mlstm_body, seq=seq),
        grid=(bsz, H),
        in_specs=[
            pl.BlockSpec((None, seq, dk), lambda b, h: (b, 0, h)),
            pl.BlockSpec((None, seq, dk), lambda b, h: (b, 0, k_off + h)),
            pl.BlockSpec((None, seq, dv), lambda b, h: (b, 0, v_off + h)),
            pl.BlockSpec((None, seq, dv), lambda b, h: (b, 0, o_off + h)),
            pl.BlockSpec((None, None, nc, L, 4), lambda b, h: (b, h, 0, 0, 0)),
            pl.BlockSpec((None, None, nc, 4, L), lambda b, h: (b, h, 0, 0, 0)),
            pl.BlockSpec((None, 1, 4), lambda b, h: (h, 0, 0)),
            pl.BlockSpec((None, 4, 1), lambda b, h: (h, 0, 0)),
            pl.BlockSpec((1, dv), lambda b, h: (0, h)),
        ],
        out_specs=pl.BlockSpec((None, seq, dv), lambda b, h: (b, 0, h)),
        out_shape=jax.ShapeDtypeStruct((bsz, seq, H * dv), BF16),
        scratch_shapes=[pltpu.VMEM((seq, dv), F32), pltpu.VMEM((2, dk, dv), F32)],
        compiler_params=_params(("parallel", "arbitrary")),
        name="mlstm",
    )(u, u, u, u, gc, gr, bias_c, bias_r, norm_w.reshape(1, H * dv))


def _cumsum_rows(x, reverse):
    n = x.shape[0]
    ridx = lax.broadcasted_iota(jnp.int32, x.shape, 0)
    s = 1
    while s < n:
        if reverse:
            x = x + jnp.where(ridx < n - s, pltpu.roll(x, n - s, axis=0), 0.0)
        else:
            x = x + jnp.where(ridx >= s, pltpu.roll(x, s, axis=0), 0.0)
        s *= 2
    return x


def _hgrn_body(q_ref, ff_ref, fb_ref, i_ref, g_ref, lbl_ref, nw_ref, y_ref, os_ref, st_ref, *, seq, layer):
    L = HGRN_CHUNK
    SB = HGRN_SUB
    nc = seq // L
    os_ref[...] = jnp.zeros_like(os_ref)
    st_ref[...] = jnp.zeros_like(st_ref)
    rowi = lax.broadcasted_iota(jnp.int32, (L, HGRN_DK), 0)

    def lower_bound(d):
        lg = lbl_ref[d]
        e = jnp.exp(lg - jnp.max(lg, axis=0, keepdims=True))
        sm = e / jnp.sum(e, axis=0, keepdims=True)
        return jnp.sum(sm[:layer + 1, :], axis=0, keepdims=True)

    lbs = (lower_bound(0), lower_bound(1))

    def one_dir(cidx, d):
        rows = pl.ds(pl.multiple_of(cidx * L, L), L)
        qr = q_ref[rows, :]
        q = qr * jax.nn.sigmoid(qr)
        vb = i_ref[rows, :].astype(BF16)
        fr = (ff_ref if d == 0 else fb_ref)[rows, :]
        lb = lbs[d]
        f = lb + (1.0 - lb) * jax.nn.sigmoid(fr)
        k = 1.0 - f
        lf = jnp.log(f)
        b = _cumsum_rows(lf, reverse=(d == 1))
        g = b[L - 1:L, :] if d == 0 else b[0:1, :]
        qi = (q * jnp.exp(b)).astype(BF16)
        ke = (k * jnp.exp(g - b)).astype(BF16)
        parts = []
        for jb in range(L // SB):
            lo, hi = jb * SB, (jb + 1) * SB
            bm = b[lo + SB // 2:lo + SB // 2 + 1, :]
            qm = (q[lo:hi, :] * jnp.exp(b[lo:hi, :] - bm)).astype(BF16)
            reach = (rowi < hi) if d == 0 else (rowi >= lo)
            km = (k * jnp.exp(jnp.where(reach, bm - b, 0.0))).astype(BF16)
            tq = lo + lax.broadcasted_iota(jnp.int32, (SB, L), 0)
            ts = lax.broadcasted_iota(jnp.int32, (SB, L), 1)
            keep = (ts <= tq) if d == 0 else (ts >= tq)
            parts.append(jnp.where(keep, _dot_nt(qm, km), 0.0))
        a = jnp.concatenate(parts, axis=0)
        st = st_ref[d]
        o = _dot(a.astype(BF16), vb) + _dot_nt(qi, st.astype(BF16))
        st_ref[d] = st * jnp.exp(g) + _dot_tn(vb, ke)
        os_ref[rows, :] += o

    def body(i, carry):
        one_dir(i, 0)
        one_dir(nc - 1 - i, 1)
        return carry

    lax.fori_loop(0, nc, body, 0, unroll=4)

    FL = 256

    def fin(i, carry):
        rows = pl.ds(pl.multiple_of(i * FL, FL), FL)
        hh = os_ref[rows, :]
        r = lax.rsqrt(jnp.mean(hh * hh, axis=-1, keepdims=True) + RMS_EPS)
        gg = g_ref[rows, :]
        y_ref[rows, :] = (hh * r * nw_ref[...] * (gg * jax.nn.sigmoid(gg))).astype(y_ref.dtype)
        return carry

    lax.fori_loop(0, seq // FL, fin, 0)


def _hgrn(u, lb_logits, norm_w, layer):
    bsz, seq, _ = u.shape
    H, dk = HGRN_HEADS, HGRN_DK
    base = (2 * MLSTM_HEADS * MLSTM_DK + 2 * A_WIDTH) // dk
    slots = lb_logits.shape[1]
    return pl.pallas_call(
        functools.partial(_hgrn_body, seq=seq, layer=layer),
        grid=(bsz, H),
        in_specs=[
            pl.BlockSpec((None, seq, dk), lambda b, h: (b, 0, base + h)),
            pl.BlockSpec((None, seq, dk), lambda b, h: (b, 0, base + H + h)),
            pl.BlockSpec((None, seq, dk), lambda b, h: (b, 0, base + 2 * H + h)),
            pl.BlockSpec((None, seq, dk), lambda b, h: (b, 0, base + 3 * H + h)),
            pl.BlockSpec((None, seq, dk), lambda b, h: (b, 0, base + 4 * H + h)),
            pl.BlockSpec((2, slots, dk), lambda b, h: (0, 0, h)),
            pl.BlockSpec((1, dk), lambda b, h: (0, h)),
        ],
        out_specs=pl.BlockSpec((None, seq, dk), lambda b, h: (b, 0, h)),
        out_shape=jax.ShapeDtypeStruct((bsz, seq, H * dk), BF16),
        scratch_shapes=[pltpu.VMEM((seq, dk), F32), pltpu.VMEM((2, dk, dk), F32)],
        compiler_params=_params(("parallel", "arbitrary")),
        name="hgrn2",
    )(u, u, u, u, u, lb_logits, norm_w.reshape(1, H * dk))


def _na_body(q_ref, k_ref, v_ref, bias_ref, o_ref, *, rows):
    W = GRID_W
    G, U = NA_GROUP, NA_UNION
    ng = rows // G
    scale = NA_DH ** -0.5

    def body(gi, carry):
        r0 = gi * G
        us = jnp.clip(r0 - NA_KH // 2, 0, rows - U)
        cls = jnp.where(gi == 0, 0, jnp.where(gi == ng - 1, 2, 1))
        qrows = pl.ds(pl.multiple_of(r0 * W, G * W), G * W)
        kwin = pl.ds(pl.multiple_of(us * W, W), U * W)
        s = _dot_nt(q_ref[qrows, :], k_ref[kwin, :]) * scale + bias_ref[cls]
        m = jnp.max(s, axis=-1, keepdims=True)
        p = jnp.exp(s - m)
        den = jnp.sum(p, axis=-1, keepdims=True)
        o = _dot(p.astype(BF16), v_ref[kwin, :]) / den
        o_ref[qrows, :] = o.astype(o_ref.dtype)
        return carry

    lax.fori_loop(0, ng, body, 0, unroll=8)


def _na_group_classes(rows):
    G, U, kh = NA_GROUP, NA_UNION, NA_KH

    def info(r0):
        us = min(max(r0 - kh // 2, 0), rows - U)
        return us - r0, tuple(min(max(r0 + i - kh // 2, 0), rows - kh) - us for i in range(G))

    ng = rows // G
    infos = [info(G * g) for g in range(ng)]
    classes = [infos[0], infos[1], infos[-1]]
    assert rows % G == 0 and ng >= 3 and all(infos[g] == classes[1] for g in range(1, ng - 1))
    assert all(0 <= o and o + kh <= U for c in classes for o in c[1])
    return classes


def _na_bias_table(rpb, rows):
    W = GRID_W
    G, U, kh = NA_GROUP, NA_UNION, NA_KH
    colv = jnp.arange(W)
    col_start = jnp.clip(colv - NA_KW // 2, 0, W - NA_KW)
    col_mask = (colv[None, :] >= col_start[:, None]) & (colv[None, :] < col_start[:, None] + NA_KW)
    dc_idx = jnp.clip(colv[None, :] - colv[:, None] + NA_KW - 1, 0, 2 * NA_KW - 2)
    rpb_cols = jnp.where(col_mask[None, None], rpb.astype(F32)[:, :, dc_idx], NEG_BIG)
    nh = rpb.shape[0]
    by_qc = rpb_cols.transpose(0, 2, 1, 3)
    blocks = []
    for delta, offs in _na_group_classes(rows):
        for i in range(G):
            dr0 = delta + offs[i] - i + kh - 1
            inside = by_qc[:, :, dr0:dr0 + kh, :].reshape(nh, W, kh * W)
            before = jnp.full((nh, W, offs[i] * W), NEG_BIG, F32)
            after = jnp.full((nh, W, (U - kh - offs[i]) * W), NEG_BIG, F32)
            blocks.append(jnp.concatenate([before, inside, after], axis=-1))
    return jnp.stack(blocks, axis=1).reshape(nh, 3, G * W, U * W)


def _na(qkv, bias_tbl):
    bsz, seq, _ = qkv.shape
    rows = seq // GRID_W
    H, dh = NA_HEADS, NA_DH
    return pl.pallas_call(
        functools.partial(_na_body, rows=rows),
        grid=(H, bsz),
        in_specs=[
            pl.BlockSpec((None, seq, dh), lambda h, b: (b, 0, h)),
            pl.BlockSpec((None, seq, dh), lambda h, b: (b, 0, H + h)),
            pl.BlockSpec((None, seq, dh), lambda h, b: (b, 0, 2 * H + h)),
            pl.BlockSpec((None, 3, NA_GROUP * GRID_W, NA_UNION * GRID_W), lambda h, b: (h, 0, 0, 0)),
        ],
        out_specs=pl.BlockSpec((None, seq, dh), lambda h, b: (b, 0, h)),
        out_shape=jax.ShapeDtypeStruct((bsz, seq, H * dh), BF16),
        compiler_params=_params(("parallel", "arbitrary")),
        name="natten",
    )(qkv, qkv, qkv, bias_tbl)


def _router_body(x_ref, w_ref, b_ref, e_ref, g_ref):
    logits = _dot(x_ref[...], w_ref[...], precision=HIGHEST) + b_ref[...]
    lane = lax.broadcasted_iota(jnp.int32, logits.shape, 1)
    nl = logits.shape[1]
    m1 = jnp.max(logits, axis=-1, keepdims=True)
    i1 = jnp.min(jnp.where(logits == m1, lane, nl), axis=-1, keepdims=True)
    rest = jnp.where(lane == i1, -jnp.inf, logits)
    m2 = jnp.max(rest, axis=-1, keepdims=True)
    i2 = jnp.min(jnp.where(rest == m2, lane, nl), axis=-1, keepdims=True)
    ex = jnp.exp(m2 - m1)
    g1 = 1.0 / (1.0 + ex)
    g2 = ex / (1.0 + ex)
    e_ref[...] = jnp.where(lane == 0, i1, jnp.where(lane == 1, i2, 0))
    g_ref[...] = jnp.where(lane == 0, g1, jnp.where(lane == 1, g2, 0.0))


def _router(x, w_router, b_router, tm):
    n, d = x.shape
    ne = w_router.shape[1]
    w = jnp.zeros((d, V7X_LANES), F32).at[:, :ne].set(w_router.astype(F32))
    b = jnp.full((1, V7X_LANES), NEG_BIG, F32).at[0, :ne].set(b_router.astype(F32))
    return pl.pallas_call(
        _router_body,
        grid=(n // tm,),
        in_specs=[
            pl.BlockSpec((tm, d), lambda i: (i, 0)),
            pl.BlockSpec((d, V7X_LANES), lambda i: (0, 0)),
            pl.BlockSpec((1, V7X_LANES), lambda i: (0, 0)),
        ],
        out_specs=[pl.BlockSpec((tm, V7X_LANES), lambda i: (i, 0)), pl.BlockSpec((tm, V7X_LANES), lambda i: (i, 0))],
        out_shape=[jax.ShapeDtypeStruct((n, V7X_LANES), jnp.int32), jax.ShapeDtypeStruct((n, V7X_LANES), F32)],
        compiler_params=_params(("parallel",)),
        name="router_top2",
    )(x, w, b)


def _pack_bf16_pairs(xb):
    half = xb.shape[1] // 2
    lo = lax.bitcast_convert_type(xb[:, :half], jnp.uint16).astype(jnp.uint32)
    hi = lax.bitcast_convert_type(xb[:, half:], jnp.uint16).astype(jnp.uint32)
    return lo | (hi << 16)


def _moe_body(te_ref, nv_ref, xp_ref, wg_ref, wu_ref, wd_ref, o_ref, xb_ref, wgb_ref, wub_ref, wdb_ref, *, sub):
    t = pl.program_id(0)
    f = pl.program_id(1)
    nv = nv_ref[t]
    tm, half = xp_ref.shape
    tf = wg_ref.shape[1]
    d = o_ref.shape[1]
    nc = MXU_COLS

    @pl.when(f == 0)
    def _():
        o_ref[...] = jnp.zeros_like(o_ref)
        for sb in range(tm // sub):
            rows = pl.ds(sb * sub, sub)
            w = xp_ref[rows, :]
            xb_ref[rows, :half] = lax.bitcast_convert_type(w << 16, F32).astype(BF16)
            xb_ref[rows, half:] = lax.bitcast_convert_type(w & jnp.uint32(0xFFFF0000), F32).astype(BF16)

    for sb in range(tm // sub):
        rows = pl.ds(sb * sub, sub)

        @pl.when(sb * sub < nv)
        def _():
            hs = []
            for c in range(tf // nc):
                cols = pl.ds(c * nc, nc)
                if sb == 0:
                    wg = wg_ref[:, cols].astype(BF16)
                    wu = wu_ref[:, cols].astype(BF16)
                    wgb_ref[:, cols] = wg
                    wub_ref[:, cols] = wu
                else:
                    wg = wgb_ref[:, cols]
                    wu = wub_ref[:, cols]
                g = _dot(xb_ref[rows, :], wg)
                u = _dot(xb_ref[rows, :], wu)
                hs.append((g * jax.nn.sigmoid(g) * u).astype(BF16))
            h = jnp.concatenate(hs, axis=1)
            for c in range(d // nc):
                cols = pl.ds(c * nc, nc)
                if sb == 0:
                    wd = wd_ref[:, cols].astype(BF16)
                    wdb_ref[:, cols] = wd
                else:
                    wd = wdb_ref[:, cols]
                o_ref[rows, cols] += _dot(h, wd)


def _moe_experts(xp, tile_expert, tile_valid, wg, wu, wd, tm, tf, sub):
    rows, half = xp.shape
    d = 2 * half
    n_tiles = rows // tm
    fdim = wg.shape[2]
    nf = fdim // tf

    def fidx(t, f, nv):
        return jnp.where(nv[t] > 0, f, nf - 1)

    return pl.pallas_call(
        functools.partial(_moe_body, sub=sub),
        grid_spec=pltpu.PrefetchScalarGridSpec(
            num_scalar_prefetch=2,
            grid=(n_tiles, nf),
            in_specs=[
                pl.BlockSpec((tm, half), lambda t, f, te, nv: (t, 0), pipeline_mode=pl.Buffered(1)),
                pl.BlockSpec((None, d, tf), lambda t, f, te, nv: (te[t], 0, fidx(t, f, nv))),
                pl.BlockSpec((None, d, tf), lambda t, f, te, nv: (te[t], 0, fidx(t, f, nv))),
                pl.BlockSpec((None, tf, d), lambda t, f, te, nv: (te[t], fidx(t, f, nv), 0)),
            ],
            out_specs=pl.BlockSpec((tm, d), lambda t, f, te, nv: (t, 0), pipeline_mode=pl.Buffered(1)),
            scratch_shapes=[pltpu.VMEM((tm, d), BF16), pltpu.VMEM((d, tf), BF16), pltpu.VMEM((d, tf), BF16), pltpu.VMEM((tf, d), BF16)],
        ),
        out_shape=jax.ShapeDtypeStruct((rows, d), F32),
        compiler_params=pltpu.CompilerParams(dimension_semantics=("arbitrary", "arbitrary"), vmem_limit_bytes=MOE_VMEM_LIMIT),
        name="moe_experts",
    )(tile_expert, tile_valid, xp, wg, wu, wd)


def _combine_ln_body(x_ref, y0_ref, y1_ref, g_ref, w_ref, b_ref, o_ref, ob_ref):
    g = g_ref[...]
    mix = y0_ref[...] * g[:, 0:1] + y1_ref[...] * g[:, 1:2]
    out = _layer_norm(ALPHA * x_ref[...] + mix, w_ref[...], b_ref[...])
    o_ref[...] = out
    ob_ref[...] = out.astype(BF16)


def _combine_ln(x, y0, y1, gates, ln_w, ln_b, tm):
    n, d = x.shape
    row_spec = pl.BlockSpec((tm, d), lambda i: (i, 0))
    vec_spec = pl.BlockSpec((1, d), lambda i: (0, 0))
    return pl.pallas_call(
        _combine_ln_body,
        grid=(n // tm,),
        in_specs=[row_spec, row_spec, row_spec, pl.BlockSpec((tm, V7X_LANES), lambda i: (i, 0)), vec_spec, vec_spec],
        out_specs=[row_spec, row_spec],
        out_shape=[jax.ShapeDtypeStruct((n, d), F32), jax.ShapeDtypeStruct((n, d), BF16)],
        compiler_params=_params(("parallel",)),
        name="combine_ln",
    )(x, y0, y1, gates, ln_w.reshape(1, d), ln_b.reshape(1, d))


def _moe(x, xb, w_router, b_router, wg, wu, wd, ln_w, ln_b):
    n, d = x.shape
    tm = MOE_TILE
    nk = n * TOP_K
    e_out, g_out = _router(x, w_router, b_router, 512)
    e_flat = e_out[:, :TOP_K].reshape(-1)
    onehot = (e_flat[:, None] == jnp.arange(N_EXPERTS, dtype=jnp.int32)[None, :]).astype(jnp.int32)
    csum = jnp.cumsum(onehot, axis=0)
    counts = csum[-1]
    pos = jnp.take_along_axis(csum, e_flat[:, None], axis=1)[:, 0] - 1
    padded = (counts + tm - 1) // tm * tm
    pad_end = jnp.cumsum(padded)
    pad_start = pad_end - padded
    dest = pad_start[e_flat] + pos
    n_tiles = -(-nk // tm) + N_EXPERTS
    row_tok = jnp.zeros((n_tiles * tm,), jnp.int32).at[dest].set(jnp.arange(nk, dtype=jnp.int32) // TOP_K, unique_indices=True)
    n_active = (pad_end[-1] // tm).astype(jnp.int32)
    tile_all = jnp.arange(n_tiles, dtype=jnp.int32)
    tile_ids = jnp.minimum(tile_all, n_active - 1)
    tile_expert = jnp.sum((pad_end[None, :] <= (tile_ids * tm)[:, None]).astype(jnp.int32), axis=1)
    tile_expert = jnp.minimum(tile_expert, N_EXPERTS - 1)
    tile_valid = jnp.clip((pad_start + counts)[tile_expert] - tile_ids * tm, 0, tm)
    tile_valid = jnp.where(tile_all < n_active, tile_valid, 0).astype(jnp.int32)
    xs = _pack_bf16_pairs(xb)[row_tok]
    ys = _moe_experts(xs, tile_expert, tile_valid, wg, wu, wd, tm, 512, MOE_SUB)
    dest2 = dest.reshape(n, TOP_K)
    return _combine_ln(x, ys[dest2[:, 0]], ys[dest2[:, 1]], g_out, ln_w, ln_b, 256)


def kernel(x, p, ln_w, ln_b, rec_w_in, mlstm_gate_bias, mlstm_norm_w, hgrn_lb_logits, hgrn_norm_w, rec_w_out, ffn_w_gate, ffn_w_up, ffn_w_down, na_w_qkv, na_rpb, na_w_out, moe_w_router, moe_b_router, moe_w_gate, moe_w_up, moe_w_down, ple_w_gate, ple_w_proj):
    bsz, seq, d = x.shape
    n = bsz * seq
    depth = ln_w.shape[0]
    xf = x.reshape(n, d).astype(F32)
    xb = xf.astype(BF16)
    H = MLSTM_HEADS
    gate_lo = 2 * H * MLSTM_DK + 2 * A_WIDTH
    gate_hi = gate_lo + 4 * H
    for i in range(depth):
        j = i // 2
        if i % 2 == 0:
            w_in = rec_w_in[j]
            w_main = jnp.concatenate([w_in[:, :gate_lo], w_in[:, gate_hi:]], axis=1).astype(BF16)
            w_gate = jnp.zeros((d, V7X_LANES), BF16).at[:, :4 * H].set(w_in[:, gate_lo:gate_hi].astype(BF16))
            u = _matmul(xb, w_main, F32, 1024, 1024).reshape(bsz, seq, -1)
            graw = _matmul(xb, w_gate, F32, 1024, V7X_LANES)[:, :4 * H].reshape(bsz, seq, 4, H)
            L = MLSTM_CHUNK
            gc = graw.transpose(0, 3, 1, 2).reshape(bsz, H, seq // L, L, 4)
            gr = gc.transpose(0, 1, 2, 4, 3)
            bias = mlstm_gate_bias[j].astype(F32).T
            y_a = _mlstm(u, gc, gr, bias.reshape(H, 1, 4), bias.reshape(H, 4, 1), mlstm_norm_w[j].astype(F32))
            y_b = _hgrn(u, hgrn_lb_logits.astype(F32), hgrn_norm_w[j].astype(F32), j)
            mix_in = jnp.concatenate([y_a, y_b], axis=-1).reshape(n, -1)
            xf, xb = _matmul_ln(mix_in, rec_w_out[j].astype(BF16), xf, ln_w[i, 0], ln_b[i, 0], 512)
            hid = _swiglu_up(xb, ffn_w_gate[j].astype(BF16), ffn_w_up[j].astype(BF16), 1024, 512)
            xf, xb = _matmul_ln(hid, ffn_w_down[j].astype(BF16), xf, ln_w[i, 1], ln_b[i, 1], 256)
        else:
            qkv = _matmul(xb, na_w_qkv[j].astype(BF16), BF16, 1024, 1024).reshape(bsz, seq, -1)
            att = _na(qkv, _na_bias_table(na_rpb[j], seq // GRID_W)).reshape(n, -1)
            xf, xb = _matmul_ln(att, na_w_out[j].astype(BF16), xf, ln_w[i, 0], ln_b[i, 0], 512)
            xf, xb = _moe(xf, xb, moe_w_router[j], moe_b_router[j], moe_w_gate[j], moe_w_up[j], moe_w_down[j], ln_w[i, 1], ln_b[i, 1])
        xf, xb = _ple(xf, xb, p[i].reshape(n, -1), ple_w_gate[i].astype(BF16), ple_w_proj[i].astype(BF16), 1024, 1024)
    return xf.reshape(bsz, seq, d)
```

```python
import functools

import jax
import jax.numpy as jnp
import numpy as np
from jax import lax
from jax.experimental import pallas as pl
from jax.experimental.pallas import tpu as pltpu

F32 = jnp.float32
BF16 = jnp.bfloat16
HIGHEST = lax.Precision.HIGHEST

DEPTH = 2
ALPHA = (2 * DEPTH) ** 0.25
LN_EPS = 1e-5
RMS_EPS = 1e-6
GRID_W = 64

MLSTM_HEADS = 4
MLSTM_DK = 128
MLSTM_DV = 256
GATE_CAP = 15.0
HGRN_HEADS = 8
HGRN_DK = 128
A_WIDTH = MLSTM_HEADS * MLSTM_DV
B_WIDTH = HGRN_HEADS * HGRN_DK

NA_DH = 128
NA_HEADS = 16
NA_KH = 8
NA_KW = 16

N_EXPERTS = 8
TOP_K = 2

MLSTM_CHUNK = 128
HGRN_CHUNK = 128
HGRN_SUB = 32
MOE_TILE = 1536
MOE_SUB = 512
LN_ROW_GROUP = 128
NA_GROUP = 4
NA_UNION = NA_KH + NA_GROUP - 1

V7X_LANES = 128
MXU_COLS = 256
VMEM_LIMIT = 56 * 1024 * 1024
MOE_VMEM_LIMIT = 60 * 1024 * 1024
NEG_BIG = -1e30


def _params(sem):
    return pltpu.CompilerParams(dimension_semantics=sem, vmem_limit_bytes=VMEM_LIMIT)


def _dot(a, b, **kw):
    return jnp.dot(a, b, preferred_element_type=F32, **kw)


def _dot_nt(a, b):
    return lax.dot_general(a, b, (((1,), (1,)), ((), ())), preferred_element_type=F32)


def _dot_tn(a, b):
    return lax.dot_general(a, b, (((0,), (0,)), ((), ())), preferred_element_type=F32)


def _layer_norm(y, w, b):
    mu = jnp.mean(y, axis=-1, keepdims=True)
    yc = y - mu
    var = jnp.mean(yc * yc, axis=-1, keepdims=True)
    return yc * lax.rsqrt(var + LN_EPS) * w + b


def _log_sigmoid(z):
    return jnp.minimum(z, 0.0) - jnp.log(1.0 + jnp.exp(-jnp.abs(z)))


def _mm_body(a_ref, w_ref, o_ref):
    o_ref[...] = _dot(a_ref[...], w_ref[...]).astype(o_ref.dtype)


def _matmul(a, w, out_dtype, tm, tn):
    m, k = a.shape
    n = w.shape[1]
    return pl.pallas_call(
        _mm_body,
        grid=(m // tm, n // tn),
        in_specs=[pl.BlockSpec((tm, k), lambda i, j: (i, 0)), pl.BlockSpec((k, tn), lambda i, j: (0, j))],
        out_specs=pl.BlockSpec((tm, tn), lambda i, j: (i, j)),
        out_shape=jax.ShapeDtypeStruct((m, n), out_dtype),
        compiler_params=_params(("parallel", "arbitrary")),
        name="matmul",
    )(a, w)


def _mm_ln_body(a_ref, w_ref, res_ref, g_ref, b_ref, o_ref, ob_ref):
    part = LN_ROW_GROUP
    for r in range(a_ref.shape[0] // part):
        rows = pl.ds(r * part, part)
        y = ALPHA * res_ref[rows, :] + _dot(a_ref[rows, :], w_ref[...])
        out = _layer_norm(y, g_ref[...], b_ref[...])
        o_ref[rows, :] = out
        ob_ref[rows, :] = out.astype(BF16)


def _matmul_ln(a, w, res, ln_w, ln_b, tm):
    m, k = a.shape
    n = w.shape[1]
    row_spec = pl.BlockSpec((tm, n), lambda i: (i, 0))
    vec_spec = pl.BlockSpec((1, n), lambda i: (0, 0))
    return pl.pallas_call(
        _mm_ln_body,
        grid=(m // tm,),
        in_specs=[
            pl.BlockSpec((tm, k), lambda i: (i, 0)),
            pl.BlockSpec((k, n), lambda i: (0, 0), pipeline_mode=pl.Buffered(1)),
            row_spec,
            vec_spec,
            vec_spec,
        ],
        out_specs=[row_spec, row_spec],
        out_shape=[jax.ShapeDtypeStruct((m, n), F32), jax.ShapeDtypeStruct((m, n), BF16)],
        compiler_params=_params(("parallel",)),
        name="matmul_ln",
    )(a, w, res, ln_w.reshape(1, n), ln_b.reshape(1, n))


def _swiglu_up_body(a_ref, wg_ref, wu_ref, o_ref):
    for c in range(o_ref.shape[1] // MXU_COLS):
        cols = pl.ds(c * MXU_COLS, MXU_COLS)
        g = _dot(a_ref[...], wg_ref[:, cols])
        u = _dot(a_ref[...], wu_ref[:, cols])
        o_ref[:, cols] = (g * jax.nn.sigmoid(g) * u).astype(o_ref.dtype)


def _swiglu_up(a, wg, wu, tm, tf):
    m, k = a.shape
    f = wg.shape[1]
    return pl.pallas_call(
        _swiglu_up_body,
        grid=(m // tm, f // tf),
        in_specs=[
            pl.BlockSpec((tm, k), lambda i, j: (i, 0)),
            pl.BlockSpec((k, tf), lambda i, j: (0, j)),
            pl.BlockSpec((k, tf), lambda i, j: (0, j)),
        ],
        out_specs=pl.BlockSpec((tm, tf), lambda i, j: (i, j)),
        out_shape=jax.ShapeDtypeStruct((m, f), BF16),
        compiler_params=_params(("parallel", "arbitrary")),
        name="swiglu_up",
    )(a, wg, wu)


def _ple_body(xb_ref, wg_ref, p_ref, wp_ref, x_ref, o_ref, ob_ref):
    for c in range(o_ref.shape[1] // MXU_COLS):
        cols = pl.ds(c * MXU_COLS, MXU_COLS)
        gate = jax.nn.sigmoid(_dot(xb_ref[...], wg_ref[:, cols]))
        proj = _dot(p_ref[...].astype(BF16), wp_ref[:, cols])
        out = x_ref[:, cols] + gate * proj
        o_ref[:, cols] = out
        ob_ref[:, cols] = out.astype(BF16)


def _ple(x, xb, p, wg, wp, tm, tn):
    m, d = x.shape
    pd = p.shape[1]
    return pl.pallas_call(
        _ple_body,
        grid=(m // tm, d // tn),
        in_specs=[
            pl.BlockSpec((tm, d), lambda i, j: (i, 0)),
            pl.BlockSpec((d, tn), lambda i, j: (0, j)),
            pl.BlockSpec((tm, pd), lambda i, j: (i, 0)),
            pl.BlockSpec((pd, tn), lambda i, j: (0, j)),
            pl.BlockSpec((tm, tn), lambda i, j: (i, j)),
        ],
        out_specs=[pl.BlockSpec((tm, tn), lambda i, j: (i, j)), pl.BlockSpec((tm, tn), lambda i, j: (i, j))],
        out_shape=[jax.ShapeDtypeStruct((m, d), F32), jax.ShapeDtypeStruct((m, d), BF16)],
        compiler_params=_params(("parallel", "arbitrary")),
        name="ple",
    )(xb, wg, p, wp, x)


def _mlstm_body(q_ref, k_ref, v_ref, og_ref, gc_ref, gr_ref, bc_ref, br_ref, nw_ref, y_ref, hs_ref, c_ref, *, seq):
    L = MLSTM_CHUNK
    nc = seq // L
    row = lax.broadcasted_iota(jnp.int32, (L, L), 0)
    col = lax.broadcasted_iota(jnp.int32, (L, L), 1)
    lower = row >= col
    upper = row <= col
    tril = lower.astype(F32)
    triu = upper.astype(F32)
    hs_ref[...] = jnp.zeros_like(hs_ref)
    c_ref[...] = jnp.zeros_like(c_ref)

    def cap(z):
        return GATE_CAP * jnp.tanh(z / GATE_CAP)

    def one_dir(cidx, d, n, m):
        rows = pl.ds(pl.multiple_of(cidx * L, L), L)
        gcol = cap(gc_ref[cidx] + bc_ref[...])
        grow = cap(gr_ref[cidx] + br_ref[...])
        lcol = _log_sigmoid(gcol)
        lrow = _log_sigmoid(grow)
        if d == 0:
            bcol = _dot(tril, lcol, precision=HIGHEST)
            brow = _dot(lrow, triu, precision=HIGHEST)
            mask = lower
        else:
            bcol = _dot(triu, lcol, precision=HIGHEST)
            brow = _dot(lrow, tril, precision=HIGHEST)
            mask = upper
        li_col = gcol[:, 2 * d:2 * d + 1]
        b_col = bcol[:, 2 * d + 1:2 * d + 2]
        li_row = grow[2 * d:2 * d + 1, :]
        b_row = brow[2 * d + 1:2 * d + 2, :]
        g = b_col[L - 1:L, :] if d == 0 else b_col[0:1, :]

        q = q_ref[rows, :].astype(F32) * (MLSTM_DK ** -0.5)
        k = k_ref[rows, :].astype(F32)
        qb = q.astype(BF16)
        kb = k.astype(BF16)
        vb = v_ref[rows, :].astype(BF16)

        logd = jnp.where(mask, b_col + (li_row - b_row), -jnp.inf)
        m_t = jnp.maximum(jnp.max(logd, axis=1, keepdims=True), b_col + m)
        s = _dot_nt(qb, kb) * jnp.exp(logd - m_t)
        inter_w = jnp.exp(b_col + m - m_t)
        c_old = c_ref[d]
        num = _dot(s.astype(BF16), vb) + inter_w * _dot(qb, c_old.astype(BF16))
        den = jnp.sum(s, axis=1, keepdims=True) + inter_w * jnp.sum(q * n, axis=1, keepdims=True)
        h = num / jnp.maximum(jnp.abs(den), jnp.exp(-m_t))
        hs_ref[rows, :] += h

        a_col = g - b_col + li_col
        m_loc = jnp.max(a_col, axis=0, keepdims=True)
        kw = k * jnp.exp(a_col - m_loc)
        c_loc = _dot_tn(kw.astype(BF16), vb)
        n_loc = jnp.sum(kw, axis=0, keepdims=True)
        m_new = jnp.maximum(g + m, m_loc)
        decay = jnp.exp(g + m - m_new)
        inj = jnp.exp(m_loc - m_new)
        c_ref[d] = decay * c_old + inj * c_loc
        return decay * n + inj * n_loc, m_new

    def body(i, carry):
        nf, mf, nb, mb = carry
        nf, mf = one_dir(i, 0, nf, mf)
        nb, mb = one_dir(nc - 1 - i, 1, nb, mb)
        return nf, mf, nb, mb

    zn = jnp.zeros((1, MLSTM_DK), F32)
    zm = jnp.zeros((1, 1), F32)
    lax.fori_loop(0, nc, body, (zn, zm, zn, zm), unroll=4)

    def fin(i, carry):
        rows = pl.ds(pl.multiple_of(i * L, L), L)
        hh = hs_ref[rows, :]
        r = lax.rsqrt(jnp.mean(hh * hh, axis=-1, keepdims=True) + RMS_EPS)
        y_ref[rows, :] = (hh * r * nw_ref[...] * jax.nn.sigmoid(og_ref[rows, :].astype(F32))).astype(y_ref.dtype)
        return carry

    lax.fori_loop(0, nc, fin, 0)


def _mlstm(u, gc, gr, bias_c, bias_r, norm_w):
    bsz, seq, _ = u.shape
    L = MLSTM_CHUNK
    nc = seq // L
    H, dk, dv = MLSTM_HEADS, MLSTM_DK, MLSTM_DV
    k_off = H * dk // dk
    v_off = 2 * H * dk // dv
    o_off = v_off + H
    return pl.pallas_call(
        functools.partial(_mlstm_body, seq=seq),
        grid=(bsz, H),
        in_specs=[
            pl.BlockSpec((None, seq, dk), lambda b, h: (b, 0, h)),
            pl.BlockSpec((None, seq, dk), lambda b, h: (b, 0, k_off + h)),
            pl.BlockSpec((None, seq, dv), lambda b, h: (b, 0, v_off + h)),
            pl.BlockSpec((None, seq, dv), lambda b, h: (b, 0, o_off + h)),
            pl.BlockSpec((None, None, nc, L, 4), lambda b, h: (b, h, 0, 0, 0)),
            pl.BlockSpec((None, None, nc, 4, L), lambda b, h: (b, h, 0, 0, 0)),
            pl.BlockSpec((None, 1, 4), lambda b, h: (h, 0, 0)),
            pl.BlockSpec((None, 4, 1), lambda b, h: (h, 0, 0)),
            pl.BlockSpec((1, dv), lambda b, h: (0, h)),
        ],
        out_specs=pl.BlockSpec((None, seq, dv), lambda b, h: (b, 0, h)),
        out_shape=jax.ShapeDtypeStruct((bsz, seq, H * dv), BF16),
        scratch_shapes=[pltpu.VMEM((seq, dv), F32), pltpu.VMEM((2, dk, dv), F32)],
        compiler_params=_params(("parallel", "arbitrary")),
        name="mlstm",
    )(u, u, u, u, gc, gr, bias_c, bias_r, norm_w.reshape(1, H * dv))


def _cumsum_rows(x, reverse):
    n = x.shape[0]
    ridx = lax.broadcasted_iota(jnp.int32, x.shape, 0)
    s = 1
    while s < n:
        if reverse:
            x = x + jnp.where(ridx < n - s, pltpu.roll(x, n - s, axis=0), 0.0)
        else:
            x = x + jnp.where(ridx >= s, pltpu.roll(x, s, axis=0), 0.0)
        s *= 2
    return x


def _hgrn_body(q_ref, ff_ref, fb_ref, i_ref, g_ref, lbl_ref, nw_ref, y_ref, os_ref, st_ref, *, seq, layer):
    L = HGRN_CHUNK
    SB = HGRN_SUB
    nc = seq // L
    os_ref[...] = jnp.zeros_like(os_ref)
    st_ref[...] = jnp.zeros_like(st_ref)
    rowi = lax.broadcasted_iota(jnp.int32, (L, HGRN_DK), 0)

    def lower_bound(d):
        lg = lbl_ref[d]
        e = jnp.exp(lg - jnp.max(lg, axis=0, keepdims=True))
        sm = e / jnp.sum(e, axis=0, keepdims=True)
        return jnp.sum(sm[:layer + 1, :], axis=0, keepdims=True)

    lbs = (lower_bound(0), lower_bound(1))

    def one_dir(cidx, d):
        rows = pl.ds(pl.multiple_of(cidx * L, L), L)
        qr = q_ref[rows, :].astype(F32)
        q = qr * jax.nn.sigmoid(qr)
        vb = i_ref[rows, :].astype(BF16)
        fr = (ff_ref if d == 0 else fb_ref)[rows, :]
        lb = lbs[d]
        f = lb + (1.0 - lb) * jax.nn.sigmoid(fr)
        k = 1.0 - f
        lf = jnp.log(f)
        b = _cumsum_rows(lf, reverse=(d == 1))
        g = b[L - 1:L, :] if d == 0 else b[0:1, :]
        qi = (q * jnp.exp(b)).astype(BF16)
        ke = (k * jnp.exp(g - b)).astype(BF16)
        parts = []
        for jb in range(L // SB):
            lo, hi = jb * SB, (jb + 1) * SB
            bm = b[lo + SB // 2:lo + SB // 2 + 1, :]
            qm = (q[lo:hi, :] * jnp.exp(b[lo:hi, :] - bm)).astype(BF16)
            reach = (rowi < hi) if d == 0 else (rowi >= lo)
            km = (k * jnp.exp(jnp.where(reach, bm - b, 0.0))).astype(BF16)
            tq = lo + lax.broadcasted_iota(jnp.int32, (SB, L), 0)
            ts = lax.broadcasted_iota(jnp.int32, (SB, L), 1)
            keep = (ts <= tq) if d == 0 else (ts >= tq)
            parts.append(jnp.where(keep, _dot_nt(qm, km), 0.0))
        a = jnp.concatenate(parts, axis=0)
        st = st_ref[d]
        o = _dot(a.astype(BF16), vb) + _dot_nt(qi, st.astype(BF16))
        st_ref[d] = st * jnp.exp(g) + _dot_tn(vb, ke)
        os_ref[rows, :] += o

    def body(i, carry):
        one_dir(i, 0)
        one_dir(nc - 1 - i, 1)
        return carry

    lax.fori_loop(0, nc, body, 0, unroll=4)

    FL = 256

    def fin(i, carry):
        rows = pl.ds(pl.multiple_of(i * FL, FL), FL)
        hh = os_ref[rows, :]
        r = lax.rsqrt(jnp.mean(hh * hh, axis=-1, keepdims=True) + RMS_EPS)
        gg = g_ref[rows, :].astype(F32)
        y_ref[rows, :] = (hh * r * nw_ref[...] * (gg * jax.nn.sigmoid(gg))).astype(y_ref.dtype)
        return carry

    lax.fori_loop(0, seq // FL, fin, 0)


def _hgrn(u, uf, lb_logits, norm_w, layer):
    bsz, seq, _ = u.shape
    H, dk = HGRN_HEADS, HGRN_DK
    base = (2 * MLSTM_HEADS * MLSTM_DK + 2 * A_WIDTH) // dk
    slots = lb_logits.shape[1]
    return pl.pallas_call(
        functools.partial(_hgrn_body, seq=seq, layer=layer),
        grid=(bsz, H),
        in_specs=[
            pl.BlockSpec((None, seq, dk), lambda b, h: (b, 0, base + h)),
            pl.BlockSpec((None, seq, dk), lambda b, h: (b, 0, h)),
            pl.BlockSpec((None, seq, dk), lambda b, h: (b, 0, H + h)),
            pl.BlockSpec((None, seq, dk), lambda b, h: (b, 0, base + H + h)),
            pl.BlockSpec((None, seq, dk), lambda b, h: (b, 0, base + 2 * H + h)),
            pl.BlockSpec((2, slots, dk), lambda b, h: (0, 0, h)),
            pl.BlockSpec((1, dk), lambda b, h: (0, h)),
        ],
        out_specs=pl.BlockSpec((None, seq, dk), lambda b, h: (b, 0, h)),
        out_shape=jax.ShapeDtypeStruct((bsz, seq, H * dk), BF16),
        scratch_shapes=[pltpu.VMEM((seq, dk), F32), pltpu.VMEM((2, dk, dk), F32)],
        compiler_params=_params(("parallel", "arbitrary")),
        name="hgrn2",
    )(u, uf, uf, u, u, lb_logits, norm_w.reshape(1, H * dk))


def _na_body(q_ref, k_ref, v_ref, tb_ref, o_ref, bias_ref, *, rows):
    W = GRID_W
    G, U, kh = NA_GROUP, NA_UNION, NA_KH
    ng = rows // G
    scale = NA_DH ** -0.5

    @pl.when(pl.program_id(1) == 0)
    def _():
        for c, (delta, offs) in enumerate(_na_group_classes(rows)):
            for i in range(G):
                qr = pl.ds(i * W, W)
                for k0 in range(0, U, 2):
                    v0 = offs[i] <= k0 < offs[i] + kh
                    v1 = k0 + 1 < U and offs[i] <= k0 + 1 < offs[i] + kh
                    dr = delta + k0 - i + kh - 1
                    if k0 + 1 >= U:
                        blk = tb_ref[1, dr][:, :W] if v0 else jnp.full((W, W), NEG_BIG, F32)
                        bias_ref[c, qr, pl.ds(k0 * W, W)] = blk
                        continue
                    if v0 and v1:
                        blk = tb_ref[0, dr]
                    elif v0:
                        blk = tb_ref[1, dr]
                    elif v1:
                        blk = tb_ref[2, dr + 1]
                    else:
                        blk = jnp.full((W, 2 * W), NEG_BIG, F32)
                    bias_ref[c, qr, pl.ds(k0 * W, 2 * W)] = blk

    def body(gi, carry):
        r0 = gi * G
        us = jnp.clip(r0 - NA_KH // 2, 0, rows - U)
        cls = jnp.where(gi == 0, 0, jnp.where(gi == ng - 1, 2, 1))
        qrows = pl.ds(pl.multiple_of(r0 * W, G * W), G * W)
        kwin = pl.ds(pl.multiple_of(us * W, W), U * W)
        s = _dot_nt(q_ref[qrows, :], k_ref[kwin, :]) * scale + bias_ref[cls]
        m = jnp.max(s, axis=-1, keepdims=True)
        p = jnp.exp(s - m)
        den = jnp.sum(p, axis=-1, keepdims=True)
        o = _dot(p.astype(BF16), v_ref[kwin, :]) / den
        o_ref[qrows, :] = o.astype(o_ref.dtype)
        return carry

    lax.fori_loop(0, ng, body, 0, unroll=8)


def _na_group_classes(rows):
    G, U, kh = NA_GROUP, NA_UNION, NA_KH

    def info(r0):
        us = min(max(r0 - kh // 2, 0), rows - U)
        return us - r0, tuple(min(max(r0 + i - kh // 2, 0), rows - kh) - us for i in range(G))

    ng = rows // G
    infos = [info(G * g) for g in range(ng)]
    classes = [infos[0], infos[1], infos[-1]]
    assert rows % G == 0 and ng >= 3 and all(infos[g] == classes[1] for g in range(1, ng - 1))
    assert all(0 <= o and o + kh <= U for c in classes for o in c[1])
    return classes


def _na_bias_table(rpb):
    W = GRID_W
    colv = jnp.arange(W)
    col_start = jnp.clip(colv - NA_KW // 2, 0, W - NA_KW)
    col_mask = (colv[None, :] >= col_start[:, None]) & (colv[None, :] < col_start[:, None] + NA_KW)
    dc_idx = jnp.clip(colv[None, :] - colv[:, None] + NA_KW - 1, 0, 2 * NA_KW - 2)
    t = jnp.where(col_mask[None, None], rpb.astype(F32)[:, :, dc_idx], NEG_BIG)
    neg = jnp.full_like(t, NEG_BIG)
    t_next = jnp.concatenate([t[:, 1:], neg[:, :1]], axis=1)
    forms = [jnp.concatenate(pair, axis=-1) for pair in ((t, t_next), (t, neg), (neg, t))]
    return jnp.stack(forms, axis=1)


def _na(qkv, bias_tbl):
    bsz, seq, _ = qkv.shape
    rows = seq // GRID_W
    H, dh = NA_HEADS, NA_DH
    return pl.pallas_call(
        functools.partial(_na_body, rows=rows),
        grid=(H, bsz),
        in_specs=[
            pl.BlockSpec((None, seq, dh), lambda h, b: (b, 0, h)),
            pl.BlockSpec((None, seq, dh), lambda h, b: (b, 0, H + h)),
            pl.BlockSpec((None, seq, dh), lambda h, b: (b, 0, 2 * H + h)),
            pl.BlockSpec((None,) + bias_tbl.shape[1:], lambda h, b: (h, 0, 0, 0, 0)),
        ],
        out_specs=pl.BlockSpec((None, seq, dh), lambda h, b: (b, 0, h)),
        out_shape=jax.ShapeDtypeStruct((bsz, seq, H * dh), BF16),
        scratch_shapes=[pltpu.VMEM((3, NA_GROUP * GRID_W, NA_UNION * GRID_W), F32)],
        compiler_params=_params(("arbitrary", "arbitrary")),
        name="natten",
    )(qkv, qkv, qkv, bias_tbl)


def _router_body(x_ref, w_ref, b_ref, e_ref, g_ref):
    logits = _dot(x_ref[...], w_ref[...], precision=HIGHEST) + b_ref[...]
    lane = lax.broadcasted_iota(jnp.int32, logits.shape, 1)
    nl = logits.shape[1]
    m1 = jnp.max(logits, axis=-1, keepdims=True)
    i1 = jnp.min(jnp.where(logits == m1, lane, nl), axis=-1, keepdims=True)
    rest = jnp.where(lane == i1, -jnp.inf, logits)
    m2 = jnp.max(rest, axis=-1, keepdims=True)
    i2 = jnp.min(jnp.where(rest == m2, lane, nl), axis=-1, keepdims=True)
    ex = jnp.exp(m2 - m1)
    g1 = 1.0 / (1.0 + ex)
    g2 = ex / (1.0 + ex)
    e_ref[...] = jnp.where(lane == 0, i1, jnp.where(lane == 1, i2, 0))
    g_ref[...] = jnp.where(lane == 0, g1, jnp.where(lane == 1, g2, 0.0))


def _router(x, w_router, b_router, tm):
    n, d = x.shape
    ne = w_router.shape[1]
    w = jnp.zeros((d, V7X_LANES), F32).at[:, :ne].set(w_router.astype(F32))
    b = jnp.full((1, V7X_LANES), NEG_BIG, F32).at[0, :ne].set(b_router.astype(F32))
    return pl.pallas_call(
        _router_body,
        grid=(n // tm,),
        in_specs=[
            pl.BlockSpec((tm, d), lambda i: (i, 0)),
            pl.BlockSpec((d, V7X_LANES), lambda i: (0, 0)),
            pl.BlockSpec((1, V7X_LANES), lambda i: (0, 0)),
        ],
        out_specs=[pl.BlockSpec((tm, V7X_LANES), lambda i: (i, 0)), pl.BlockSpec((tm, V7X_LANES), lambda i: (i, 0))],
        out_shape=[jax.ShapeDtypeStruct((n, V7X_LANES), jnp.int32), jax.ShapeDtypeStruct((n, V7X_LANES), F32)],
        compiler_params=_params(("parallel",)),
        name="router_top2",
    )(x, w, b)


def _pack_bf16_pairs(xb):
    half = xb.shape[1] // 2
    lo = lax.bitcast_convert_type(xb[:, :half], jnp.uint16).astype(jnp.uint32)
    hi = lax.bitcast_convert_type(xb[:, half:], jnp.uint16).astype(jnp.uint32)
    return lo | (hi << 16)


def _moe_body(te_ref, nv_ref, xp_ref, wg_ref, wu_ref, wd_ref, o_ref, xb_ref, wgb_ref, wub_ref, wdb_ref, *, sub):
    t = pl.program_id(0)
    f = pl.program_id(1)
    nv = nv_ref[t]
    tm, half = xp_ref.shape
    tf = wg_ref.shape[1]
    d = o_ref.shape[1]
    nc = MXU_COLS

    @pl.when(f == 0)
    def _():
        o_ref[...] = jnp.zeros_like(o_ref)
        for sb in range(tm // sub):
            rows = pl.ds(sb * sub, sub)
            w = xp_ref[rows, :]
            xb_ref[rows, :half] = lax.bitcast_convert_type(w << 16, F32).astype(BF16)
            xb_ref[rows, half:] = lax.bitcast_convert_type(w & jnp.uint32(0xFFFF0000), F32).astype(BF16)

    for sb in range(tm // sub):
        rows = pl.ds(sb * sub, sub)

        @pl.when(sb * sub < nv)
        def _():
            hs = []
            for c in range(tf // nc):
                cols = pl.ds(c * nc, nc)
                if sb == 0:
                    wg = wg_ref[:, cols].astype(BF16)
                    wu = wu_ref[:, cols].astype(BF16)
                    wgb_ref[:, cols] = wg
                    wub_ref[:, cols] = wu
                else:
                    wg = wgb_ref[:, cols]
                    wu = wub_ref[:, cols]
                g = _dot(xb_ref[rows, :], wg)
                u = _dot(xb_ref[rows, :], wu)
                hs.append((g * jax.nn.sigmoid(g) * u).astype(BF16))
            h = jnp.concatenate(hs, axis=1)
            for c in range(d // nc):
                cols = pl.ds(c * nc, nc)
                if sb == 0:
                    wd = wd_ref[:, cols].astype(BF16)
                    wdb_ref[:, cols] = wd
                else:
                    wd = wdb_ref[:, cols]
                o_ref[rows, cols] += _dot(h, wd)


def _moe_experts(xp, tile_expert, tile_valid, wg, wu, wd, tm, tf, sub):
    rows, half = xp.shape
    d = 2 * half
    n_tiles = rows // tm
    fdim = wg.shape[2]
    nf = fdim // tf

    def fidx(t, f, nv):
        return jnp.where(nv[t] > 0, f, nf - 1)

    return pl.pallas_call(
        functools.partial(_moe_body, sub=sub),
        grid_spec=pltpu.PrefetchScalarGridSpec(
            num_scalar_prefetch=2,
            grid=(n_tiles, nf),
            in_specs=[
                pl.BlockSpec((tm, half), lambda t, f, te, nv: (t, 0), pipeline_mode=pl.Buffered(1)),
                pl.BlockSpec((None, d, tf), lambda t, f, te, nv: (te[t], 0, fidx(t, f, nv))),
                pl.BlockSpec((None, d, tf), lambda t, f, te, nv: (te[t], 0, fidx(t, f, nv))),
                pl.BlockSpec((None, tf, d), lambda t, f, te, nv: (te[t], fidx(t, f, nv), 0)),
            ],
            out_specs=pl.BlockSpec((tm, d), lambda t, f, te, nv: (t, 0), pipeline_mode=pl.Buffered(1)),
            scratch_shapes=[pltpu.VMEM((tm, d), BF16), pltpu.VMEM((d, tf), BF16), pltpu.VMEM((d, tf), BF16), pltpu.VMEM((tf, d), BF16)],
        ),
        out_shape=jax.ShapeDtypeStruct((rows, d), F32),
        compiler_params=pltpu.CompilerParams(dimension_semantics=("arbitrary", "arbitrary"), vmem_limit_bytes=MOE_VMEM_LIMIT),
        name="moe_experts",
    )(tile_expert, tile_valid, xp, wg, wu, wd)


def _combine_ln_body(x_ref, y0_ref, y1_ref, g_ref, w_ref, b_ref, o_ref, ob_ref):
    g = g_ref[...]
    mix = y0_ref[...] * g[:, 0:1] + y1_ref[...] * g[:, 1:2]
    out = _layer_norm(ALPHA * x_ref[...] + mix, w_ref[...], b_ref[...])
    o_ref[...] = out
    ob_ref[...] = out.astype(BF16)


def _combine_ln(x, y0, y1, gates, ln_w, ln_b, tm):
    n, d = x.shape
    row_spec = pl.BlockSpec((tm, d), lambda i: (i, 0))
    vec_spec = pl.BlockSpec((1, d), lambda i: (0, 0))
    return pl.pallas_call(
        _combine_ln_body,
        grid=(n // tm,),
        in_specs=[row_spec, row_spec, row_spec, pl.BlockSpec((tm, V7X_LANES), lambda i: (i, 0)), vec_spec, vec_spec],
        out_specs=[row_spec, row_spec],
        out_shape=[jax.ShapeDtypeStruct((n, d), F32), jax.ShapeDtypeStruct((n, d), BF16)],
        compiler_params=_params(("parallel",)),
        name="combine_ln",
    )(x, y0, y1, gates, ln_w.reshape(1, d), ln_b.reshape(1, d))


def _moe(x, xb, w_router, b_router, wg, wu, wd, ln_w, ln_b):
    n, d = x.shape
    tm = MOE_TILE
    nk = n * TOP_K
    e_out, g_out = _router(x, w_router, b_router, 512)
    e_flat = e_out[:, :TOP_K].reshape(-1)
    onehot = (e_flat[:, None] == jnp.arange(N_EXPERTS, dtype=jnp.int32)[None, :]).astype(jnp.int32)
    csum = jnp.cumsum(onehot, axis=0)
    counts = csum[-1]
    pos = jnp.take_along_axis(csum, e_flat[:, None], axis=1)[:, 0] - 1
    padded = (counts + tm - 1) // tm * tm
    pad_end = jnp.cumsum(padded)
    pad_start = pad_end - padded
    dest = pad_start[e_flat] + pos
    n_tiles = -(-nk // tm) + N_EXPERTS
    row_tok = (jnp.arange(n_tiles * tm, dtype=jnp.int32) % n).at[dest].set(jnp.arange(nk, dtype=jnp.int32) // TOP_K, unique_indices=True)
    n_active = (pad_end[-1] // tm).astype(jnp.int32)
    tile_all = jnp.arange(n_tiles, dtype=jnp.int32)
    tile_ids = jnp.minimum(tile_all, n_active - 1)
    tile_expert = jnp.sum((pad_end[None, :] <= (tile_ids * tm)[:, None]).astype(jnp.int32), axis=1)
    tile_expert = jnp.minimum(tile_expert, N_EXPERTS - 1)
    tile_valid = jnp.clip((pad_start + counts)[tile_expert] - tile_ids * tm, 0, tm)
    tile_valid = jnp.where(tile_all < n_active, tile_valid, 0).astype(jnp.int32)
    xs = _pack_bf16_pairs(xb)[row_tok]
    ys = _moe_experts(xs, tile_expert, tile_valid, wg, wu, wd, tm, 512, MOE_SUB)
    dest2 = dest.reshape(n, TOP_K)
    return _combine_ln(x, ys[dest2[:, 0]], ys[dest2[:, 1]], g_out, ln_w, ln_b, 256)


def kernel(x, p, ln_w, ln_b, rec_w_in, mlstm_gate_bias, mlstm_norm_w, hgrn_lb_logits, hgrn_norm_w, rec_w_out, ffn_w_gate, ffn_w_up, ffn_w_down, na_w_qkv, na_rpb, na_w_out, moe_w_router, moe_b_router, moe_w_gate, moe_w_up, moe_w_down, ple_w_gate, ple_w_proj):
    bsz, seq, d = x.shape
    n = bsz * seq
    depth = ln_w.shape[0]
    xf = x.reshape(n, d).astype(F32)
    xb = xf.astype(BF16)
    H = MLSTM_HEADS
    gate_lo = 2 * H * MLSTM_DK + 2 * A_WIDTH
    gate_hi = gate_lo + 4 * H
    for i in range(depth):
        j = i // 2
        if i % 2 == 0:
            w_in = rec_w_in[j]
            qb_lo, ff_lo, ib_lo = gate_hi, gate_hi + B_WIDTH, gate_hi + 3 * B_WIDTH
            w_main = jnp.concatenate([w_in[:, :gate_lo], w_in[:, qb_lo:ff_lo], w_in[:, ib_lo:]], axis=1).astype(BF16)
            w_forget = w_in[:, ff_lo:ib_lo].astype(BF16)
            w_gate = jnp.zeros((d, V7X_LANES), BF16).at[:, :4 * H].set(w_in[:, gate_lo:gate_hi].astype(BF16))
            u = _matmul(xb, w_main, BF16, 1024, 1024).reshape(bsz, seq, -1)
            uf = _matmul(xb, w_forget, F32, 1024, 1024).reshape(bsz, seq, -1)
            graw = _matmul(xb, w_gate, F32, 1024, V7X_LANES)[:, :4 * H].reshape(bsz, seq, 4, H)
            L = MLSTM_CHUNK
            gc = graw.transpose(0, 3, 1, 2).reshape(bsz, H, seq // L, L, 4)
            gr = gc.transpose(0, 1, 2, 4, 3)
            bias = mlstm_gate_bias[j].astype(F32).T
            y_a = _mlstm(u, gc, gr, bias.reshape(H, 1, 4), bias.reshape(H, 4, 1), mlstm_norm_w[j].astype(F32))
            y_b = _hgrn(u, uf, hgrn_lb_logits.astype(F32), hgrn_norm_w[j].astype(F32), j)
            mix_in = jnp.concatenate([y_a, y_b], axis=-1).reshape(n, -1)
            xf, xb = _matmul_ln(mix_in, rec_w_out[j].astype(BF16), xf, ln_w[i, 0], ln_b[i, 0], 512)
            hid = _swiglu_up(xb, ffn_w_gate[j].astype(BF16), ffn_w_up[j].astype(BF16), 1024, 512)
            xf, xb = _matmul_ln(hid, ffn_w_down[j].astype(BF16), xf, ln_w[i, 1], ln_b[i, 1], 256)
        else:
            qkv = _matmul(xb, na_w_qkv[j].astype(BF16), BF16, 1024, 1024).reshape(bsz, seq, -1)
            att = _na(qkv, _na_bias_table(na_rpb[j])).reshape(n, -1)
            xf, xb = _matmul_ln(att, na_w_out[j].astype(BF16), xf, ln_w[i, 0], ln_b[i, 0], 512)
            xf, xb = _moe(xf, xb, moe_w_router[j], moe_b_router[j], moe_w_gate[j], moe_w_up[j], moe_w_down[j], ln_w[i, 1], ln_b[i, 1])
        xf, xb = _ple(xf, xb, p[i].reshape(n, -1), ple_w_gate[i].astype(BF16), ple_w_proj[i].astype(BF16), 1024, 1024)
    return xf.reshape(bsz, seq, d)
```

```python
import functools

import jax
import jax.numpy as jnp
import numpy as np
from jax import lax
from jax.experimental import pallas as pl
from jax.experimental.pallas import tpu as pltpu

F32 = jnp.float32
BF16 = jnp.bfloat16
HIGHEST = lax.Precision.HIGHEST

DEPTH = 2
ALPHA = (2 * DEPTH) ** 0.25
LN_EPS = 1e-5
RMS_EPS = 1e-6
GRID_W = 64

MLSTM_HEADS = 4
MLSTM_DK = 128
MLSTM_DV = 256
GATE_CAP = 15.0
HGRN_HEADS = 8
HGRN_DK = 128
A_WIDTH = MLSTM_HEADS * MLSTM_DV
B_WIDTH = HGRN_HEADS * HGRN_DK

NA_DH = 128
NA_HEADS = 16
NA_KH = 8
NA_KW = 16

N_EXPERTS = 8
TOP_K = 2

MLSTM_CHUNK = 256
HGRN_CHUNK = 128
HGRN_SUB = 32
MOE_TILE = 1536
MOE_SUB = 512
LN_ROW_GROUP = 128
NA_GROUP = 4
NA_UNION = NA_KH + NA_GROUP - 1

V7X_LANES = 128
MXU_COLS = 256
VMEM_LIMIT = 56 * 1024 * 1024
MOE_VMEM_LIMIT = 60 * 1024 * 1024
NEG_BIG = -1e30


def _params(sem):
    return pltpu.CompilerParams(dimension_semantics=sem, vmem_limit_bytes=VMEM_LIMIT)


def _dot(a, b, **kw):
    return jnp.dot(a, b, preferred_element_type=F32, **kw)


def _dot_nt(a, b):
    return lax.dot_general(a, b, (((1,), (1,)), ((), ())), preferred_element_type=F32)


def _dot_tn(a, b):
    return lax.dot_general(a, b, (((0,), (0,)), ((), ())), preferred_element_type=F32)


def _layer_norm(y, w, b):
    mu = jnp.mean(y, axis=-1, keepdims=True)
    yc = y - mu
    var = jnp.mean(yc * yc, axis=-1, keepdims=True)
    return yc * lax.rsqrt(var + LN_EPS) * w + b


def _log_sigmoid(z):
    return jnp.minimum(z, 0.0) - jnp.log(1.0 + jnp.exp(-jnp.abs(z)))


def _mm_body(a_ref, w_ref, o_ref, wb_ref):
    @pl.when(pl.program_id(1) == 0)
    def _():
        wb_ref[...] = w_ref[...].astype(BF16)

    o_ref[...] = _dot(a_ref[...], wb_ref[...]).astype(o_ref.dtype)


def _matmul(a, w, out_dtype, tm, tn):
    m, k = a.shape
    n = w.shape[1]
    return pl.pallas_call(
        _mm_body,
        grid=(n // tn, m // tm),
        in_specs=[pl.BlockSpec((tm, k), lambda j, i: (i, 0)), pl.BlockSpec((k, tn), lambda j, i: (0, j))],
        out_specs=pl.BlockSpec((tm, tn), lambda j, i: (i, j)),
        out_shape=jax.ShapeDtypeStruct((m, n), out_dtype),
        scratch_shapes=[pltpu.VMEM((k, tn), BF16)],
        compiler_params=_params(("arbitrary", "arbitrary")),
        name="matmul",
    )(a, w)


def _mm_ln_body(*refs, na):
    a_refs = refs[:na]
    w_ref, res_ref, g_ref, b_ref, o_ref, ob_ref = refs[na:]
    part = LN_ROW_GROUP
    for r in range(res_ref.shape[0] // part):
        rows = pl.ds(r * part, part)
        y = ALPHA * res_ref[rows, :]
        k0 = 0
        for a_ref in a_refs:
            ka = a_ref.shape[1]
            y = y + _dot(a_ref[rows, :], w_ref[k0:k0 + ka, :])
            k0 += ka
        out = _layer_norm(y, g_ref[...], b_ref[...])
        o_ref[rows, :] = out
        ob_ref[rows, :] = out.astype(BF16)


def _matmul_ln(a_parts, w, res, ln_w, ln_b, tm):
    m = res.shape[0]
    k, n = w.shape
    assert sum(a.shape[1] for a in a_parts) == k
    row_spec = pl.BlockSpec((tm, n), lambda i: (i, 0))
    vec_spec = pl.BlockSpec((1, n), lambda i: (0, 0))
    return pl.pallas_call(
        functools.partial(_mm_ln_body, na=len(a_parts)),
        grid=(m // tm,),
        in_specs=[pl.BlockSpec((tm, a.shape[1]), lambda i: (i, 0)) for a in a_parts] + [
            pl.BlockSpec((k, n), lambda i: (0, 0), pipeline_mode=pl.Buffered(1)),
            row_spec,
            vec_spec,
            vec_spec,
        ],
        out_specs=[row_spec, row_spec],
        out_shape=[jax.ShapeDtypeStruct((m, n), F32), jax.ShapeDtypeStruct((m, n), BF16)],
        compiler_params=_params(("parallel",)),
        name="matmul_ln",
    )(*a_parts, w, res, ln_w.reshape(1, n), ln_b.reshape(1, n))


def _swiglu_up_body(a_ref, wg_ref, wu_ref, o_ref, wgb_ref, wub_ref):
    @pl.when(pl.program_id(1) == 0)
    def _():
        wgb_ref[...] = wg_ref[...].astype(BF16)
        wub_ref[...] = wu_ref[...].astype(BF16)

    for c in range(o_ref.shape[1] // MXU_COLS):
        cols = pl.ds(c * MXU_COLS, MXU_COLS)
        g = _dot(a_ref[...], wgb_ref[:, cols])
        u = _dot(a_ref[...], wub_ref[:, cols])
        o_ref[:, cols] = (g * jax.nn.sigmoid(g) * u).astype(o_ref.dtype)


def _swiglu_up(a, wg, wu, tm, tf):
    m, k = a.shape
    f = wg.shape[1]
    return pl.pallas_call(
        _swiglu_up_body,
        grid=(f // tf, m // tm),
        in_specs=[
            pl.BlockSpec((tm, k), lambda j, i: (i, 0)),
            pl.BlockSpec((k, tf), lambda j, i: (0, j)),
            pl.BlockSpec((k, tf), lambda j, i: (0, j)),
        ],
        out_specs=pl.BlockSpec((tm, tf), lambda j, i: (i, j)),
        out_shape=jax.ShapeDtypeStruct((m, f), BF16),
        scratch_shapes=[pltpu.VMEM((k, tf), BF16), pltpu.VMEM((k, tf), BF16)],
        compiler_params=_params(("arbitrary", "arbitrary")),
        name="swiglu_up",
    )(a, wg, wu)


def _ple_body(xb_ref, wg_ref, p_ref, wp_ref, x_ref, o_ref, ob_ref):
    for c in range(o_ref.shape[1] // MXU_COLS):
        cols = pl.ds(c * MXU_COLS, MXU_COLS)
        gate = jax.nn.sigmoid(_dot(xb_ref[...], wg_ref[:, cols]))
        proj = _dot(p_ref[...].astype(BF16), wp_ref[:, cols])
        out = x_ref[:, cols] + gate * proj
        o_ref[:, cols] = out
        ob_ref[:, cols] = out.astype(BF16)


def _ple(x, xb, p, wg, wp, tm, tn):
    m, d = x.shape
    pd = p.shape[1]
    return pl.pallas_call(
        _ple_body,
        grid=(m // tm, d // tn),
        in_specs=[
            pl.BlockSpec((tm, d), lambda i, j: (i, 0)),
            pl.BlockSpec((d, tn), lambda i, j: (0, j)),
            pl.BlockSpec((tm, pd), lambda i, j: (i, 0)),
            pl.BlockSpec((pd, tn), lambda i, j: (0, j)),
            pl.BlockSpec((tm, tn), lambda i, j: (i, j)),
        ],
        out_specs=[pl.BlockSpec((tm, tn), lambda i, j: (i, j)), pl.BlockSpec((tm, tn), lambda i, j: (i, j))],
        out_shape=[jax.ShapeDtypeStruct((m, d), F32), jax.ShapeDtypeStruct((m, d), BF16)],
        compiler_params=_params(("parallel", "arbitrary")),
        name="ple",
    )(xb, wg, p, wp, x)


def _mlstm_body(q_ref, k_ref, v_ref, og_ref, gc_ref, gr_ref, bc_ref, br_ref, nw_ref, y_ref, hs_ref, c_ref, *, seq):
    L = MLSTM_CHUNK
    nc = seq // L
    row = lax.broadcasted_iota(jnp.int32, (L, L), 0)
    col = lax.broadcasted_iota(jnp.int32, (L, L), 1)
    lower = row >= col
    upper = row <= col
    tril = lower.astype(F32)
    triu = upper.astype(F32)
    hs_ref[...] = jnp.zeros_like(hs_ref)
    c_ref[...] = jnp.zeros_like(c_ref)

    def cap(z):
        return GATE_CAP * jnp.tanh(z / GATE_CAP)

    def one_dir(cidx, d, n, m):
        rows = pl.ds(pl.multiple_of(cidx * L, L), L)
        gcol = cap(gc_ref[cidx] + bc_ref[...])
        grow = cap(gr_ref[cidx] + br_ref[...])
        lcol = _log_sigmoid(gcol)
        lrow = _log_sigmoid(grow)
        if d == 0:
            bcol = _dot(tril, lcol, precision=HIGHEST)
            brow = _dot(lrow, triu, precision=HIGHEST)
            mask = lower
        else:
            bcol = _dot(triu, lcol, precision=HIGHEST)
            brow = _dot(lrow, tril, precision=HIGHEST)
            mask = upper
        li_col = gcol[:, 2 * d:2 * d + 1]
        b_col = bcol[:, 2 * d + 1:2 * d + 2]
        li_row = grow[2 * d:2 * d + 1, :]
        b_row = brow[2 * d + 1:2 * d + 2, :]
        g = b_col[L - 1:L, :] if d == 0 else b_col[0:1, :]

        q = q_ref[rows, :].astype(F32) * (MLSTM_DK ** -0.5)
        k = k_ref[rows, :].astype(F32)
        qb = q.astype(BF16)
        kb = k.astype(BF16)
        vb = v_ref[rows, :].astype(BF16)

        logd = jnp.where(mask, b_col + (li_row - b_row), -jnp.inf)
        m_t = jnp.maximum(jnp.max(logd, axis=1, keepdims=True), b_col + m)
        s = _dot_nt(qb, kb) * jnp.exp(logd - m_t)
        inter_w = jnp.exp(b_col + m - m_t)
        c_old = c_ref[d]
        num = _dot(s.astype(BF16), vb) + inter_w * _dot(qb, c_old.astype(BF16))
        den = jnp.sum(s, axis=1, keepdims=True) + inter_w * jnp.sum(q * n, axis=1, keepdims=True)
        h = num / jnp.maximum(jnp.abs(den), jnp.exp(-m_t))
        hs_ref[rows, :] += h

        a_col = g - b_col + li_col
        m_loc = jnp.max(a_col, axis=0, keepdims=True)
        kw = k * jnp.exp(a_col - m_loc)
        c_loc = _dot_tn(kw.astype(BF16), vb)
        n_loc = jnp.sum(kw, axis=0, keepdims=True)
        m_new = jnp.maximum(g + m, m_loc)
        decay = jnp.exp(g + m - m_new)
        inj = jnp.exp(m_loc - m_new)
        c_ref[d] = decay * c_old + inj * c_loc
        return decay * n + inj * n_loc, m_new

    def body(i, carry):
        nf, mf, nb, mb = carry
        nf, mf = one_dir(i, 0, nf, mf)
        nb, mb = one_dir(nc - 1 - i, 1, nb, mb)
        return nf, mf, nb, mb

    zn = jnp.zeros((1, MLSTM_DK), F32)
    zm = jnp.zeros((1, 1), F32)
    lax.fori_loop(0, nc, body, (zn, zm, zn, zm), unroll=4)

    def fin(i, carry):
        rows = pl.ds(pl.multiple_of(i * L, L), L)
        hh = hs_ref[rows, :]
        r = lax.rsqrt(jnp.mean(hh * hh, axis=-1, keepdims=True) + RMS_EPS)
        y_ref[rows, :] = (hh * r * nw_ref[...] * jax.nn.sigmoid(og_ref[rows, :].astype(F32))).astype(y_ref.dtype)
        return carry

    lax.fori_loop(0, nc, fin, 0)


def _mlstm(u, gc, gr, bias_c, bias_r, norm_w):
    bsz, seq, _ = u.shape
    L = MLSTM_CHUNK
    nc = seq // L
    H, dk, dv = MLSTM_HEADS, MLSTM_DK, MLSTM_DV
    k_off = H * dk // dk
    v_off = 2 * H * dk // dv
    o_off = v_off + H
    return pl.pallas_call(
        functools.partial(_mlstm_body, seq=seq),
        grid=(bsz, H),
        in_specs=[
            pl.BlockSpec((None, seq, dk), lambda b, h: (b, 0, h)),
            pl.BlockSpec((None, seq, dk), lambda b, h: (b, 0, k_off + h)),
            pl.BlockSpec((None, seq, dv), lambda b, h: (b, 0, v_off + h)),
            pl.BlockSpec((None, seq, dv), lambda b, h: (b, 0, o_off + h)),
            pl.BlockSpec((None, None, nc, L, 4), lambda b, h: (b, h, 0, 0, 0)),
            pl.BlockSpec((None, None, nc, 4, L), lambda b, h: (b, h, 0, 0, 0)),
            pl.BlockSpec((None, 1, 4), lambda b, h: (h, 0, 0)),
            pl.BlockSpec((None, 4, 1), lambda b, h: (h, 0, 0)),
            pl.BlockSpec((1, dv), lambda b, h: (0, h)),
        ],
        out_specs=pl.BlockSpec((None, seq, dv), lambda b, h: (b, 0, h)),
        out_shape=jax.ShapeDtypeStruct((bsz, seq, H * dv), BF16),
        scratch_shapes=[pltpu.VMEM((seq, dv), F32), pltpu.VMEM((2, dk, dv), F32)],
        compiler_params=_params(("parallel", "arbitrary")),
        name="mlstm",
    )(u, u, u, u, gc, gr, bias_c, bias_r, norm_w.reshape(1, H * dv))


def _cumsum_rows(x, reverse):
    n = x.shape[0]
    ridx = lax.broadcasted_iota(jnp.int32, x.shape, 0)
    s = 1
    while s < n:
        if reverse:
            x = x + jnp.where(ridx < n - s, pltpu.roll(x, n - s, axis=0), 0.0)
        else:
            x = x + jnp.where(ridx >= s, pltpu.roll(x, s, axis=0), 0.0)
        s *= 2
    return x


def _hgrn_body(q_ref, ff_ref, fb_ref, i_ref, g_ref, lbl_ref, nw_ref, y_ref, os_ref, st_ref, *, seq, layer):
    L = HGRN_CHUNK
    SB = HGRN_SUB
    nc = seq // L
    os_ref[...] = jnp.zeros_like(os_ref)
    st_ref[...] = jnp.zeros_like(st_ref)
    rowi = lax.broadcasted_iota(jnp.int32, (L, HGRN_DK), 0)

    def lower_bound(d):
        lg = lbl_ref[d]
        e = jnp.exp(lg - jnp.max(lg, axis=0, keepdims=True))
        sm = e / jnp.sum(e, axis=0, keepdims=True)
        return jnp.sum(sm[:layer + 1, :], axis=0, keepdims=True)

    lbs = (lower_bound(0), lower_bound(1))

    def one_dir(cidx, d):
        rows = pl.ds(pl.multiple_of(cidx * L, L), L)
        qr = q_ref[rows, :].astype(F32)
        q = qr * jax.nn.sigmoid(qr)
        vb = i_ref[rows, :].astype(BF16)
        fr = (ff_ref if d == 0 else fb_ref)[rows, :]
        lb = lbs[d]
        f = lb + (1.0 - lb) * jax.nn.sigmoid(fr)
        k = 1.0 - f
        lf = jnp.log(f)
        b = _cumsum_rows(lf, reverse=(d == 1))
        g = b[L - 1:L, :] if d == 0 else b[0:1, :]
        qi = (q * jnp.exp(b)).astype(BF16)
        ke = (k * jnp.exp(g - b)).astype(BF16)
        parts = []
        for jb in range(L // SB):
            lo, hi = jb * SB, (jb + 1) * SB
            bm = b[lo + SB // 2:lo + SB // 2 + 1, :]
            qm = (q[lo:hi, :] * jnp.exp(b[lo:hi, :] - bm)).astype(BF16)
            reach = (rowi < hi) if d == 0 else (rowi >= lo)
            km = (k * jnp.exp(jnp.where(reach, bm - b, 0.0))).astype(BF16)
            tq = lo + lax.broadcasted_iota(jnp.int32, (SB, L), 0)
            ts = lax.broadcasted_iota(jnp.int32, (SB, L), 1)
            keep = (ts <= tq) if d == 0 else (ts >= tq)
            parts.append(jnp.where(keep, _dot_nt(qm, km), 0.0))
        a = jnp.concatenate(parts, axis=0)
        st = st_ref[d]
        o = _dot(a.astype(BF16), vb) + _dot_nt(qi, st.astype(BF16))
        st_ref[d] = st * jnp.exp(g) + _dot_tn(vb, ke)
        os_ref[rows, :] += o

    def body(i, carry):
        one_dir(i, 0)
        one_dir(nc - 1 - i, 1)
        return carry

    lax.fori_loop(0, nc, body, 0, unroll=4)

    FL = 256

    def fin(i, carry):
        rows = pl.ds(pl.multiple_of(i * FL, FL), FL)
        hh = os_ref[rows, :]
        r = lax.rsqrt(jnp.mean(hh * hh, axis=-1, keepdims=True) + RMS_EPS)
        gg = g_ref[rows, :].astype(F32)
        y_ref[rows, :] = (hh * r * nw_ref[...] * (gg * jax.nn.sigmoid(gg))).astype(y_ref.dtype)
        return carry

    lax.fori_loop(0, seq // FL, fin, 0)


def _hgrn(u, uf, lb_logits, norm_w, layer):
    bsz, seq, _ = u.shape
    H, dk = HGRN_HEADS, HGRN_DK
    base = (2 * MLSTM_HEADS * MLSTM_DK + 2 * A_WIDTH) // dk
    slots = lb_logits.shape[1]
    return pl.pallas_call(
        functools.partial(_hgrn_body, seq=seq, layer=layer),
        grid=(bsz, H),
        in_specs=[
            pl.BlockSpec((None, seq, dk), lambda b, h: (b, 0, base + h)),
            pl.BlockSpec((None, seq, dk), lambda b, h: (b, 0, h)),
            pl.BlockSpec((None, seq, dk), lambda b, h: (b, 0, H + h)),
            pl.BlockSpec((None, seq, dk), lambda b, h: (b, 0, base + H + h)),
            pl.BlockSpec((None, seq, dk), lambda b, h: (b, 0, base + 2 * H + h)),
            pl.BlockSpec((2, slots, dk), lambda b, h: (0, 0, h)),
            pl.BlockSpec((1, dk), lambda b, h: (0, h)),
        ],
        out_specs=pl.BlockSpec((None, seq, dk), lambda b, h: (b, 0, h)),
        out_shape=jax.ShapeDtypeStruct((bsz, seq, H * dk), BF16),
        scratch_shapes=[pltpu.VMEM((seq, dk), F32), pltpu.VMEM((2, dk, dk), F32)],
        compiler_params=_params(("parallel", "arbitrary")),
        name="hgrn2",
    )(u, uf, uf, u, u, lb_logits, norm_w.reshape(1, H * dk))


def _na_body(q_ref, k_ref, v_ref, tb_ref, o_ref, bias_ref, *, rows):
    W = GRID_W
    G, U, kh = NA_GROUP, NA_UNION, NA_KH
    ng = rows // G
    scale = NA_DH ** -0.5

    @pl.when(pl.program_id(1) == 0)
    def _():
        for c, (delta, offs) in enumerate(_na_group_classes(rows)):
            for i in range(G):
                qr = pl.ds(i * W, W)
                for k0 in range(0, U, 2):
                    v0 = offs[i] <= k0 < offs[i] + kh
                    v1 = k0 + 1 < U and offs[i] <= k0 + 1 < offs[i] + kh
                    dr = delta + k0 - i + kh - 1
                    if k0 + 1 >= U:
                        blk = tb_ref[1, dr][:, :W] if v0 else jnp.full((W, W), NEG_BIG, F32)
                        bias_ref[c, qr, pl.ds(k0 * W, W)] = blk
                        continue
                    if v0 and v1:
                        blk = tb_ref[0, dr]
                    elif v0:
                        blk = tb_ref[1, dr]
                    elif v1:
                        blk = tb_ref[2, dr + 1]
                    else:
                        blk = jnp.full((W, 2 * W), NEG_BIG, F32)
                    bias_ref[c, qr, pl.ds(k0 * W, 2 * W)] = blk

    def body(gi, carry):
        r0 = gi * G
        us = jnp.clip(r0 - NA_KH // 2, 0, rows - U)
        cls = jnp.where(gi == 0, 0, jnp.where(gi == ng - 1, 2, 1))
        qrows = pl.ds(pl.multiple_of(r0 * W, G * W), G * W)
        kwin = pl.ds(pl.multiple_of(us * W, W), U * W)
        s = _dot_nt(q_ref[qrows, :], k_ref[kwin, :]) * scale + bias_ref[cls]
        m = jnp.max(s, axis=-1, keepdims=True)
        p = jnp.exp(s - m)
        den = jnp.sum(p, axis=-1, keepdims=True)
        o = _dot(p.astype(BF16), v_ref[kwin, :]) / den
        o_ref[qrows, :] = o.astype(o_ref.dtype)
        return carry

    lax.fori_loop(0, ng, body, 0, unroll=8)


def _na_group_classes(rows):
    G, U, kh = NA_GROUP, NA_UNION, NA_KH

    def info(r0):
        us = min(max(r0 - kh // 2, 0), rows - U)
        return us - r0, tuple(min(max(r0 + i - kh // 2, 0), rows - kh) - us for i in range(G))

    ng = rows // G
    infos = [info(G * g) for g in range(ng)]
    classes = [infos[0], infos[1], infos[-1]]
    assert rows % G == 0 and ng >= 3 and all(infos[g] == classes[1] for g in range(1, ng - 1))
    assert all(0 <= o and o + kh <= U for c in classes for o in c[1])
    return classes


def _na_bias_table(rpb):
    W = GRID_W
    colv = jnp.arange(W)
    col_start = jnp.clip(colv - NA_KW // 2, 0, W - NA_KW)
    col_mask = (colv[None, :] >= col_start[:, None]) & (colv[None, :] < col_start[:, None] + NA_KW)
    dc_idx = jnp.clip(colv[None, :] - colv[:, None] + NA_KW - 1, 0, 2 * NA_KW - 2)
    t = jnp.where(col_mask[None, None], rpb.astype(F32)[:, :, dc_idx], NEG_BIG)
    neg = jnp.full_like(t, NEG_BIG)
    t_next = jnp.concatenate([t[:, 1:], neg[:, :1]], axis=1)
    forms = [jnp.concatenate(pair, axis=-1) for pair in ((t, t_next), (t, neg), (neg, t))]
    return jnp.stack(forms, axis=1)


def _na(qkv, bias_tbl):
    bsz, seq, _ = qkv.shape
    rows = seq // GRID_W
    H, dh = NA_HEADS, NA_DH
    return pl.pallas_call(
        functools.partial(_na_body, rows=rows),
        grid=(H, bsz),
        in_specs=[
            pl.BlockSpec((None, seq, dh), lambda h, b: (b, 0, h)),
            pl.BlockSpec((None, seq, dh), lambda h, b: (b, 0, H + h)),
            pl.BlockSpec((None, seq, dh), lambda h, b: (b, 0, 2 * H + h)),
            pl.BlockSpec((None,) + bias_tbl.shape[1:], lambda h, b: (h, 0, 0, 0, 0)),
        ],
        out_specs=pl.BlockSpec((None, seq, dh), lambda h, b: (b, 0, h)),
        out_shape=jax.ShapeDtypeStruct((bsz, seq, H * dh), BF16),
        scratch_shapes=[pltpu.VMEM((3, NA_GROUP * GRID_W, NA_UNION * GRID_W), F32)],
        compiler_params=_params(("arbitrary", "arbitrary")),
        name="natten",
    )(qkv, qkv, qkv, bias_tbl)


def _router_body(x_ref, w_ref, b_ref, e_ref, g_ref):
    logits = _dot(x_ref[...], w_ref[...], precision=HIGHEST) + b_ref[...]
    lane = lax.broadcasted_iota(jnp.int32, logits.shape, 1)
    nl = logits.shape[1]
    m1 = jnp.max(logits, axis=-1, keepdims=True)
    i1 = jnp.min(jnp.where(logits == m1, lane, nl), axis=-1, keepdims=True)
    rest = jnp.where(lane == i1, -jnp.inf, logits)
    m2 = jnp.max(rest, axis=-1, keepdims=True)
    i2 = jnp.min(jnp.where(rest == m2, lane, nl), axis=-1, keepdims=True)
    ex = jnp.exp(m2 - m1)
    g1 = 1.0 / (1.0 + ex)
    g2 = ex / (1.0 + ex)
    e_ref[...] = jnp.where(lane == 0, i1, jnp.where(lane == 1, i2, 0))
    g_ref[...] = jnp.where(lane == 0, g1, jnp.where(lane == 1, g2, 0.0))


def _router(x, w_router, b_router, tm):
    n, d = x.shape
    ne = w_router.shape[1]
    w = jnp.zeros((d, V7X_LANES), F32).at[:, :ne].set(w_router.astype(F32))
    b = jnp.full((1, V7X_LANES), NEG_BIG, F32).at[0, :ne].set(b_router.astype(F32))
    return pl.pallas_call(
        _router_body,
        grid=(n // tm,),
        in_specs=[
            pl.BlockSpec((tm, d), lambda i: (i, 0)),
            pl.BlockSpec((d, V7X_LANES), lambda i: (0, 0)),
            pl.BlockSpec((1, V7X_LANES), lambda i: (0, 0)),
        ],
        out_specs=[pl.BlockSpec((tm, V7X_LANES), lambda i: (i, 0)), pl.BlockSpec((tm, V7X_LANES), lambda i: (i, 0))],
        out_shape=[jax.ShapeDtypeStruct((n, V7X_LANES), jnp.int32), jax.ShapeDtypeStruct((n, V7X_LANES), F32)],
        compiler_params=_params(("parallel",)),
        name="router_top2",
    )(x, w, b)


def _pack_bf16_pairs(xb):
    half = xb.shape[1] // 2
    lo = lax.bitcast_convert_type(xb[:, :half], jnp.uint16).astype(jnp.uint32)
    hi = lax.bitcast_convert_type(xb[:, half:], jnp.uint16).astype(jnp.uint32)
    return lo | (hi << 16)


def _moe_body(te_ref, nv_ref, xp_ref, wg_ref, wu_ref, wd_ref, o_ref, xb_ref, wgb_ref, wub_ref, wdb_ref, *, sub):
    t = pl.program_id(0)
    f = pl.program_id(1)
    nv = nv_ref[t]
    tm, half = xp_ref.shape
    tf = wg_ref.shape[1]
    d = o_ref.shape[1]
    nc = MXU_COLS

    @pl.when(f == 0)
    def _():
        o_ref[...] = jnp.zeros_like(o_ref)
        for sb in range(tm // sub):
            rows = pl.ds(sb * sub, sub)
            w = xp_ref[rows, :]
            xb_ref[rows, :half] = lax.bitcast_convert_type(w << 16, F32).astype(BF16)
            xb_ref[rows, half:] = lax.bitcast_convert_type(w & jnp.uint32(0xFFFF0000), F32).astype(BF16)

    for sb in range(tm // sub):
        rows = pl.ds(sb * sub, sub)

        @pl.when(sb * sub < nv)
        def _():
            hs = []
            for c in range(tf // nc):
                cols = pl.ds(c * nc, nc)
                if sb == 0:
                    wg = wg_ref[:, cols].astype(BF16)
                    wu = wu_ref[:, cols].astype(BF16)
                    wgb_ref[:, cols] = wg
                    wub_ref[:, cols] = wu
                else:
                    wg = wgb_ref[:, cols]
                    wu = wub_ref[:, cols]
                g = _dot(xb_ref[rows, :], wg)
                u = _dot(xb_ref[rows, :], wu)
                hs.append((g * jax.nn.sigmoid(g) * u).astype(BF16))
            h = jnp.concatenate(hs, axis=1)
            for c in range(d // nc):
                cols = pl.ds(c * nc, nc)
                if sb == 0:
                    wd = wd_ref[:, cols].astype(BF16)
                    wdb_ref[:, cols] = wd
                else:
                    wd = wdb_ref[:, cols]
                o_ref[rows, cols] += _dot(h, wd)


def _moe_experts(xp, tile_expert, tile_valid, wg, wu, wd, tm, tf, sub):
    rows, half = xp.shape
    d = 2 * half
    n_tiles = rows // tm
    fdim = wg.shape[2]
    nf = fdim // tf

    def fidx(t, f, nv):
        return jnp.where(nv[t] > 0, f, nf - 1)

    return pl.pallas_call(
        functools.partial(_moe_body, sub=sub),
        grid_spec=pltpu.PrefetchScalarGridSpec(
            num_scalar_prefetch=2,
            grid=(n_tiles, nf),
            in_specs=[
                pl.BlockSpec((tm, half), lambda t, f, te, nv: (t, 0), pipeline_mode=pl.Buffered(1)),
                pl.BlockSpec((None, d, tf), lambda t, f, te, nv: (te[t], 0, fidx(t, f, nv))),
                pl.BlockSpec((None, d, tf), lambda t, f, te, nv: (te[t], 0, fidx(t, f, nv))),
                pl.BlockSpec((None, tf, d), lambda t, f, te, nv: (te[t], fidx(t, f, nv), 0)),
            ],
            out_specs=pl.BlockSpec((tm, d), lambda t, f, te, nv: (t, 0), pipeline_mode=pl.Buffered(1)),
            scratch_shapes=[pltpu.VMEM((tm, d), BF16), pltpu.VMEM((d, tf), BF16), pltpu.VMEM((d, tf), BF16), pltpu.VMEM((tf, d), BF16)],
        ),
        out_shape=jax.ShapeDtypeStruct((rows, d), F32),
        compiler_params=pltpu.CompilerParams(dimension_semantics=("arbitrary", "arbitrary"), vmem_limit_bytes=MOE_VMEM_LIMIT),
        name="moe_experts",
    )(tile_expert, tile_valid, xp, wg, wu, wd)


def _combine_ln_body(x_ref, y0_ref, y1_ref, g_ref, w_ref, b_ref, o_ref, ob_ref):
    g = g_ref[...]
    mix = y0_ref[...] * g[:, 0:1] + y1_ref[...] * g[:, 1:2]
    out = _layer_norm(ALPHA * x_ref[...] + mix, w_ref[...], b_ref[...])
    o_ref[...] = out
    ob_ref[...] = out.astype(BF16)


def _combine_ln(x, y0, y1, gates, ln_w, ln_b, tm):
    n, d = x.shape
    row_spec = pl.BlockSpec((tm, d), lambda i: (i, 0))
    vec_spec = pl.BlockSpec((1, d), lambda i: (0, 0))
    return pl.pallas_call(
        _combine_ln_body,
        grid=(n // tm,),
        in_specs=[row_spec, row_spec, row_spec, pl.BlockSpec((tm, V7X_LANES), lambda i: (i, 0)), vec_spec, vec_spec],
        out_specs=[row_spec, row_spec],
        out_shape=[jax.ShapeDtypeStruct((n, d), F32), jax.ShapeDtypeStruct((n, d), BF16)],
        compiler_params=_params(("parallel",)),
        name="combine_ln",
    )(x, y0, y1, gates, ln_w.reshape(1, d), ln_b.reshape(1, d))


def _moe(x, xb, w_router, b_router, wg, wu, wd, ln_w, ln_b):
    n, d = x.shape
    tm = MOE_TILE
    nk = n * TOP_K
    e_out, g_out = _router(x, w_router, b_router, 512)
    e_flat = e_out[:, :TOP_K].reshape(-1)
    onehot = (e_flat[:, None] == jnp.arange(N_EXPERTS, dtype=jnp.int32)[None, :]).astype(jnp.int32)
    csum = jnp.cumsum(onehot, axis=0)
    counts = csum[-1]
    pos = jnp.take_along_axis(csum, e_flat[:, None], axis=1)[:, 0] - 1
    padded = (counts + tm - 1) // tm * tm
    pad_end = jnp.cumsum(padded)
    pad_start = pad_end - padded
    dest = pad_start[e_flat] + pos
    n_tiles = -(-nk // tm) + N_EXPERTS
    row_tok = (jnp.arange(n_tiles * tm, dtype=jnp.int32) % n).at[dest].set(jnp.arange(nk, dtype=jnp.int32) // TOP_K, unique_indices=True)
    n_active = (pad_end[-1] // tm).astype(jnp.int32)
    tile_all = jnp.arange(n_tiles, dtype=jnp.int32)
    tile_ids = jnp.minimum(tile_all, n_active - 1)
    tile_expert = jnp.sum((pad_end[None, :] <= (tile_ids * tm)[:, None]).astype(jnp.int32), axis=1)
    tile_expert = jnp.minimum(tile_expert, N_EXPERTS - 1)
    tile_valid = jnp.clip((pad_start + counts)[tile_expert] - tile_ids * tm, 0, tm)
    tile_valid = jnp.where(tile_all < n_active, tile_valid, 0).astype(jnp.int32)
    xs = _pack_bf16_pairs(xb)[row_tok]
    ys = _moe_experts(xs, tile_expert, tile_valid, wg, wu, wd, tm, 512, MOE_SUB)
    dest2 = dest.reshape(n, TOP_K)
    return _combine_ln(x, ys[dest2[:, 0]], ys[dest2[:, 1]], g_out, ln_w, ln_b, 256)


def kernel(x, p, ln_w, ln_b, rec_w_in, mlstm_gate_bias, mlstm_norm_w, hgrn_lb_logits, hgrn_norm_w, rec_w_out, ffn_w_gate, ffn_w_up, ffn_w_down, na_w_qkv, na_rpb, na_w_out, moe_w_router, moe_b_router, moe_w_gate, moe_w_up, moe_w_down, ple_w_gate, ple_w_proj):
    bsz, seq, d = x.shape
    n = bsz * seq
    depth = ln_w.shape[0]
    xf = x.reshape(n, d).astype(F32)
    xb = xf.astype(BF16)
    H = MLSTM_HEADS
    gate_lo = 2 * H * MLSTM_DK + 2 * A_WIDTH
    gate_hi = gate_lo + 4 * H
    for i in range(depth):
        j = i // 2
        if i % 2 == 0:
            w_in = rec_w_in[j]
            qb_lo, ff_lo, ib_lo = gate_hi, gate_hi + B_WIDTH, gate_hi + 3 * B_WIDTH
            w_main = jnp.concatenate([w_in[:, :gate_lo], w_in[:, qb_lo:ff_lo], w_in[:, ib_lo:]], axis=1).astype(BF16)
            w_forget = w_in[:, ff_lo:ib_lo].astype(BF16)
            w_gate = jnp.zeros((d, V7X_LANES), BF16).at[:, :4 * H].set(w_in[:, gate_lo:gate_hi].astype(BF16))
            u = _matmul(xb, w_main, BF16, 1024, 1024).reshape(bsz, seq, -1)
            uf = _matmul(xb, w_forget, F32, 1024, 1024).reshape(bsz, seq, -1)
            graw = _matmul(xb, w_gate, F32, 1024, V7X_LANES)[:, :4 * H].reshape(bsz, seq, 4, H)
            L = MLSTM_CHUNK
            gc = graw.transpose(0, 3, 1, 2).reshape(bsz, H, seq // L, L, 4)
            gr = gc.transpose(0, 1, 2, 4, 3)
            bias = mlstm_gate_bias[j].astype(F32).T
            y_a = _mlstm(u, gc, gr, bias.reshape(H, 1, 4), bias.reshape(H, 4, 1), mlstm_norm_w[j].astype(F32))
            y_b = _hgrn(u, uf, hgrn_lb_logits.astype(F32), hgrn_norm_w[j].astype(F32), j)
            mix_in = (y_a.reshape(n, -1), y_b.reshape(n, -1))
            xf, xb = _matmul_ln(mix_in, rec_w_out[j].astype(BF16), xf, ln_w[i, 0], ln_b[i, 0], 512)
            hid = _swiglu_up(xb, ffn_w_gate[j], ffn_w_up[j], 1024, 512)
            xf, xb = _matmul_ln((hid,), ffn_w_down[j].astype(BF16), xf, ln_w[i, 1], ln_b[i, 1], 256)
        else:
            qkv = _matmul(xb, na_w_qkv[j], BF16, 1024, 1024).reshape(bsz, seq, -1)
            att = _na(qkv, _na_bias_table(na_rpb[j])).reshape(n, -1)
            xf, xb = _matmul_ln((att,), na_w_out[j].astype(BF16), xf, ln_w[i, 0], ln_b[i, 0], 512)
            xf, xb = _moe(xf, xb, moe_w_router[j], moe_b_router[j], moe_w_gate[j], moe_w_up[j], moe_w_down[j], ln_w[i, 1], ln_b[i, 1])
        xf, xb = _ple(xf, xb, p[i].reshape(n, -1), ple_w_gate[i].astype(BF16), ple_w_proj[i].astype(BF16), 1024, 1024)
    return xf.reshape(bsz, seq, d)
```

```python
import functools

import jax
import jax.numpy as jnp
import numpy as np
from jax import lax
from jax.experimental import pallas as pl
from jax.experimental.pallas import tpu as pltpu

F32 = jnp.float32
BF16 = jnp.bfloat16
HIGHEST = lax.Precision.HIGHEST

DEPTH = 2
ALPHA = (2 * DEPTH) ** 0.25
LN_EPS = 1e-5
RMS_EPS = 1e-6
GRID_W = 64

MLSTM_HEADS = 4
MLSTM_DK = 128
MLSTM_DV = 256
GATE_CAP = 15.0
HGRN_HEADS = 8
HGRN_DK = 128
A_WIDTH = MLSTM_HEADS * MLSTM_DV
B_WIDTH = HGRN_HEADS * HGRN_DK

NA_DH = 128
NA_HEADS = 16
NA_KH = 8
NA_KW = 16

N_EXPERTS = 8
TOP_K = 2

MLSTM_CHUNK = 256
HGRN_CHUNK = 128
HGRN_SUB = 32
MOE_TILE = 1536
MOE_SUB = 512
LN_ROW_GROUP = 128
NA_GROUP = 4
NA_UNION = NA_KH + NA_GROUP - 1

V7X_LANES = 128
MXU_COLS = 256
VMEM_LIMIT = 56 * 1024 * 1024
MOE_VMEM_LIMIT = 60 * 1024 * 1024
NEG_BIG = -1e30


def _params(sem):
    return pltpu.CompilerParams(dimension_semantics=sem, vmem_limit_bytes=VMEM_LIMIT)


def _dot(a, b, **kw):
    return jnp.dot(a, b, preferred_element_type=F32, **kw)


def _dot_nt(a, b):
    return lax.dot_general(a, b, (((1,), (1,)), ((), ())), preferred_element_type=F32)


def _dot_tn(a, b):
    return lax.dot_general(a, b, (((0,), (0,)), ((), ())), preferred_element_type=F32)


def _layer_norm(y, w, b):
    mu = jnp.mean(y, axis=-1, keepdims=True)
    yc = y - mu
    var = jnp.mean(yc * yc, axis=-1, keepdims=True)
    return yc * lax.rsqrt(var + LN_EPS) * w + b


def _log_sigmoid(z):
    return jnp.minimum(z, 0.0) - jnp.log(1.0 + jnp.exp(-jnp.abs(z)))


def _mm_body(a_ref, w_ref, o_ref, wb_ref):
    @pl.when(pl.program_id(1) == 0)
    def _():
        wb_ref[...] = w_ref[...].astype(BF16)

    o_ref[...] = _dot(a_ref[...], wb_ref[...]).astype(o_ref.dtype)


def _matmul(a, w, out_dtype, tm, tn):
    m, k = a.shape
    n = w.shape[1]
    return pl.pallas_call(
        _mm_body,
        grid=(n // tn, m // tm),
        in_specs=[pl.BlockSpec((tm, k), lambda j, i: (i, 0)), pl.BlockSpec((k, tn), lambda j, i: (0, j))],
        out_specs=pl.BlockSpec((tm, tn), lambda j, i: (i, j)),
        out_shape=jax.ShapeDtypeStruct((m, n), out_dtype),
        scratch_shapes=[pltpu.VMEM((k, tn), BF16)],
        compiler_params=_params(("arbitrary", "arbitrary")),
        name="matmul",
    )(a, w)


def _ple_epilogue(p_ref, wpg_ref, wpp_ref, xs_ref, o_ref, ob_ref):
    for c in range(o_ref.shape[1] // MXU_COLS):
        cols = pl.ds(c * MXU_COLS, MXU_COLS)
        gate = jax.nn.sigmoid(_dot(xs_ref[...], wpg_ref[:, cols]))
        proj = _dot(p_ref[...].astype(BF16), wpp_ref[:, cols])
        new = o_ref[:, cols] + gate * proj
        o_ref[:, cols] = new
        ob_ref[:, cols] = new.astype(BF16)


def _mm_ln_body(*refs, na, ple):
    a_refs = refs[:na]
    w_ref, res_ref, g_ref, b_ref = refs[na:na + 4]
    rest = refs[na + 4:]
    if ple:
        p_ref, wpg_ref, wpp_ref, o_ref, ob_ref, xs_ref = rest
    else:
        o_ref, ob_ref = rest
    part = LN_ROW_GROUP
    for r in range(res_ref.shape[0] // part):
        rows = pl.ds(r * part, part)
        y = ALPHA * res_ref[rows, :]
        k0 = 0
        for a_ref in a_refs:
            ka = a_ref.shape[1]
            y = y + _dot(a_ref[rows, :], w_ref[k0:k0 + ka, :])
            k0 += ka
        out = _layer_norm(y, g_ref[...], b_ref[...])
        o_ref[rows, :] = out
        (xs_ref if ple else ob_ref)[rows, :] = out.astype(BF16)
    if ple:
        _ple_epilogue(p_ref, wpg_ref, wpp_ref, xs_ref, o_ref, ob_ref)


def _matmul_ln(a_parts, w, res, ln_w, ln_b, tm, ple=None):
    m = res.shape[0]
    k, n = w.shape
    assert sum(a.shape[1] for a in a_parts) == k
    row_spec = pl.BlockSpec((tm, n), lambda i: (i, 0))
    vec_spec = pl.BlockSpec((1, n), lambda i: (0, 0))

    def whole(arr):
        return pl.BlockSpec(arr.shape, lambda i: (0, 0), pipeline_mode=pl.Buffered(1))

    in_specs = [pl.BlockSpec((tm, a.shape[1]), lambda i: (i, 0)) for a in a_parts] + [whole(w), row_spec, vec_spec, vec_spec]
    args = [*a_parts, w, res, ln_w.reshape(1, n), ln_b.reshape(1, n)]
    scratch = []
    if ple is not None:
        p, wpg, wpp = ple
        in_specs += [pl.BlockSpec((tm, p.shape[1]), lambda i: (i, 0)), whole(wpg), whole(wpp)]
        args += [p, wpg, wpp]
        scratch = [pltpu.VMEM((tm, n), BF16)]
    return pl.pallas_call(
        functools.partial(_mm_ln_body, na=len(a_parts), ple=ple is not None),
        grid=(m // tm,),
        in_specs=in_specs,
        out_specs=[row_spec, row_spec],
        out_shape=[jax.ShapeDtypeStruct((m, n), F32), jax.ShapeDtypeStruct((m, n), BF16)],
        scratch_shapes=scratch,
        compiler_params=_params(("parallel",)),
        name="matmul_ln",
    )(*args)


def _swiglu_up_body(a_ref, wg_ref, wu_ref, o_ref, wgb_ref, wub_ref):
    @pl.when(pl.program_id(1) == 0)
    def _():
        wgb_ref[...] = wg_ref[...].astype(BF16)
        wub_ref[...] = wu_ref[...].astype(BF16)

    for c in range(o_ref.shape[1] // MXU_COLS):
        cols = pl.ds(c * MXU_COLS, MXU_COLS)
        g = _dot(a_ref[...], wgb_ref[:, cols])
        u = _dot(a_ref[...], wub_ref[:, cols])
        o_ref[:, cols] = (g * jax.nn.sigmoid(g) * u).astype(o_ref.dtype)


def _swiglu_up(a, wg, wu, tm, tf):
    m, k = a.shape
    f = wg.shape[1]
    return pl.pallas_call(
        _swiglu_up_body,
        grid=(f // tf, m // tm),
        in_specs=[
            pl.BlockSpec((tm, k), lambda j, i: (i, 0)),
            pl.BlockSpec((k, tf), lambda j, i: (0, j)),
            pl.BlockSpec((k, tf), lambda j, i: (0, j)),
        ],
        out_specs=pl.BlockSpec((tm, tf), lambda j, i: (i, j)),
        out_shape=jax.ShapeDtypeStruct((m, f), BF16),
        scratch_shapes=[pltpu.VMEM((k, tf), BF16), pltpu.VMEM((k, tf), BF16)],
        compiler_params=_params(("arbitrary", "arbitrary")),
        name="swiglu_up",
    )(a, wg, wu)


def _mlstm_body(q_ref, k_ref, v_ref, og_ref, gc_ref, gr_ref, bc_ref, br_ref, nw_ref, y_ref, hs_ref, c_ref, *, seq):
    L = MLSTM_CHUNK
    nc = seq // L
    row = lax.broadcasted_iota(jnp.int32, (L, L), 0)
    col = lax.broadcasted_iota(jnp.int32, (L, L), 1)
    lower = row >= col
    upper = row <= col
    tril = lower.astype(F32)
    triu = upper.astype(F32)
    hs_ref[...] = jnp.zeros_like(hs_ref)
    c_ref[...] = jnp.zeros_like(c_ref)

    def cap(z):
        return GATE_CAP * jnp.tanh(z / GATE_CAP)

    def one_dir(cidx, d, n, m):
        rows = pl.ds(pl.multiple_of(cidx * L, L), L)
        gcol = cap(gc_ref[cidx] + bc_ref[...])
        grow = cap(gr_ref[cidx] + br_ref[...])
        lcol = _log_sigmoid(gcol)
        lrow = _log_sigmoid(grow)
        if d == 0:
            brow = _dot(lrow, triu, precision=HIGHEST)
            mask = lower
        else:
            brow = _dot(lrow, tril, precision=HIGHEST)
            mask = upper
        li_col = gcol[:, 2 * d:2 * d + 1]
        lf_wide = jnp.broadcast_to(lcol[:, 2 * d + 1:2 * d + 2], (L, V7X_LANES))
        b_col = _cumsum_rows(lf_wide, reverse=(d == 1))[:, 0:1]
        li_row = grow[2 * d:2 * d + 1, :]
        b_row = brow[2 * d + 1:2 * d + 2, :]
        g = b_col[L - 1:L, :] if d == 0 else b_col[0:1, :]

        q = q_ref[rows, :].astype(F32) * (MLSTM_DK ** -0.5)
        k = k_ref[rows, :].astype(F32)
        qb = q.astype(BF16)
        kb = k.astype(BF16)
        vb = v_ref[rows, :].astype(BF16)

        logd = jnp.where(mask, b_col + (li_row - b_row), -jnp.inf)
        m_t = jnp.maximum(jnp.max(logd, axis=1, keepdims=True), b_col + m)
        s = _dot_nt(qb, kb) * jnp.exp(logd - m_t)
        inter_w = jnp.exp(b_col + m - m_t)
        c_old = c_ref[d]
        num = _dot(s.astype(BF16), vb) + inter_w * _dot(qb, c_old.astype(BF16))
        den = jnp.sum(s, axis=1, keepdims=True) + inter_w * jnp.sum(q * n, axis=1, keepdims=True)
        h = num / jnp.maximum(jnp.abs(den), jnp.exp(-m_t))
        hs_ref[rows, :] += h

        a_col = g - b_col + li_col
        m_loc = jnp.max(a_col, axis=0, keepdims=True)
        kw = k * jnp.exp(a_col - m_loc)
        c_loc = _dot_tn(kw.astype(BF16), vb)
        n_loc = jnp.sum(kw, axis=0, keepdims=True)
        m_new = jnp.maximum(g + m, m_loc)
        decay = jnp.exp(g + m - m_new)
        inj = jnp.exp(m_loc - m_new)
        c_ref[d] = decay * c_old + inj * c_loc
        return decay * n + inj * n_loc, m_new

    def body(i, carry):
        nf, mf, nb, mb = carry
        nf, mf = one_dir(i, 0, nf, mf)
        nb, mb = one_dir(nc - 1 - i, 1, nb, mb)
        return nf, mf, nb, mb

    zn = jnp.zeros((1, MLSTM_DK), F32)
    zm = jnp.zeros((1, 1), F32)
    lax.fori_loop(0, nc, body, (zn, zm, zn, zm), unroll=4)

    def fin(i, carry):
        rows = pl.ds(pl.multiple_of(i * L, L), L)
        hh = hs_ref[rows, :]
        r = lax.rsqrt(jnp.mean(hh * hh, axis=-1, keepdims=True) + RMS_EPS)
        y_ref[rows, :] = (hh * r * nw_ref[...] * jax.nn.sigmoid(og_ref[rows, :].astype(F32))).astype(y_ref.dtype)
        return carry

    lax.fori_loop(0, nc, fin, 0)


def _mlstm(u, gc, gr, bias_c, bias_r, norm_w):
    bsz, seq, _ = u.shape
    L = MLSTM_CHUNK
    nc = seq // L
    H, dk, dv = MLSTM_HEADS, MLSTM_DK, MLSTM_DV
    k_off = H * dk // dk
    v_off = 2 * H * dk // dv
    o_off = v_off + H
    return pl.pallas_call(
        functools.partial(_mlstm_body, seq=seq),
        grid=(bsz, H),
        in_specs=[
            pl.BlockSpec((None, seq, dk), lambda b, h: (b, 0, h)),
            pl.BlockSpec((None, seq, dk), lambda b, h: (b, 0, k_off + h)),
            pl.BlockSpec((None, seq, dv), lambda b, h: (b, 0, v_off + h)),
            pl.BlockSpec((None, seq, dv), lambda b, h: (b, 0, o_off + h)),
            pl.BlockSpec((None, None, nc, L, 4), lambda b, h: (b, h, 0, 0, 0)),
            pl.BlockSpec((None, None, nc, 4, L), lambda b, h: (b, h, 0, 0, 0)),
            pl.BlockSpec((None, 1, 4), lambda b, h: (h, 0, 0)),
            pl.BlockSpec((None, 4, 1), lambda b, h: (h, 0, 0)),
            pl.BlockSpec((1, dv), lambda b, h: (0, h)),
        ],
        out_specs=pl.BlockSpec((None, seq, dv), lambda b, h: (b, 0, h)),
        out_shape=jax.ShapeDtypeStruct((bsz, seq, H * dv), BF16),
        scratch_shapes=[pltpu.VMEM((seq, dv), F32), pltpu.VMEM((2, dk, dv), F32)],
        compiler_params=_params(("parallel", "arbitrary")),
        name="mlstm",
    )(u, u, u, u, gc, gr, bias_c, bias_r, norm_w.reshape(1, H * dv))


def _cumsum_rows(x, reverse):
    n = x.shape[0]
    ridx = lax.broadcasted_iota(jnp.int32, x.shape, 0)
    s = 1
    while s < n:
        if reverse:
            x = x + jnp.where(ridx < n - s, pltpu.roll(x, n - s, axis=0), 0.0)
        else:
            x = x + jnp.where(ridx >= s, pltpu.roll(x, s, axis=0), 0.0)
        s *= 2
    return x


def _hgrn_body(q_ref, ff_ref, fb_ref, i_ref, g_ref, lbl_ref, nw_ref, y_ref, os_ref, st_ref, *, seq, layer):
    L = HGRN_CHUNK
    SB = HGRN_SUB
    nc = seq // L
    os_ref[...] = jnp.zeros_like(os_ref)
    st_ref[...] = jnp.zeros_like(st_ref)
    rowi = lax.broadcasted_iota(jnp.int32, (L, HGRN_DK), 0)

    def lower_bound(d):
        lg = lbl_ref[d]
        e = jnp.exp(lg - jnp.max(lg, axis=0, keepdims=True))
        sm = e / jnp.sum(e, axis=0, keepdims=True)
        return jnp.sum(sm[:layer + 1, :], axis=0, keepdims=True)

    lbs = (lower_bound(0), lower_bound(1))

    def one_dir(cidx, d):
        rows = pl.ds(pl.multiple_of(cidx * L, L), L)
        qr = q_ref[rows, :].astype(F32)
        q = qr * jax.nn.sigmoid(qr)
        vb = i_ref[rows, :].astype(BF16)
        fr = (ff_ref if d == 0 else fb_ref)[rows, :]
        lb = lbs[d]
        f = lb + (1.0 - lb) * jax.nn.sigmoid(fr)
        k = 1.0 - f
        lf = jnp.log(f)
        b = _cumsum_rows(lf, reverse=(d == 1))
        g = b[L - 1:L, :] if d == 0 else b[0:1, :]
        qi = (q * jnp.exp(b)).astype(BF16)
        ke = (k * jnp.exp(g - b)).astype(BF16)
        parts = []
        for jb in range(L // SB):
            lo, hi = jb * SB, (jb + 1) * SB
            bm = b[lo + SB // 2:lo + SB // 2 + 1, :]
            qm = (q[lo:hi, :] * jnp.exp(b[lo:hi, :] - bm)).astype(BF16)
            reach = (rowi < hi) if d == 0 else (rowi >= lo)
            km = (k * jnp.exp(jnp.where(reach, bm - b, 0.0))).astype(BF16)
            tq = lo + lax.broadcasted_iota(jnp.int32, (SB, L), 0)
            ts = lax.broadcasted_iota(jnp.int32, (SB, L), 1)
            keep = (ts <= tq) if d == 0 else (ts >= tq)
            parts.append(jnp.where(keep, _dot_nt(qm, km), 0.0))
        a = jnp.concatenate(parts, axis=0)
        st = st_ref[d]
        o = _dot(a.astype(BF16), vb) + _dot_nt(qi, st.astype(BF16))
        st_ref[d] = st * jnp.exp(g) + _dot_tn(vb, ke)
        os_ref[rows, :] += o

    def body(i, carry):
        one_dir(i, 0)
        one_dir(nc - 1 - i, 1)
        return carry

    lax.fori_loop(0, nc, body, 0, unroll=4)

    FL = 256

    def fin(i, carry):
        rows = pl.ds(pl.multiple_of(i * FL, FL), FL)
        hh = os_ref[rows, :]
        r = lax.rsqrt(jnp.mean(hh * hh, axis=-1, keepdims=True) + RMS_EPS)
        gg = g_ref[rows, :].astype(F32)
        y_ref[rows, :] = (hh * r * nw_ref[...] * (gg * jax.nn.sigmoid(gg))).astype(y_ref.dtype)
        return carry

    lax.fori_loop(0, seq // FL, fin, 0)


def _hgrn(u, uf, lb_logits, norm_w, layer):
    bsz, seq, _ = u.shape
    H, dk = HGRN_HEADS, HGRN_DK
    base = (2 * MLSTM_HEADS * MLSTM_DK + 2 * A_WIDTH) // dk
    slots = lb_logits.shape[1]
    return pl.pallas_call(
        functools.partial(_hgrn_body, seq=seq, layer=layer),
        grid=(bsz, H),
        in_specs=[
            pl.BlockSpec((None, seq, dk), lambda b, h: (b, 0, base + h)),
            pl.BlockSpec((None, seq, dk), lambda b, h: (b, 0, h)),
            pl.BlockSpec((None, seq, dk), lambda b, h: (b, 0, H + h)),
            pl.BlockSpec((None, seq, dk), lambda b, h: (b, 0, base + H + h)),
            pl.BlockSpec((None, seq, dk), lambda b, h: (b, 0, base + 2 * H + h)),
            pl.BlockSpec((2, slots, dk), lambda b, h: (0, 0, h)),
            pl.BlockSpec((1, dk), lambda b, h: (0, h)),
        ],
        out_specs=pl.BlockSpec((None, seq, dk), lambda b, h: (b, 0, h)),
        out_shape=jax.ShapeDtypeStruct((bsz, seq, H * dk), BF16),
        scratch_shapes=[pltpu.VMEM((seq, dk), F32), pltpu.VMEM((2, dk, dk), F32)],
        compiler_params=_params(("parallel", "arbitrary")),
        name="hgrn2",
    )(u, uf, uf, u, u, lb_logits, norm_w.reshape(1, H * dk))


def _na_body(q_ref, k_ref, v_ref, tb_ref, o_ref, bias_ref, *, rows):
    W = GRID_W
    G, U, kh = NA_GROUP, NA_UNION, NA_KH
    ng = rows // G
    scale = NA_DH ** -0.5

    @pl.when(pl.program_id(1) == 0)
    def _():
        for c, (delta, offs) in enumerate(_na_group_classes(rows)):
            for i in range(G):
                qr = pl.ds(i * W, W)
                for k0 in range(0, U, 2):
                    v0 = offs[i] <= k0 < offs[i] + kh
                    v1 = k0 + 1 < U and offs[i] <= k0 + 1 < offs[i] + kh
                    dr = delta + k0 - i + kh - 1
                    if k0 + 1 >= U:
                        blk = tb_ref[1, dr][:, :W] if v0 else jnp.full((W, W), NEG_BIG, F32)
                        bias_ref[c, qr, pl.ds(k0 * W, W)] = blk
                        continue
                    if v0 and v1:
                        blk = tb_ref[0, dr]
                    elif v0:
                        blk = tb_ref[1, dr]
                    elif v1:
                        blk = tb_ref[2, dr + 1]
                    else:
                        blk = jnp.full((W, 2 * W), NEG_BIG, F32)
                    bias_ref[c, qr, pl.ds(k0 * W, 2 * W)] = blk

    def body(gi, carry):
        r0 = gi * G
        us = jnp.clip(r0 - NA_KH // 2, 0, rows - U)
        cls = jnp.where(gi == 0, 0, jnp.where(gi == ng - 1, 2, 1))
        qrows = pl.ds(pl.multiple_of(r0 * W, G * W), G * W)
        kwin = pl.ds(pl.multiple_of(us * W, W), U * W)
        s = _dot_nt(q_ref[qrows, :], k_ref[kwin, :]) * scale + bias_ref[cls]
        m = jnp.max(s, axis=-1, keepdims=True)
        p = jnp.exp(s - m)
        den = jnp.sum(p, axis=-1, keepdims=True)
        o = _dot(p.astype(BF16), v_ref[kwin, :]) / den
        o_ref[qrows, :] = o.astype(o_ref.dtype)
        return carry

    lax.fori_loop(0, ng, body, 0, unroll=8)


def _na_group_classes(rows):
    G, U, kh = NA_GROUP, NA_UNION, NA_KH

    def info(r0):
        us = min(max(r0 - kh // 2, 0), rows - U)
        return us - r0, tuple(min(max(r0 + i - kh // 2, 0), rows - kh) - us for i in range(G))

    ng = rows // G
    infos = [info(G * g) for g in range(ng)]
    classes = [infos[0], infos[1], infos[-1]]
    assert rows % G == 0 and ng >= 3 and all(infos[g] == classes[1] for g in range(1, ng - 1))
    assert all(0 <= o and o + kh <= U for c in classes for o in c[1])
    return classes


def _na_bias_table(rpb):
    W = GRID_W
    colv = jnp.arange(W)
    col_start = jnp.clip(colv - NA_KW // 2, 0, W - NA_KW)
    col_mask = (colv[None, :] >= col_start[:, None]) & (colv[None, :] < col_start[:, None] + NA_KW)
    dc_idx = jnp.clip(colv[None, :] - colv[:, None] + NA_KW - 1, 0, 2 * NA_KW - 2)
    t = jnp.where(col_mask[None, None], rpb.astype(F32)[:, :, dc_idx], NEG_BIG)
    neg = jnp.full_like(t, NEG_BIG)
    t_next = jnp.concatenate([t[:, 1:], neg[:, :1]], axis=1)
    forms = [jnp.concatenate(pair, axis=-1) for pair in ((t, t_next), (t, neg), (neg, t))]
    return jnp.stack(forms, axis=1)


def _na(qkv, bias_tbl):
    bsz, seq, _ = qkv.shape
    rows = seq // GRID_W
    H, dh = NA_HEADS, NA_DH
    return pl.pallas_call(
        functools.partial(_na_body, rows=rows),
        grid=(H, bsz),
        in_specs=[
            pl.BlockSpec((None, seq, dh), lambda h, b: (b, 0, h)),
            pl.BlockSpec((None, seq, dh), lambda h, b: (b, 0, H + h)),
            pl.BlockSpec((None, seq, dh), lambda h, b: (b, 0, 2 * H + h)),
            pl.BlockSpec((None,) + bias_tbl.shape[1:], lambda h, b: (h, 0, 0, 0, 0)),
        ],
        out_specs=pl.BlockSpec((None, seq, dh), lambda h, b: (b, 0, h)),
        out_shape=jax.ShapeDtypeStruct((bsz, seq, H * dh), BF16),
        scratch_shapes=[pltpu.VMEM((3, NA_GROUP * GRID_W, NA_UNION * GRID_W), F32)],
        compiler_params=_params(("arbitrary", "arbitrary")),
        name="natten",
    )(qkv, qkv, qkv, bias_tbl)


def _router_body(x_ref, wh_ref, wl_ref, b_ref, e_ref, g_ref):
    x = x_ref[...]
    xh = x.astype(BF16)
    xl = (x - xh.astype(F32)).astype(BF16)
    logits = _dot(xh, wh_ref[...]) + _dot(xl, wh_ref[...]) + _dot(xh, wl_ref[...]) + b_ref[...]
    lane = lax.broadcasted_iota(jnp.int32, logits.shape, 1)
    nl = logits.shape[1]
    m1 = jnp.max(logits, axis=-1, keepdims=True)
    i1 = jnp.min(jnp.where(logits == m1, lane, nl), axis=-1, keepdims=True)
    rest = jnp.where(lane == i1, -jnp.inf, logits)
    m2 = jnp.max(rest, axis=-1, keepdims=True)
    i2 = jnp.min(jnp.where(rest == m2, lane, nl), axis=-1, keepdims=True)
    ex = jnp.exp(m2 - m1)
    g1 = 1.0 / (1.0 + ex)
    g2 = ex / (1.0 + ex)
    e_ref[...] = jnp.where(lane == 0, i1, jnp.where(lane == 1, i2, 0))
    g_ref[...] = jnp.where(lane == 0, g1, jnp.where(lane == 1, g2, 0.0))


def _router(x, w_router, b_router, tm):
    n, d = x.shape
    ne = w_router.shape[1]
    w = jnp.zeros((d, V7X_LANES), F32).at[:, :ne].set(w_router.astype(F32))
    b = jnp.full((1, V7X_LANES), NEG_BIG, F32).at[0, :ne].set(b_router.astype(F32))
    wh = w.astype(BF16)
    wl = (w - wh.astype(F32)).astype(BF16)
    return pl.pallas_call(
        _router_body,
        grid=(n // tm,),
        in_specs=[
            pl.BlockSpec((tm, d), lambda i: (i, 0)),
            pl.BlockSpec((d, V7X_LANES), lambda i: (0, 0)),
            pl.BlockSpec((d, V7X_LANES), lambda i: (0, 0)),
            pl.BlockSpec((1, V7X_LANES), lambda i: (0, 0)),
        ],
        out_specs=[pl.BlockSpec((tm, V7X_LANES), lambda i: (i, 0)), pl.BlockSpec((tm, V7X_LANES), lambda i: (i, 0))],
        out_shape=[jax.ShapeDtypeStruct((n, V7X_LANES), jnp.int32), jax.ShapeDtypeStruct((n, V7X_LANES), F32)],
        compiler_params=_params(("parallel",)),
        name="router_top2",
    )(x, wh, wl, b)


def _pack_bf16_pairs(xb):
    half = xb.shape[1] // 2
    lo = lax.bitcast_convert_type(xb[:, :half], jnp.uint16).astype(jnp.uint32)
    hi = lax.bitcast_convert_type(xb[:, half:], jnp.uint16).astype(jnp.uint32)
    return lo | (hi << 16)


def _moe_body(te_ref, nv_ref, xp_ref, wg_ref, wu_ref, wd_ref, o_ref, xb_ref, wgb_ref, wub_ref, wdb_ref, *, sub):
    t = pl.program_id(0)
    f = pl.program_id(1)
    nv = nv_ref[t]
    tm, half = xp_ref.shape
    tf = wg_ref.shape[1]
    d = o_ref.shape[1]
    nc = MXU_COLS

    @pl.when(f == 0)
    def _():
        o_ref[...] = jnp.zeros_like(o_ref)
        for sb in range(tm // sub):
            rows = pl.ds(sb * sub, sub)
            w = xp_ref[rows, :]
            xb_ref[rows, :half] = lax.bitcast_convert_type(w << 16, F32).astype(BF16)
            xb_ref[rows, half:] = lax.bitcast_convert_type(w & jnp.uint32(0xFFFF0000), F32).astype(BF16)

    for sb in range(tm // sub):
        rows = pl.ds(sb * sub, sub)

        @pl.when(sb * sub < nv)
        def _():
            hs = []
            for c in range(tf // nc):
                cols = pl.ds(c * nc, nc)
                if sb == 0:
                    wg = wg_ref[:, cols].astype(BF16)
                    wu = wu_ref[:, cols].astype(BF16)
                    wgb_ref[:, cols] = wg
                    wub_ref[:, cols] = wu
                else:
                    wg = wgb_ref[:, cols]
                    wu = wub_ref[:, cols]
                g = _dot(xb_ref[rows, :], wg)
                u = _dot(xb_ref[rows, :], wu)
                hs.append((g * jax.nn.sigmoid(g) * u).astype(BF16))
            h = jnp.concatenate(hs, axis=1)
            for c in range(d // nc):
                cols = pl.ds(c * nc, nc)
                if sb == 0:
                    wd = wd_ref[:, cols].astype(BF16)
                    wdb_ref[:, cols] = wd
                else:
                    wd = wdb_ref[:, cols]
                o_ref[rows, cols] += _dot(h, wd)


def _moe_experts(xp, tile_expert, tile_valid, wg, wu, wd, tm, tf, sub):
    rows, half = xp.shape
    d = 2 * half
    n_tiles = rows // tm
    fdim = wg.shape[2]
    nf = fdim // tf

    def fidx(t, f, nv):
        return jnp.where(nv[t] > 0, f, nf - 1)

    return pl.pallas_call(
        functools.partial(_moe_body, sub=sub),
        grid_spec=pltpu.PrefetchScalarGridSpec(
            num_scalar_prefetch=2,
            grid=(n_tiles, nf),
            in_specs=[
                pl.BlockSpec((tm, half), lambda t, f, te, nv: (t, 0), pipeline_mode=pl.Buffered(1)),
                pl.BlockSpec((None, d, tf), lambda t, f, te, nv: (te[t], 0, fidx(t, f, nv))),
                pl.BlockSpec((None, d, tf), lambda t, f, te, nv: (te[t], 0, fidx(t, f, nv))),
                pl.BlockSpec((None, tf, d), lambda t, f, te, nv: (te[t], fidx(t, f, nv), 0)),
            ],
            out_specs=pl.BlockSpec((tm, d), lambda t, f, te, nv: (t, 0), pipeline_mode=pl.Buffered(1)),
            scratch_shapes=[pltpu.VMEM((tm, d), BF16), pltpu.VMEM((d, tf), BF16), pltpu.VMEM((d, tf), BF16), pltpu.VMEM((tf, d), BF16)],
        ),
        out_shape=jax.ShapeDtypeStruct((rows, d), F32),
        compiler_params=pltpu.CompilerParams(dimension_semantics=("arbitrary", "arbitrary"), vmem_limit_bytes=MOE_VMEM_LIMIT),
        name="moe_experts",
    )(tile_expert, tile_valid, xp, wg, wu, wd)


def _combine_ln_body(x_ref, y0_ref, y1_ref, g_ref, w_ref, b_ref, p_ref, wpg_ref, wpp_ref, o_ref, ob_ref, xs_ref):
    g = g_ref[...]
    mix = y0_ref[...] * g[:, 0:1] + y1_ref[...] * g[:, 1:2]
    out = _layer_norm(ALPHA * x_ref[...] + mix, w_ref[...], b_ref[...])
    o_ref[...] = out
    xs_ref[...] = out.astype(BF16)
    _ple_epilogue(p_ref, wpg_ref, wpp_ref, xs_ref, o_ref, ob_ref)


def _combine_ln(x, y0, y1, gates, ln_w, ln_b, ple, tm):
    n, d = x.shape
    p, wpg, wpp = ple
    row_spec = pl.BlockSpec((tm, d), lambda i: (i, 0))
    vec_spec = pl.BlockSpec((1, d), lambda i: (0, 0))

    def whole(arr):
        return pl.BlockSpec(arr.shape, lambda i: (0, 0), pipeline_mode=pl.Buffered(1))

    return pl.pallas_call(
        _combine_ln_body,
        grid=(n // tm,),
        in_specs=[row_spec, row_spec, row_spec, pl.BlockSpec((tm, V7X_LANES), lambda i: (i, 0)), vec_spec, vec_spec,
                  pl.BlockSpec((tm, p.shape[1]), lambda i: (i, 0)), whole(wpg), whole(wpp)],
        out_specs=[row_spec, row_spec],
        out_shape=[jax.ShapeDtypeStruct((n, d), F32), jax.ShapeDtypeStruct((n, d), BF16)],
        scratch_shapes=[pltpu.VMEM((tm, d), BF16)],
        compiler_params=_params(("parallel",)),
        name="combine_ln",
    )(x, y0, y1, gates, ln_w.reshape(1, d), ln_b.reshape(1, d), p, wpg, wpp)


def _moe(x, xb, w_router, b_router, wg, wu, wd, ln_w, ln_b, ple):
    n, d = x.shape
    tm = MOE_TILE
    nk = n * TOP_K
    e_out, g_out = _router(x, w_router, b_router, 512)
    e_flat = e_out[:, :TOP_K].reshape(-1)
    onehot = (e_flat[:, None] == jnp.arange(N_EXPERTS, dtype=jnp.int32)[None, :]).astype(jnp.int32)
    csum = jnp.cumsum(onehot, axis=0)
    counts = csum[-1]
    pos = jnp.take_along_axis(csum, e_flat[:, None], axis=1)[:, 0] - 1
    padded = (counts + tm - 1) // tm * tm
    pad_end = jnp.cumsum(padded)
    pad_start = pad_end - padded
    dest = pad_start[e_flat] + pos
    n_tiles = -(-nk // tm) + N_EXPERTS
    row_tok = (jnp.arange(n_tiles * tm, dtype=jnp.int32) % n).at[dest].set(jnp.arange(nk, dtype=jnp.int32) // TOP_K, unique_indices=True)
    n_active = (pad_end[-1] // tm).astype(jnp.int32)
    tile_all = jnp.arange(n_tiles, dtype=jnp.int32)
    tile_ids = jnp.minimum(tile_all, n_active - 1)
    tile_expert = jnp.sum((pad_end[None, :] <= (tile_ids * tm)[:, None]).astype(jnp.int32), axis=1)
    tile_expert = jnp.minimum(tile_expert, N_EXPERTS - 1)
    tile_valid = jnp.clip((pad_start + counts)[tile_expert] - tile_ids * tm, 0, tm)
    tile_valid = jnp.where(tile_all < n_active, tile_valid, 0).astype(jnp.int32)
    xs = _pack_bf16_pairs(xb)[row_tok]
    ys = _moe_experts(xs, tile_expert, tile_valid, wg, wu, wd, tm, 512, MOE_SUB)
    dest2 = dest.reshape(n, TOP_K)
    return _combine_ln(x, ys[dest2[:, 0]], ys[dest2[:, 1]], g_out, ln_w, ln_b, ple, 256)


def kernel(x, p, ln_w, ln_b, rec_w_in, mlstm_gate_bias, mlstm_norm_w, hgrn_lb_logits, hgrn_norm_w, rec_w_out, ffn_w_gate, ffn_w_up, ffn_w_down, na_w_qkv, na_rpb, na_w_out, moe_w_router, moe_b_router, moe_w_gate, moe_w_up, moe_w_down, ple_w_gate, ple_w_proj):
    bsz, seq, d = x.shape
    n = bsz * seq
    depth = ln_w.shape[0]
    xf = x.reshape(n, d).astype(F32)
    xb = xf.astype(BF16)
    H = MLSTM_HEADS
    gate_lo = 2 * H * MLSTM_DK + 2 * A_WIDTH
    gate_hi = gate_lo + 4 * H
    for i in range(depth):
        j = i // 2
        ple = (p[i].reshape(n, -1), ple_w_gate[i].astype(BF16), ple_w_proj[i].astype(BF16))
        if i % 2 == 0:
            w_in = rec_w_in[j]
            qb_lo, ff_lo, ib_lo = gate_hi, gate_hi + B_WIDTH, gate_hi + 3 * B_WIDTH
            w_main = jnp.concatenate([w_in[:, :gate_lo], w_in[:, qb_lo:ff_lo], w_in[:, ib_lo:]], axis=1).astype(BF16)
            w_forget = w_in[:, ff_lo:ib_lo].astype(BF16)
            w_gate = jnp.zeros((d, V7X_LANES), BF16).at[:, :4 * H].set(w_in[:, gate_lo:gate_hi].astype(BF16))
            u = _matmul(xb, w_main, BF16, 1024, 1024).reshape(bsz, seq, -1)
            uf = _matmul(xb, w_forget, F32, 1024, 1024).reshape(bsz, seq, -1)
            graw = _matmul(xb, w_gate, F32, 1024, V7X_LANES)[:, :4 * H].reshape(bsz, seq, 4, H)
            L = MLSTM_CHUNK
            gc = graw.transpose(0, 3, 1, 2).reshape(bsz, H, seq // L, L, 4)
            gr = gc.transpose(0, 1, 2, 4, 3)
            bias = mlstm_gate_bias[j].astype(F32).T
            y_a = _mlstm(u, gc, gr, bias.reshape(H, 1, 4), bias.reshape(H, 4, 1), mlstm_norm_w[j].astype(F32))
            y_b = _hgrn(u, uf, hgrn_lb_logits.astype(F32), hgrn_norm_w[j].astype(F32), j)
            mix_in = (y_a.reshape(n, -1), y_b.reshape(n, -1))
            xf, xb = _matmul_ln(mix_in, rec_w_out[j].astype(BF16), xf, ln_w[i, 0], ln_b[i, 0], 512)
            hid = _swiglu_up(xb, ffn_w_gate[j], ffn_w_up[j], 1024, 512)
            xf, xb = _matmul_ln((hid,), ffn_w_down[j].astype(BF16), xf, ln_w[i, 1], ln_b[i, 1], 256, ple=ple)
        else:
            qkv = _matmul(xb, na_w_qkv[j], BF16, 1024, 1024).reshape(bsz, seq, -1)
            att = _na(qkv, _na_bias_table(na_rpb[j])).reshape(n, -1)
            xf, xb = _matmul_ln((att,), na_w_out[j].astype(BF16), xf, ln_w[i, 0], ln_b[i, 0], 512)
            xf, xb = _moe(xf, xb, moe_w_router[j], moe_b_router[j], moe_w_gate[j], moe_w_up[j], moe_w_down[j], ln_w[i, 1], ln_b[i, 1], ple)
    return xf.reshape(bsz, seq, d)
```

```python
import functools

import jax
import jax.numpy as jnp
import numpy as np
from jax import lax
from jax.experimental import pallas as pl
from jax.experimental.pallas import tpu as pltpu

F32 = jnp.float32
BF16 = jnp.bfloat16
HIGHEST = lax.Precision.HIGHEST

DEPTH = 2
ALPHA = (2 * DEPTH) ** 0.25
LN_EPS = 1e-5
RMS_EPS = 1e-6
GRID_W = 64

MLSTM_HEADS = 4
MLSTM_DK = 128
MLSTM_DV = 256
GATE_CAP = 15.0
HGRN_HEADS = 8
HGRN_DK = 128
A_WIDTH = MLSTM_HEADS * MLSTM_DV
B_WIDTH = HGRN_HEADS * HGRN_DK

NA_DH = 128
NA_HEADS = 16
NA_KH = 8
NA_KW = 16

N_EXPERTS = 8
TOP_K = 2

MLSTM_CHUNK = 256
HGRN_CHUNK = 128
HGRN_SUB = 32
MOE_TILE = 1536
MOE_BLOCKS = (512, 512, 256, 256)
MOE_TF = 512
LN_ROW_GROUP = 128
MM_TM = 1024
MM_TN = 1024
SWIGLU_TF = 512
LN_TM = 512
FFN_DOWN_TM = 256
ROUTER_TM = 512
COMBINE_TM = 256
NA_GROUP = 4
NA_UNION = NA_KH + NA_GROUP - 1

V7X_LANES = 128
MXU_COLS = 256
VMEM_LIMIT = 56 * 1024 * 1024
MOE_VMEM_LIMIT = 60 * 1024 * 1024
NEG_BIG = -1e30


def _params(sem):
    return pltpu.CompilerParams(dimension_semantics=sem, vmem_limit_bytes=VMEM_LIMIT)


def _dot(a, b, **kw):
    return jnp.dot(a, b, preferred_element_type=F32, **kw)


def _dot_nt(a, b):
    return lax.dot_general(a, b, (((1,), (1,)), ((), ())), preferred_element_type=F32)


def _dot_tn(a, b):
    return lax.dot_general(a, b, (((0,), (0,)), ((), ())), preferred_element_type=F32)


def _layer_norm(y, w, b):
    mu = jnp.mean(y, axis=-1, keepdims=True)
    yc = y - mu
    var = jnp.mean(yc * yc, axis=-1, keepdims=True)
    return yc * lax.rsqrt(var + LN_EPS) * w + b


def _log_sigmoid(z):
    return jnp.minimum(z, 0.0) - jnp.log(1.0 + jnp.exp(-jnp.abs(z)))


def _mm_body(a_ref, w_ref, o_ref, wb_ref):
    @pl.when(pl.program_id(1) == 0)
    def _():
        wb_ref[...] = w_ref[...].astype(BF16)

    o_ref[...] = _dot(a_ref[...], wb_ref[...]).astype(o_ref.dtype)


def _matmul(a, w, out_dtype, tm, tn):
    m, k = a.shape
    n = w.shape[1]
    return pl.pallas_call(
        _mm_body,
        grid=(n // tn, m // tm),
        in_specs=[pl.BlockSpec((tm, k), lambda j, i: (i, 0)), pl.BlockSpec((k, tn), lambda j, i: (0, j))],
        out_specs=pl.BlockSpec((tm, tn), lambda j, i: (i, j)),
        out_shape=jax.ShapeDtypeStruct((m, n), out_dtype),
        scratch_shapes=[pltpu.VMEM((k, tn), BF16)],
        compiler_params=_params(("arbitrary", "arbitrary")),
        name="matmul",
    )(a, w)


def _ple_epilogue(p_ref, wpg_ref, wpp_ref, xs_ref, o_ref, ob_ref):
    for c in range(o_ref.shape[1] // MXU_COLS):
        cols = pl.ds(c * MXU_COLS, MXU_COLS)
        gate = jax.nn.sigmoid(_dot(xs_ref[...], wpg_ref[:, cols]))
        proj = _dot(p_ref[...].astype(BF16), wpp_ref[:, cols])
        new = o_ref[:, cols] + gate * proj
        o_ref[:, cols] = new
        ob_ref[:, cols] = new.astype(BF16)


def _mm_ln_body(*refs, na, ple):
    a_refs = refs[:na]
    w_ref, res_ref, g_ref, b_ref = refs[na:na + 4]
    rest = refs[na + 4:]
    if ple:
        p_ref, wpg_ref, wpp_ref, o_ref, ob_ref, xs_ref = rest
    else:
        o_ref, ob_ref = rest
    part = LN_ROW_GROUP
    for r in range(res_ref.shape[0] // part):
        rows = pl.ds(r * part, part)
        y = ALPHA * res_ref[rows, :]
        k0 = 0
        for a_ref in a_refs:
            ka = a_ref.shape[1]
            y = y + _dot(a_ref[rows, :], w_ref[k0:k0 + ka, :])
            k0 += ka
        out = _layer_norm(y, g_ref[...], b_ref[...])
        o_ref[rows, :] = out
        (xs_ref if ple else ob_ref)[rows, :] = out.astype(BF16)
    if ple:
        _ple_epilogue(p_ref, wpg_ref, wpp_ref, xs_ref, o_ref, ob_ref)


def _matmul_ln(a_parts, w, res, ln_w, ln_b, tm, ple=None):
    m = res.shape[0]
    k, n = w.shape
    assert sum(a.shape[1] for a in a_parts) == k
    row_spec = pl.BlockSpec((tm, n), lambda i: (i, 0))
    vec_spec = pl.BlockSpec((1, n), lambda i: (0, 0))

    def whole(arr):
        return pl.BlockSpec(arr.shape, lambda i: (0, 0), pipeline_mode=pl.Buffered(1))

    in_specs = [pl.BlockSpec((tm, a.shape[1]), lambda i: (i, 0)) for a in a_parts] + [whole(w), row_spec, vec_spec, vec_spec]
    args = [*a_parts, w, res, ln_w.reshape(1, n), ln_b.reshape(1, n)]
    scratch = []
    if ple is not None:
        p, layer, wpg, wpp = ple
        in_specs += [pl.BlockSpec((None, tm, p.shape[2]), lambda i: (layer, i, 0)), whole(wpg), whole(wpp)]
        args += [p, wpg, wpp]
        scratch = [pltpu.VMEM((tm, n), BF16)]
    return pl.pallas_call(
        functools.partial(_mm_ln_body, na=len(a_parts), ple=ple is not None),
        grid=(m // tm,),
        in_specs=in_specs,
        out_specs=[row_spec, row_spec],
        out_shape=[jax.ShapeDtypeStruct((m, n), F32), jax.ShapeDtypeStruct((m, n), BF16)],
        scratch_shapes=scratch,
        compiler_params=_params(("parallel",)),
        name="matmul_ln",
    )(*args)


def _swiglu_up_body(a_ref, wg_ref, wu_ref, o_ref, wgb_ref, wub_ref):
    @pl.when(pl.program_id(1) == 0)
    def _():
        wgb_ref[...] = wg_ref[...].astype(BF16)
        wub_ref[...] = wu_ref[...].astype(BF16)

    for c in range(o_ref.shape[1] // MXU_COLS):
        cols = pl.ds(c * MXU_COLS, MXU_COLS)
        g = _dot(a_ref[...], wgb_ref[:, cols])
        u = _dot(a_ref[...], wub_ref[:, cols])
        o_ref[:, cols] = (g * jax.nn.sigmoid(g) * u).astype(o_ref.dtype)


def _swiglu_up(a, wg, wu, tm, tf):
    m, k = a.shape
    f = wg.shape[1]
    return pl.pallas_call(
        _swiglu_up_body,
        grid=(f // tf, m // tm),
        in_specs=[
            pl.BlockSpec((tm, k), lambda j, i: (i, 0)),
            pl.BlockSpec((k, tf), lambda j, i: (0, j)),
            pl.BlockSpec((k, tf), lambda j, i: (0, j)),
        ],
        out_specs=pl.BlockSpec((tm, tf), lambda j, i: (i, j)),
        out_shape=jax.ShapeDtypeStruct((m, f), BF16),
        scratch_shapes=[pltpu.VMEM((k, tf), BF16), pltpu.VMEM((k, tf), BF16)],
        compiler_params=_params(("arbitrary", "arbitrary")),
        name="swiglu_up",
    )(a, wg, wu)


def _mlstm_body(q_ref, k_ref, v_ref, og_ref, gc_ref, gr_ref, bc_ref, br_ref, nw_ref, y_ref, hs_ref, c_ref, *, seq):
    L = MLSTM_CHUNK
    nc = seq // L
    row = lax.broadcasted_iota(jnp.int32, (L, L), 0)
    col = lax.broadcasted_iota(jnp.int32, (L, L), 1)
    lower = row >= col
    upper = row <= col
    tril = lower.astype(F32)
    triu = upper.astype(F32)
    hs_ref[...] = jnp.zeros_like(hs_ref)
    c_ref[...] = jnp.zeros_like(c_ref)

    def cap(z):
        return GATE_CAP * jnp.tanh(z / GATE_CAP)

    def one_dir(cidx, d, n, m):
        rows = pl.ds(pl.multiple_of(cidx * L, L), L)
        gcol = cap(gc_ref[cidx] + bc_ref[...])
        grow = cap(gr_ref[cidx] + br_ref[...])
        lcol = _log_sigmoid(gcol)
        lrow = _log_sigmoid(grow)
        if d == 0:
            brow = _dot(lrow, triu, precision=HIGHEST)
            mask = lower
        else:
            brow = _dot(lrow, tril, precision=HIGHEST)
            mask = upper
        li_col = gcol[:, 2 * d:2 * d + 1]
        lf_wide = jnp.broadcast_to(lcol[:, 2 * d + 1:2 * d + 2], (L, V7X_LANES))
        b_col = _cumsum_rows(lf_wide, reverse=(d == 1))[:, 0:1]
        li_row = grow[2 * d:2 * d + 1, :]
        b_row = brow[2 * d + 1:2 * d + 2, :]
        g = b_col[L - 1:L, :] if d == 0 else b_col[0:1, :]

        q = q_ref[rows, :].astype(F32) * (MLSTM_DK ** -0.5)
        k = k_ref[rows, :].astype(F32)
        qb = q.astype(BF16)
        kb = k.astype(BF16)
        vb = v_ref[rows, :].astype(BF16)

        logd = jnp.where(mask, b_col + (li_row - b_row), -jnp.inf)
        m_t = jnp.maximum(jnp.max(logd, axis=1, keepdims=True), b_col + m)
        s = _dot_nt(qb, kb) * jnp.exp(logd - m_t)
        inter_w = jnp.exp(b_col + m - m_t)
        c_old = c_ref[d]
        num = _dot(s.astype(BF16), vb) + inter_w * _dot(qb, c_old.astype(BF16))
        den = jnp.sum(s, axis=1, keepdims=True) + inter_w * jnp.sum(q * n, axis=1, keepdims=True)
        h = num / jnp.maximum(jnp.abs(den), jnp.exp(-m_t))
        hs_ref[rows, :] += h

        a_col = g - b_col + li_col
        m_loc = jnp.max(a_col, axis=0, keepdims=True)
        kw = k * jnp.exp(a_col - m_loc)
        c_loc = _dot_tn(kw.astype(BF16), vb)
        n_loc = jnp.sum(kw, axis=0, keepdims=True)
        m_new = jnp.maximum(g + m, m_loc)
        decay = jnp.exp(g + m - m_new)
        inj = jnp.exp(m_loc - m_new)
        c_ref[d] = decay * c_old + inj * c_loc
        return decay * n + inj * n_loc, m_new

    def body(i, carry):
        nf, mf, nb, mb = carry
        nf, mf = one_dir(i, 0, nf, mf)
        nb, mb = one_dir(nc - 1 - i, 1, nb, mb)
        return nf, mf, nb, mb

    zn = jnp.zeros((1, MLSTM_DK), F32)
    zm = jnp.zeros((1, 1), F32)
    lax.fori_loop(0, nc, body, (zn, zm, zn, zm), unroll=4)

    def fin(i, carry):
        rows = pl.ds(pl.multiple_of(i * L, L), L)
        hh = hs_ref[rows, :]
        r = lax.rsqrt(jnp.mean(hh * hh, axis=-1, keepdims=True) + RMS_EPS)
        y_ref[rows, :] = (hh * r * nw_ref[...] * jax.nn.sigmoid(og_ref[rows, :].astype(F32))).astype(y_ref.dtype)
        return carry

    lax.fori_loop(0, nc, fin, 0)


def _mlstm(u, gc, gr, bias_c, bias_r, norm_w):
    bsz, seq, _ = u.shape
    L = MLSTM_CHUNK
    nc = seq // L
    H, dk, dv = MLSTM_HEADS, MLSTM_DK, MLSTM_DV
    k_off = H * dk // dk
    v_off = 2 * H * dk // dv
    o_off = v_off + H
    return pl.pallas_call(
        functools.partial(_mlstm_body, seq=seq),
        grid=(bsz, H),
        in_specs=[
            pl.BlockSpec((None, seq, dk), lambda b, h: (b, 0, h)),
            pl.BlockSpec((None, seq, dk), lambda b, h: (b, 0, k_off + h)),
            pl.BlockSpec((None, seq, dv), lambda b, h: (b, 0, v_off + h)),
            pl.BlockSpec((None, seq, dv), lambda b, h: (b, 0, o_off + h)),
            pl.BlockSpec((None, None, nc, L, 4), lambda b, h: (b, h, 0, 0, 0)),
            pl.BlockSpec((None, None, nc, 4, L), lambda b, h: (b, h, 0, 0, 0)),
            pl.BlockSpec((None, 1, 4), lambda b, h: (h, 0, 0)),
            pl.BlockSpec((None, 4, 1), lambda b, h: (h, 0, 0)),
            pl.BlockSpec((1, dv), lambda b, h: (0, h)),
        ],
        out_specs=pl.BlockSpec((None, seq, dv), lambda b, h: (b, 0, h)),
        out_shape=jax.ShapeDtypeStruct((bsz, seq, H * dv), BF16),
        scratch_shapes=[pltpu.VMEM((seq, dv), F32), pltpu.VMEM((2, dk, dv), F32)],
        compiler_params=_params(("parallel", "arbitrary")),
        name="mlstm",
    )(u, u, u, u, gc, gr, bias_c, bias_r, norm_w.reshape(1, H * dv))


def _cumsum_rows(x, reverse):
    n = x.shape[0]
    ridx = lax.broadcasted_iota(jnp.int32, x.shape, 0)
    s = 1
    while s < n:
        if reverse:
            x = x + jnp.where(ridx < n - s, pltpu.roll(x, n - s, axis=0), 0.0)
        else:
            x = x + jnp.where(ridx >= s, pltpu.roll(x, s, axis=0), 0.0)
        s *= 2
    return x


def _hgrn_body(q_ref, ff_ref, fb_ref, i_ref, g_ref, lbl_ref, nw_ref, y_ref, os_ref, st_ref, *, seq, layer):
    L = HGRN_CHUNK
    SB = HGRN_SUB
    nc = seq // L
    os_ref[...] = jnp.zeros_like(os_ref)
    st_ref[...] = jnp.zeros_like(st_ref)

    def lower_bound(d):
        lg = lbl_ref[d]
        e = jnp.exp(lg - jnp.max(lg, axis=0, keepdims=True))
        sm = e / jnp.sum(e, axis=0, keepdims=True)
        return jnp.sum(sm[:layer + 1, :], axis=0, keepdims=True)

    lbs = (lower_bound(0), lower_bound(1))

    def one_dir(cidx, d):
        rows = pl.ds(pl.multiple_of(cidx * L, L), L)
        qr = q_ref[rows, :].astype(F32)
        q = qr * jax.nn.sigmoid(qr)
        vb = i_ref[rows, :].astype(BF16)
        fr = (ff_ref if d == 0 else fb_ref)[rows, :]
        lb = lbs[d]
        f = lb + (1.0 - lb) * jax.nn.sigmoid(fr)
        k = 1.0 - f
        lf = jnp.log(f)
        b = _cumsum_rows(lf, reverse=(d == 1))
        g = b[L - 1:L, :] if d == 0 else b[0:1, :]
        qi = (q * jnp.exp(b)).astype(BF16)
        ke = (k * jnp.exp(g - b)).astype(BF16)
        parts = []
        for jb in range(L // SB):
            lo, hi = jb * SB, (jb + 1) * SB
            bm = b[lo + SB // 2:lo + SB // 2 + 1, :]
            qm = (q[lo:hi, :] * jnp.exp(b[lo:hi, :] - bm)).astype(BF16)
            ks = slice(0, hi) if d == 0 else slice(lo, L)
            km = (k[ks, :] * jnp.exp(bm - b[ks, :])).astype(BF16)
            if km.shape[0] < L:
                pad = jnp.zeros((L - km.shape[0], HGRN_DK), BF16)
                km = jnp.concatenate([km, pad] if d == 0 else [pad, km], axis=0)
            tq = lo + lax.broadcasted_iota(jnp.int32, (SB, L), 0)
            ts = lax.broadcasted_iota(jnp.int32, (SB, L), 1)
            keep = (ts <= tq) if d == 0 else (ts >= tq)
            parts.append(jnp.where(keep, _dot_nt(qm, km), 0.0))
        a = jnp.concatenate(parts, axis=0)
        st = st_ref[d]
        o = _dot(a.astype(BF16), vb) + _dot_nt(qi, st.astype(BF16))
        st_ref[d] = st * jnp.exp(g) + _dot_tn(vb, ke)
        os_ref[rows, :] += o

    def body(i, carry):
        one_dir(i, 0)
        one_dir(nc - 1 - i, 1)
        return carry

    lax.fori_loop(0, nc, body, 0, unroll=4)

    FL = 256

    def fin(i, carry):
        rows = pl.ds(pl.multiple_of(i * FL, FL), FL)
        hh = os_ref[rows, :]
        r = lax.rsqrt(jnp.mean(hh * hh, axis=-1, keepdims=True) + RMS_EPS)
        gg = g_ref[rows, :].astype(F32)
        y_ref[rows, :] = (hh * r * nw_ref[...] * (gg * jax.nn.sigmoid(gg))).astype(y_ref.dtype)
        return carry

    lax.fori_loop(0, seq // FL, fin, 0)


def _hgrn(u, uf, lb_logits, norm_w, layer):
    bsz, seq, _ = u.shape
    H, dk = HGRN_HEADS, HGRN_DK
    base = (2 * MLSTM_HEADS * MLSTM_DK + 2 * A_WIDTH) // dk
    slots = lb_logits.shape[1]
    return pl.pallas_call(
        functools.partial(_hgrn_body, seq=seq, layer=layer),
        grid=(bsz, H),
        in_specs=[
            pl.BlockSpec((None, seq, dk), lambda b, h: (b, 0, base + h)),
            pl.BlockSpec((None, seq, dk), lambda b, h: (b, 0, h)),
            pl.BlockSpec((None, seq, dk), lambda b, h: (b, 0, H + h)),
            pl.BlockSpec((None, seq, dk), lambda b, h: (b, 0, base + H + h)),
            pl.BlockSpec((None, seq, dk), lambda b, h: (b, 0, base + 2 * H + h)),
            pl.BlockSpec((2, slots, dk), lambda b, h: (0, 0, h)),
            pl.BlockSpec((1, dk), lambda b, h: (0, h)),
        ],
        out_specs=pl.BlockSpec((None, seq, dk), lambda b, h: (b, 0, h)),
        out_shape=jax.ShapeDtypeStruct((bsz, seq, H * dk), BF16),
        scratch_shapes=[pltpu.VMEM((seq, dk), F32), pltpu.VMEM((2, dk, dk), F32)],
        compiler_params=_params(("parallel", "arbitrary")),
        name="hgrn2",
    )(u, uf, uf, u, u, lb_logits, norm_w.reshape(1, H * dk))


def _na_body(q_ref, k_ref, v_ref, tb_ref, o_ref, bias_ref, *, rows):
    W = GRID_W
    G, U, kh = NA_GROUP, NA_UNION, NA_KH
    ng = rows // G
    scale = NA_DH ** -0.5

    @pl.when(pl.program_id(1) == 0)
    def _():
        for c, (delta, offs) in enumerate(_na_group_classes(rows)):
            for i in range(G):
                qr = pl.ds(i * W, W)
                for k0 in range(0, U, 2):
                    v0 = offs[i] <= k0 < offs[i] + kh
                    v1 = k0 + 1 < U and offs[i] <= k0 + 1 < offs[i] + kh
                    dr = delta + k0 - i + kh - 1
                    if k0 + 1 >= U:
                        blk = tb_ref[1, dr][:, :W] if v0 else jnp.full((W, W), NEG_BIG, F32)
                        bias_ref[c, qr, pl.ds(k0 * W, W)] = blk
                        continue
                    if v0 and v1:
                        blk = tb_ref[0, dr]
                    elif v0:
                        blk = tb_ref[1, dr]
                    elif v1:
                        blk = tb_ref[2, dr + 1]
                    else:
                        blk = jnp.full((W, 2 * W), NEG_BIG, F32)
                    bias_ref[c, qr, pl.ds(k0 * W, 2 * W)] = blk

    def body(gi, carry):
        r0 = gi * G
        us = jnp.clip(r0 - NA_KH // 2, 0, rows - U)
        cls = jnp.where(gi == 0, 0, jnp.where(gi == ng - 1, 2, 1))
        qrows = pl.ds(pl.multiple_of(r0 * W, G * W), G * W)
        kwin = pl.ds(pl.multiple_of(us * W, W), U * W)
        s = _dot_nt(q_ref[qrows, :], k_ref[kwin, :]) * scale + bias_ref[cls]
        m = jnp.max(s, axis=-1, keepdims=True)
        p = jnp.exp(s - m)
        den = jnp.sum(p, axis=-1, keepdims=True)
        o = _dot(p.astype(BF16), v_ref[kwin, :]) / den
        o_ref[qrows, :] = o.astype(o_ref.dtype)
        return carry

    lax.fori_loop(0, ng, body, 0, unroll=8)


def _na_group_classes(rows):
    G, U, kh = NA_GROUP, NA_UNION, NA_KH

    def info(r0):
        us = min(max(r0 - kh // 2, 0), rows - U)
        return us - r0, tuple(min(max(r0 + i - kh // 2, 0), rows - kh) - us for i in range(G))

    ng = rows // G
    infos = [info(G * g) for g in range(ng)]
    classes = [infos[0], infos[1], infos[-1]]
    assert rows % G == 0 and ng >= 3 and all(infos[g] == classes[1] for g in range(1, ng - 1))
    assert all(0 <= o and o + kh <= U for c in classes for o in c[1])
    return classes


def _na_bias_table(rpb):
    W = GRID_W
    ndr, ndc = rpb.shape[1], rpb.shape[2]
    col = np.arange(W)
    col_start = np.clip(col - NA_KW // 2, 0, W - NA_KW)
    inside = (col[None, :] >= col_start[:, None]) & (col[None, :] < col_start[:, None] + NA_KW)
    dc = np.clip(col[None, :] - col[:, None] + NA_KW - 1, 0, ndc - 1)
    sel = np.zeros((ndc, W, W), np.float32)
    qq, kk = np.nonzero(inside)
    sel[dc[qq, kk], qq, kk] = 1.0
    off = np.zeros_like(sel)
    sel_left = jnp.asarray(np.concatenate([sel, off], axis=2))
    sel_right = jnp.asarray(np.concatenate([off, sel], axis=2))
    mask = np.where(inside, 0.0, NEG_BIG).astype(np.float32)
    allneg = np.full_like(mask, NEG_BIG)
    neg_pair = np.tile(np.concatenate([mask, mask], axis=1), (ndr, 1, 1))
    neg_pair[ndr - 1, :, W:] = NEG_BIG
    neg_lo = np.concatenate([mask, allneg], axis=1)
    neg_hi = np.concatenate([allneg, mask], axis=1)
    r = rpb.astype(F32)
    r_next = jnp.concatenate([r[:, 1:], jnp.zeros_like(r[:, :1])], axis=1)

    def expand(rows, sel_half):
        return jnp.einsum('hdm,mqk->hdqk', rows, sel_half, precision=HIGHEST)

    pair = expand(r, sel_left) + expand(r_next, sel_right) + neg_pair
    lo = expand(r, sel_left) + neg_lo
    hi = expand(r, sel_right) + neg_hi
    return jnp.stack([pair, lo, hi], axis=1)


def _na(qkv, bias_tbl):
    bsz, seq, _ = qkv.shape
    rows = seq // GRID_W
    H, dh = NA_HEADS, NA_DH
    return pl.pallas_call(
        functools.partial(_na_body, rows=rows),
        grid=(H, bsz),
        in_specs=[
            pl.BlockSpec((None, seq, dh), lambda h, b: (b, 0, h)),
            pl.BlockSpec((None, seq, dh), lambda h, b: (b, 0, H + h)),
            pl.BlockSpec((None, seq, dh), lambda h, b: (b, 0, 2 * H + h)),
            pl.BlockSpec((None,) + bias_tbl.shape[1:], lambda h, b: (h, 0, 0, 0, 0)),
        ],
        out_specs=pl.BlockSpec((None, seq, dh), lambda h, b: (b, 0, h)),
        out_shape=jax.ShapeDtypeStruct((bsz, seq, H * dh), BF16),
        scratch_shapes=[pltpu.VMEM((3, NA_GROUP * GRID_W, NA_UNION * GRID_W), F32)],
        compiler_params=_params(("arbitrary", "arbitrary")),
        name="natten",
    )(qkv, qkv, qkv, bias_tbl)


def _router_body(x_ref, wh_ref, wl_ref, b_ref, e_ref, g_ref):
    x = x_ref[...]
    xh = x.astype(BF16)
    xl = (x - xh.astype(F32)).astype(BF16)
    logits = _dot(xh, wh_ref[...]) + _dot(xl, wh_ref[...]) + _dot(xh, wl_ref[...]) + b_ref[...]
    lane = lax.broadcasted_iota(jnp.int32, logits.shape, 1)
    nl = logits.shape[1]
    m1 = jnp.max(logits, axis=-1, keepdims=True)
    i1 = jnp.min(jnp.where(logits == m1, lane, nl), axis=-1, keepdims=True)
    rest = jnp.where(lane == i1, -jnp.inf, logits)
    m2 = jnp.max(rest, axis=-1, keepdims=True)
    i2 = jnp.min(jnp.where(rest == m2, lane, nl), axis=-1, keepdims=True)
    ex = jnp.exp(m2 - m1)
    g1 = 1.0 / (1.0 + ex)
    g2 = ex / (1.0 + ex)
    e_ref[...] = jnp.where(lane == 0, i1, jnp.where(lane == 1, i2, 0))
    g_ref[...] = jnp.where(lane == 0, g1, jnp.where(lane == 1, g2, 0.0))


def _router(x, w_router, b_router, tm):
    n, d = x.shape
    ne = w_router.shape[1]
    w = jnp.zeros((d, V7X_LANES), F32).at[:, :ne].set(w_router.astype(F32))
    b = jnp.full((1, V7X_LANES), NEG_BIG, F32).at[0, :ne].set(b_router.astype(F32))
    wh = w.astype(BF16)
    wl = (w - wh.astype(F32)).astype(BF16)
    return pl.pallas_call(
        _router_body,
        grid=(n // tm,),
        in_specs=[
            pl.BlockSpec((tm, d), lambda i: (i, 0)),
            pl.BlockSpec((d, V7X_LANES), lambda i: (0, 0)),
            pl.BlockSpec((d, V7X_LANES), lambda i: (0, 0)),
            pl.BlockSpec((1, V7X_LANES), lambda i: (0, 0)),
        ],
        out_specs=[pl.BlockSpec((tm, V7X_LANES), lambda i: (i, 0)), pl.BlockSpec((tm, V7X_LANES), lambda i: (i, 0))],
        out_shape=[jax.ShapeDtypeStruct((n, V7X_LANES), jnp.int32), jax.ShapeDtypeStruct((n, V7X_LANES), F32)],
        compiler_params=_params(("parallel",)),
        name="router_top2",
    )(x, wh, wl, b)


def _pack_bf16_pairs(xb):
    half = xb.shape[1] // 2
    lo = lax.bitcast_convert_type(xb[:, :half], jnp.uint16).astype(jnp.uint32)
    hi = lax.bitcast_convert_type(xb[:, half:], jnp.uint16).astype(jnp.uint32)
    return lo | (hi << 16)


def _moe_body(te_ref, nv_ref, xp_ref, wg_ref, wu_ref, wd_ref, o_ref, xb_ref, wgb_ref, wub_ref, wdb_ref, *, blocks):
    t = pl.program_id(0)
    f = pl.program_id(1)
    nv = nv_ref[t]
    tm, half = xp_ref.shape
    tf = wg_ref.shape[1]
    d = o_ref.shape[1]
    nc = MXU_COLS

    @pl.when(f == 0)
    def _():
        o_ref[...] = jnp.zeros_like(o_ref)
        for start, size in blocks:
            rows = pl.ds(start, size)
            w = xp_ref[rows, :]
            xb_ref[rows, :half] = lax.bitcast_convert_type(w << 16, F32).astype(BF16)
            xb_ref[rows, half:] = lax.bitcast_convert_type(w & jnp.uint32(0xFFFF0000), F32).astype(BF16)

    for sb, (start, size) in enumerate(blocks):
        rows = pl.ds(start, size)

        @pl.when(start < nv)
        def _():
            hs = []
            for c in range(tf // nc):
                cols = pl.ds(c * nc, nc)
                if sb == 0:
                    wg = wg_ref[:, cols].astype(BF16)
                    wu = wu_ref[:, cols].astype(BF16)
                    wgb_ref[:, cols] = wg
                    wub_ref[:, cols] = wu
                else:
                    wg = wgb_ref[:, cols]
                    wu = wub_ref[:, cols]
                g = _dot(xb_ref[rows, :], wg)
                u = _dot(xb_ref[rows, :], wu)
                hs.append((g * jax.nn.sigmoid(g) * u).astype(BF16))
            h = jnp.concatenate(hs, axis=1)
            for c in range(d // nc):
                cols = pl.ds(c * nc, nc)
                if sb == 0:
                    wd = wd_ref[:, cols].astype(BF16)
                    wdb_ref[:, cols] = wd
                else:
                    wd = wdb_ref[:, cols]
                o_ref[rows, cols] += _dot(h, wd)


def _moe_experts(xp, tile_expert, tile_valid, wg, wu, wd, tm, tf, block_sizes):
    rows, half = xp.shape
    d = 2 * half
    n_tiles = rows // tm
    fdim = wg.shape[2]
    nf = fdim // tf
    assert sum(block_sizes) == tm
    blocks = tuple((sum(block_sizes[:i]), s) for i, s in enumerate(block_sizes))

    def fidx(t, f, nv):
        return jnp.where(nv[t] > 0, f, nf - 1)

    return pl.pallas_call(
        functools.partial(_moe_body, blocks=blocks),
        grid_spec=pltpu.PrefetchScalarGridSpec(
            num_scalar_prefetch=2,
            grid=(n_tiles, nf),
            in_specs=[
                pl.BlockSpec((tm, half), lambda t, f, te, nv: (t, 0), pipeline_mode=pl.Buffered(1)),
                pl.BlockSpec((None, d, tf), lambda t, f, te, nv: (te[t], 0, fidx(t, f, nv))),
                pl.BlockSpec((None, d, tf), lambda t, f, te, nv: (te[t], 0, fidx(t, f, nv))),
                pl.BlockSpec((None, tf, d), lambda t, f, te, nv: (te[t], fidx(t, f, nv), 0)),
            ],
            out_specs=pl.BlockSpec((tm, d), lambda t, f, te, nv: (t, 0), pipeline_mode=pl.Buffered(1)),
            scratch_shapes=[pltpu.VMEM((tm, d), BF16), pltpu.VMEM((d, tf), BF16), pltpu.VMEM((d, tf), BF16), pltpu.VMEM((tf, d), BF16)],
        ),
        out_shape=jax.ShapeDtypeStruct((rows, d), F32),
        compiler_params=pltpu.CompilerParams(dimension_semantics=("arbitrary", "arbitrary"), vmem_limit_bytes=MOE_VMEM_LIMIT),
        name="moe_experts",
    )(tile_expert, tile_valid, xp, wg, wu, wd)


def _combine_ln_body(x_ref, y0_ref, y1_ref, g_ref, w_ref, b_ref, p_ref, wpg_ref, wpp_ref, o_ref, ob_ref, xs_ref):
    g = g_ref[...]
    mix = y0_ref[...] * g[:, 0:1] + y1_ref[...] * g[:, 1:2]
    out = _layer_norm(ALPHA * x_ref[...] + mix, w_ref[...], b_ref[...])
    o_ref[...] = out
    xs_ref[...] = out.astype(BF16)
    _ple_epilogue(p_ref, wpg_ref, wpp_ref, xs_ref, o_ref, ob_ref)


def _combine_ln(x, y0, y1, gates, ln_w, ln_b, ple, tm):
    n, d = x.shape
    p, layer, wpg, wpp = ple
    row_spec = pl.BlockSpec((tm, d), lambda i: (i, 0))
    vec_spec = pl.BlockSpec((1, d), lambda i: (0, 0))

    def whole(arr):
        return pl.BlockSpec(arr.shape, lambda i: (0, 0), pipeline_mode=pl.Buffered(1))

    return pl.pallas_call(
        _combine_ln_body,
        grid=(n // tm,),
        in_specs=[row_spec, row_spec, row_spec, pl.BlockSpec((tm, V7X_LANES), lambda i: (i, 0)), vec_spec, vec_spec,
                  pl.BlockSpec((None, tm, p.shape[2]), lambda i: (layer, i, 0)), whole(wpg), whole(wpp)],
        out_specs=[row_spec, row_spec],
        out_shape=[jax.ShapeDtypeStruct((n, d), F32), jax.ShapeDtypeStruct((n, d), BF16)],
        scratch_shapes=[pltpu.VMEM((tm, d), BF16)],
        compiler_params=_params(("parallel",)),
        name="combine_ln",
    )(x, y0, y1, gates, ln_w.reshape(1, d), ln_b.reshape(1, d), p, wpg, wpp)


def _moe(x, xb, w_router, b_router, wg, wu, wd, ln_w, ln_b, ple):
    n, d = x.shape
    tm = MOE_TILE
    nk = n * TOP_K
    e_out, g_out = _router(x, w_router, b_router, ROUTER_TM)
    e_flat = e_out[:, :TOP_K].reshape(-1)
    onehot = (e_flat[:, None] == jnp.arange(N_EXPERTS, dtype=jnp.int32)[None, :]).astype(jnp.int32)
    csum = jnp.cumsum(onehot, axis=0)
    counts = csum[-1]
    pos = jnp.take_along_axis(csum, e_flat[:, None], axis=1)[:, 0] - 1
    padded = (counts + tm - 1) // tm * tm
    pad_end = jnp.cumsum(padded)
    pad_start = pad_end - padded
    dest = pad_start[e_flat] + pos
    n_tiles = -(-nk // tm) + N_EXPERTS
    row_tok = (jnp.arange(n_tiles * tm, dtype=jnp.int32) % n).at[dest].set(jnp.arange(nk, dtype=jnp.int32) // TOP_K, unique_indices=True)
    n_active = (pad_end[-1] // tm).astype(jnp.int32)
    tile_all = jnp.arange(n_tiles, dtype=jnp.int32)
    tile_ids = jnp.minimum(tile_all, n_active - 1)
    tile_expert = jnp.sum((pad_end[None, :] <= (tile_ids * tm)[:, None]).astype(jnp.int32), axis=1)
    tile_expert = jnp.minimum(tile_expert, N_EXPERTS - 1)
    tile_valid = jnp.clip((pad_start + counts)[tile_expert] - tile_ids * tm, 0, tm)
    tile_valid = jnp.where(tile_all < n_active, tile_valid, 0).astype(jnp.int32)
    xs = _pack_bf16_pairs(xb)[row_tok]
    ys = _moe_experts(xs, tile_expert, tile_valid, wg, wu, wd, tm, MOE_TF, MOE_BLOCKS)
    dest2 = dest.reshape(n, TOP_K)
    return _combine_ln(x, ys[dest2[:, 0]], ys[dest2[:, 1]], g_out, ln_w, ln_b, ple, COMBINE_TM)


def kernel(x, p, ln_w, ln_b, rec_w_in, mlstm_gate_bias, mlstm_norm_w, hgrn_lb_logits, hgrn_norm_w, rec_w_out, ffn_w_gate, ffn_w_up, ffn_w_down, na_w_qkv, na_rpb, na_w_out, moe_w_router, moe_b_router, moe_w_gate, moe_w_up, moe_w_down, ple_w_gate, ple_w_proj):
    bsz, seq, d = x.shape
    n = bsz * seq
    depth = ln_w.shape[0]
    xf = x.reshape(n, d).astype(F32)
    xb = xf.astype(BF16)
    H = MLSTM_HEADS
    gate_lo = 2 * H * MLSTM_DK + 2 * A_WIDTH
    gate_hi = gate_lo + 4 * H
    for i in range(depth):
        j = i // 2
        ple = (p.reshape(depth, n, -1), i, ple_w_gate[i].astype(BF16), ple_w_proj[i].astype(BF16))
        if i % 2 == 0:
            w_in = rec_w_in[j]
            qb_lo, ff_lo, ib_lo = gate_hi, gate_hi + B_WIDTH, gate_hi + 3 * B_WIDTH
            w_main = jnp.concatenate([w_in[:, :gate_lo], w_in[:, qb_lo:ff_lo], w_in[:, ib_lo:]], axis=1).astype(BF16)
            w_forget = w_in[:, ff_lo:ib_lo].astype(BF16)
            w_gate = jnp.zeros((d, V7X_LANES), BF16).at[:, :4 * H].set(w_in[:, gate_lo:gate_hi].astype(BF16))
            u = _matmul(xb, w_main, BF16, MM_TM, MM_TN).reshape(bsz, seq, -1)
            uf = _matmul(xb, w_forget, F32, MM_TM, MM_TN).reshape(bsz, seq, -1)
            graw = _matmul(xb, w_gate, F32, MM_TM, V7X_LANES)[:, :4 * H].reshape(bsz, seq, 4, H)
            L = MLSTM_CHUNK
            gc = graw.transpose(0, 3, 1, 2).reshape(bsz, H, seq // L, L, 4)
            gr = gc.transpose(0, 1, 2, 4, 3)
            bias = mlstm_gate_bias[j].astype(F32).T
            y_a = _mlstm(u, gc, gr, bias.reshape(H, 1, 4), bias.reshape(H, 4, 1), mlstm_norm_w[j].astype(F32))
            y_b = _hgrn(u, uf, hgrn_lb_logits.astype(F32), hgrn_norm_w[j].astype(F32), j)
            mix_in = (y_a.reshape(n, -1), y_b.reshape(n, -1))
            xf, xb = _matmul_ln(mix_in, rec_w_out[j].astype(BF16), xf, ln_w[i, 0], ln_b[i, 0], LN_TM)
            hid = _swiglu_up(xb, ffn_w_gate[j], ffn_w_up[j], MM_TM, SWIGLU_TF)
            xf, xb = _matmul_ln((hid,), ffn_w_down[j].astype(BF16), xf, ln_w[i, 1], ln_b[i, 1], FFN_DOWN_TM, ple=ple)
        else:
            qkv = _matmul(xb, na_w_qkv[j], BF16, MM_TM, MM_TN).reshape(bsz, seq, -1)
            att = _na(qkv, _na_bias_table(na_rpb[j])).reshape(n, -1)
            xf, xb = _matmul_ln((att,), na_w_out[j].astype(BF16), xf, ln_w[i, 0], ln_b[i, 0], LN_TM)
            xf, xb = _moe(xf, xb, moe_w_router[j], moe_b_router[j], moe_w_gate[j], moe_w_up[j], moe_w_down[j], ln_w[i, 1], ln_b[i, 1], ple)
    return xf.reshape(bsz, seq, d)
```

```python
import functools

import jax
import jax.numpy as jnp
import numpy as np
from jax import lax
from jax.experimental import pallas as pl
from jax.experimental.pallas import tpu as pltpu

F32 = jnp.float32
BF16 = jnp.bfloat16
HIGHEST = lax.Precision.HIGHEST

DEPTH = 2
ALPHA = (2 * DEPTH) ** 0.25
LN_EPS = 1e-5
RMS_EPS = 1e-6
GRID_W = 64

MLSTM_HEADS = 4
MLSTM_DK = 128
MLSTM_DV = 256
GATE_CAP = 15.0
HGRN_HEADS = 8
HGRN_DK = 128
A_WIDTH = MLSTM_HEADS * MLSTM_DV
B_WIDTH = HGRN_HEADS * HGRN_DK

NA_DH = 128
NA_HEADS = 16
NA_KH = 8
NA_KW = 16

N_EXPERTS = 8
TOP_K = 2

MLSTM_CHUNK = 256
HGRN_CHUNK = 128
HGRN_SUB = 32
MOE_TILE = 1024
MOE_BLOCKS = (256, 256, 512)
MOE_TF = 512
LN_ROW_GROUP = 128
MM_TM = 1024
MM_TN = 1024
SWIGLU_TF = 512
LN_TM = 512
FFN_DOWN_TM = 256
ROUTER_TM = 512
COMBINE_TM = 256
NA_GROUP = 4
NA_UNION = NA_KH + NA_GROUP - 1

V7X_LANES = 128
MXU_COLS = 256
VMEM_LIMIT = 56 * 1024 * 1024
MOE_VMEM_LIMIT = 60 * 1024 * 1024
NEG_BIG = -1e30


def _params(sem):
    return pltpu.CompilerParams(dimension_semantics=sem, vmem_limit_bytes=VMEM_LIMIT)


def _dot(a, b, **kw):
    return jnp.dot(a, b, preferred_element_type=F32, **kw)


def _dot_nt(a, b):
    return lax.dot_general(a, b, (((1,), (1,)), ((), ())), preferred_element_type=F32)


def _dot_tn(a, b):
    return lax.dot_general(a, b, (((0,), (0,)), ((), ())), preferred_element_type=F32)


def _layer_norm(y, w, b):
    mu = jnp.mean(y, axis=-1, keepdims=True)
    yc = y - mu
    var = jnp.mean(yc * yc, axis=-1, keepdims=True)
    return yc * lax.rsqrt(var + LN_EPS) * w + b


def _log_sigmoid(z):
    return jnp.minimum(z, 0.0) - jnp.log(1.0 + jnp.exp(-jnp.abs(z)))


def _mm_body(a_ref, w_ref, o_ref, wb_ref):
    @pl.when(pl.program_id(1) == 0)
    def _():
        wb_ref[...] = w_ref[...].astype(BF16)

    o_ref[...] = _dot(a_ref[...], wb_ref[...]).astype(o_ref.dtype)


def _matmul(a, w, out_dtype, tm, tn):
    m, k = a.shape
    n = w.shape[1]
    return pl.pallas_call(
        _mm_body,
        grid=(n // tn, m // tm),
        in_specs=[pl.BlockSpec((tm, k), lambda j, i: (i, 0)), pl.BlockSpec((k, tn), lambda j, i: (0, j))],
        out_specs=pl.BlockSpec((tm, tn), lambda j, i: (i, j)),
        out_shape=jax.ShapeDtypeStruct((m, n), out_dtype),
        scratch_shapes=[pltpu.VMEM((k, tn), BF16)],
        compiler_params=_params(("arbitrary", "arbitrary")),
        name="matmul",
    )(a, w)


def _cast_gates_body(x_ref, w_ref, xb_ref, g_ref):
    xb = x_ref[...].astype(BF16)
    xb_ref[...] = xb
    g_ref[...] = _dot(xb, w_ref[...])


def _cast_and_gates(x, w_gate, tm):
    m, d = x.shape
    n = w_gate.shape[1]
    return pl.pallas_call(
        _cast_gates_body,
        grid=(m // tm,),
        in_specs=[pl.BlockSpec((tm, d), lambda i: (i, 0)), pl.BlockSpec((d, n), lambda i: (0, 0))],
        out_specs=[pl.BlockSpec((tm, d), lambda i: (i, 0)), pl.BlockSpec((tm, n), lambda i: (i, 0))],
        out_shape=[jax.ShapeDtypeStruct((m, d), BF16), jax.ShapeDtypeStruct((m, n), F32)],
        compiler_params=_params(("parallel",)),
        name="cast_gates",
    )(x, w_gate)


def _ple_epilogue(p_ref, wpg_ref, wpp_ref, xs_ref, o_ref, ob_ref):
    for c in range(o_ref.shape[1] // MXU_COLS):
        cols = pl.ds(c * MXU_COLS, MXU_COLS)
        gate = jax.nn.sigmoid(_dot(xs_ref[...], wpg_ref[:, cols]))
        proj = _dot(p_ref[...].astype(BF16), wpp_ref[:, cols])
        new = o_ref[:, cols] + gate * proj
        o_ref[:, cols] = new
        ob_ref[:, cols] = new.astype(BF16)


def _mm_ln_body(*refs, na, ple):
    a_refs = refs[:na]
    w_ref, res_ref, g_ref, b_ref = refs[na:na + 4]
    rest = refs[na + 4:]
    if ple:
        p_ref, wpg_ref, wpp_ref, o_ref, ob_ref, xs_ref = rest
    else:
        o_ref, ob_ref = rest
    part = LN_ROW_GROUP
    for r in range(res_ref.shape[0] // part):
        rows = pl.ds(r * part, part)
        y = ALPHA * res_ref[rows, :]
        k0 = 0
        for a_ref in a_refs:
            ka = a_ref.shape[1]
            y = y + _dot(a_ref[rows, :], w_ref[k0:k0 + ka, :])
            k0 += ka
        out = _layer_norm(y, g_ref[...], b_ref[...])
        o_ref[rows, :] = out
        (xs_ref if ple else ob_ref)[rows, :] = out.astype(BF16)
    if ple:
        _ple_epilogue(p_ref, wpg_ref, wpp_ref, xs_ref, o_ref, ob_ref)


def _matmul_ln(a_parts, w, res, ln_w, ln_b, tm, ple=None):
    m = res.shape[0]
    k, n = w.shape
    assert sum(a.shape[1] for a in a_parts) == k
    row_spec = pl.BlockSpec((tm, n), lambda i: (i, 0))
    vec_spec = pl.BlockSpec((1, n), lambda i: (0, 0))

    def whole(arr):
        return pl.BlockSpec(arr.shape, lambda i: (0, 0), pipeline_mode=pl.Buffered(1))

    in_specs = [pl.BlockSpec((tm, a.shape[1]), lambda i: (i, 0)) for a in a_parts] + [whole(w), row_spec, vec_spec, vec_spec]
    args = [*a_parts, w, res, ln_w.reshape(1, n), ln_b.reshape(1, n)]
    scratch = []
    if ple is not None:
        p, layer, wpg, wpp = ple
        in_specs += [pl.BlockSpec((None, tm, p.shape[2]), lambda i: (layer, i, 0)), whole(wpg), whole(wpp)]
        args += [p, wpg, wpp]
        scratch = [pltpu.VMEM((tm, n), BF16)]
    return pl.pallas_call(
        functools.partial(_mm_ln_body, na=len(a_parts), ple=ple is not None),
        grid=(m // tm,),
        in_specs=in_specs,
        out_specs=[row_spec, row_spec],
        out_shape=[jax.ShapeDtypeStruct((m, n), F32), jax.ShapeDtypeStruct((m, n), BF16)],
        scratch_shapes=scratch,
        compiler_params=_params(("parallel",)),
        name="matmul_ln",
    )(*args)


def _swiglu_up_body(a_ref, wg_ref, wu_ref, o_ref, wgb_ref, wub_ref):
    @pl.when(pl.program_id(1) == 0)
    def _():
        wgb_ref[...] = wg_ref[...].astype(BF16)
        wub_ref[...] = wu_ref[...].astype(BF16)

    for c in range(o_ref.shape[1] // MXU_COLS):
        cols = pl.ds(c * MXU_COLS, MXU_COLS)
        g = _dot(a_ref[...], wgb_ref[:, cols])
        u = _dot(a_ref[...], wub_ref[:, cols])
        o_ref[:, cols] = (g * jax.nn.sigmoid(g) * u).astype(o_ref.dtype)


def _swiglu_up(a, wg, wu, tm, tf):
    m, k = a.shape
    f = wg.shape[1]
    return pl.pallas_call(
        _swiglu_up_body,
        grid=(f // tf, m // tm),
        in_specs=[
            pl.BlockSpec((tm, k), lambda j, i: (i, 0)),
            pl.BlockSpec((k, tf), lambda j, i: (0, j)),
            pl.BlockSpec((k, tf), lambda j, i: (0, j)),
        ],
        out_specs=pl.BlockSpec((tm, tf), lambda j, i: (i, j)),
        out_shape=jax.ShapeDtypeStruct((m, f), BF16),
        scratch_shapes=[pltpu.VMEM((k, tf), BF16), pltpu.VMEM((k, tf), BF16)],
        compiler_params=_params(("arbitrary", "arbitrary")),
        name="swiglu_up",
    )(a, wg, wu)


def _mlstm_body(q_ref, k_ref, v_ref, og_ref, gc_ref, gr_ref, bc_ref, br_ref, nw_ref, y_ref, hs_ref, c_ref, *, seq):
    L = MLSTM_CHUNK
    nc = seq // L
    row = lax.broadcasted_iota(jnp.int32, (L, L), 0)
    col = lax.broadcasted_iota(jnp.int32, (L, L), 1)
    lower = row >= col
    upper = row <= col
    tril = lower.astype(F32)
    triu = upper.astype(F32)
    hs_ref[...] = jnp.zeros_like(hs_ref)
    c_ref[...] = jnp.zeros_like(c_ref)

    def cap(z):
        return GATE_CAP * jnp.tanh(z / GATE_CAP)

    def one_dir(cidx, d, n, m):
        rows = pl.ds(pl.multiple_of(cidx * L, L), L)
        gcol = cap(gc_ref[cidx] + bc_ref[...])
        grow = cap(gr_ref[cidx] + br_ref[...])
        lcol = _log_sigmoid(gcol)
        lrow = _log_sigmoid(grow)
        if d == 0:
            brow = _dot(lrow, triu, precision=HIGHEST)
            mask = lower
        else:
            brow = _dot(lrow, tril, precision=HIGHEST)
            mask = upper
        li_col = gcol[:, 2 * d:2 * d + 1]
        lf_wide = jnp.broadcast_to(lcol[:, 2 * d + 1:2 * d + 2], (L, V7X_LANES))
        b_col = _cumsum_rows(lf_wide, reverse=(d == 1))[:, 0:1]
        li_row = grow[2 * d:2 * d + 1, :]
        b_row = brow[2 * d + 1:2 * d + 2, :]
        g = b_col[L - 1:L, :] if d == 0 else b_col[0:1, :]

        q = q_ref[rows, :].astype(F32) * (MLSTM_DK ** -0.5)
        k = k_ref[rows, :].astype(F32)
        qb = q.astype(BF16)
        kb = k.astype(BF16)
        vb = v_ref[rows, :].astype(BF16)

        logd = jnp.where(mask, b_col + (li_row - b_row), -jnp.inf)
        m_t = jnp.maximum(jnp.max(logd, axis=1, keepdims=True), b_col + m)
        s = _dot_nt(qb, kb) * jnp.exp(logd - m_t)
        inter_w = jnp.exp(b_col + m - m_t)
        c_old = c_ref[d]
        num = _dot(s.astype(BF16), vb) + inter_w * _dot(qb, c_old.astype(BF16))
        den = jnp.sum(s, axis=1, keepdims=True) + inter_w * jnp.sum(q * n, axis=1, keepdims=True)
        h = num / jnp.maximum(jnp.abs(den), jnp.exp(-m_t))
        hs_ref[rows, :] += h

        a_col = g - b_col + li_col
        m_loc = jnp.max(a_col, axis=0, keepdims=True)
        kw = k * jnp.exp(a_col - m_loc)
        c_loc = _dot_tn(kw.astype(BF16), vb)
        n_loc = jnp.sum(kw, axis=0, keepdims=True)
        m_new = jnp.maximum(g + m, m_loc)
        decay = jnp.exp(g + m - m_new)
        inj = jnp.exp(m_loc - m_new)
        c_ref[d] = decay * c_old + inj * c_loc
        return decay * n + inj * n_loc, m_new

    def body(i, carry):
        nf, mf, nb, mb = carry
        nf, mf = one_dir(i, 0, nf, mf)
        nb, mb = one_dir(nc - 1 - i, 1, nb, mb)
        return nf, mf, nb, mb

    zn = jnp.zeros((1, MLSTM_DK), F32)
    zm = jnp.zeros((1, 1), F32)
    lax.fori_loop(0, nc, body, (zn, zm, zn, zm), unroll=8)

    def fin(i, carry):
        rows = pl.ds(pl.multiple_of(i * L, L), L)
        hh = hs_ref[rows, :]
        r = lax.rsqrt(jnp.mean(hh * hh, axis=-1, keepdims=True) + RMS_EPS)
        y_ref[rows, :] = (hh * r * nw_ref[...] * jax.nn.sigmoid(og_ref[rows, :].astype(F32))).astype(y_ref.dtype)
        return carry

    lax.fori_loop(0, nc, fin, 0)


def _mlstm(u, gc, gr, bias_c, bias_r, norm_w):
    bsz, seq, _ = u.shape
    L = MLSTM_CHUNK
    nc = seq // L
    H, dk, dv = MLSTM_HEADS, MLSTM_DK, MLSTM_DV
    k_off = H * dk // dk
    v_off = 2 * H * dk // dv
    o_off = v_off + H
    return pl.pallas_call(
        functools.partial(_mlstm_body, seq=seq),
        grid=(bsz, H),
        in_specs=[
            pl.BlockSpec((None, seq, dk), lambda b, h: (b, 0, h)),
            pl.BlockSpec((None, seq, dk), lambda b, h: (b, 0, k_off + h)),
            pl.BlockSpec((None, seq, dv), lambda b, h: (b, 0, v_off + h)),
            pl.BlockSpec((None, seq, dv), lambda b, h: (b, 0, o_off + h)),
            pl.BlockSpec((None, None, nc, L, 4), lambda b, h: (b, h, 0, 0, 0)),
            pl.BlockSpec((None, None, nc, 4, L), lambda b, h: (b, h, 0, 0, 0)),
            pl.BlockSpec((None, 1, 4), lambda b, h: (h, 0, 0)),
            pl.BlockSpec((None, 4, 1), lambda b, h: (h, 0, 0)),
            pl.BlockSpec((1, dv), lambda b, h: (0, h)),
        ],
        out_specs=pl.BlockSpec((None, seq, dv), lambda b, h: (b, 0, h)),
        out_shape=jax.ShapeDtypeStruct((bsz, seq, H * dv), BF16),
        scratch_shapes=[pltpu.VMEM((seq, dv), F32), pltpu.VMEM((2, dk, dv), F32)],
        compiler_params=_params(("parallel", "arbitrary")),
        name="mlstm",
    )(u, u, u, u, gc, gr, bias_c, bias_r, norm_w.reshape(1, H * dv))


def _cumsum_rows(x, reverse):
    n = x.shape[0]
    ridx = lax.broadcasted_iota(jnp.int32, x.shape, 0)
    s = 1
    while s < n:
        if reverse:
            x = x + jnp.where(ridx < n - s, pltpu.roll(x, n - s, axis=0), 0.0)
        else:
            x = x + jnp.where(ridx >= s, pltpu.roll(x, s, axis=0), 0.0)
        s *= 2
    return x


def _hgrn_body(q_ref, ff_ref, fb_ref, i_ref, g_ref, lbl_ref, nw_ref, y_ref, os_ref, st_ref, *, seq, layer):
    L = HGRN_CHUNK
    SB = HGRN_SUB
    nc = seq // L
    os_ref[...] = jnp.zeros_like(os_ref)
    st_ref[...] = jnp.zeros_like(st_ref)

    def lower_bound(d):
        lg = lbl_ref[d]
        e = jnp.exp(lg - jnp.max(lg, axis=0, keepdims=True))
        sm = e / jnp.sum(e, axis=0, keepdims=True)
        return jnp.sum(sm[:layer + 1, :], axis=0, keepdims=True)

    lbs = (lower_bound(0), lower_bound(1))

    def one_dir(cidx, d):
        rows = pl.ds(pl.multiple_of(cidx * L, L), L)
        qr = q_ref[rows, :].astype(F32)
        q = qr * jax.nn.sigmoid(qr)
        vb = i_ref[rows, :].astype(BF16)
        fr = (ff_ref if d == 0 else fb_ref)[rows, :]
        lb = lbs[d]
        f = lb + (1.0 - lb) * jax.nn.sigmoid(fr)
        k = 1.0 - f
        lf = jnp.log(f)
        b = _cumsum_rows(lf, reverse=(d == 1))
        g = b[L - 1:L, :] if d == 0 else b[0:1, :]
        qi = (q * jnp.exp(b)).astype(BF16)
        ke = (k * jnp.exp(g - b)).astype(BF16)
        parts = []
        for jb in range(L // SB):
            lo, hi = jb * SB, (jb + 1) * SB
            bm = b[lo + SB // 2:lo + SB // 2 + 1, :]
            qm = (q[lo:hi, :] * jnp.exp(b[lo:hi, :] - bm)).astype(BF16)
            ks = slice(0, hi) if d == 0 else slice(lo, L)
            km = (k[ks, :] * jnp.exp(bm - b[ks, :])).astype(BF16)
            if km.shape[0] < L:
                pad = jnp.zeros((L - km.shape[0], HGRN_DK), BF16)
                km = jnp.concatenate([km, pad] if d == 0 else [pad, km], axis=0)
            tq = lo + lax.broadcasted_iota(jnp.int32, (SB, L), 0)
            ts = lax.broadcasted_iota(jnp.int32, (SB, L), 1)
            keep = (ts <= tq) if d == 0 else (ts >= tq)
            parts.append(jnp.where(keep, _dot_nt(qm, km), 0.0))
        a = jnp.concatenate(parts, axis=0)
        st = st_ref[d]
        o = _dot(a.astype(BF16), vb) + _dot_nt(qi, st.astype(BF16))
        st_ref[d] = st * jnp.exp(g) + _dot_tn(vb, ke)
        os_ref[rows, :] += o

    def body(i, carry):
        one_dir(i, 0)
        one_dir(nc - 1 - i, 1)
        return carry

    lax.fori_loop(0, nc, body, 0, unroll=8)

    FL = 256

    def fin(i, carry):
        rows = pl.ds(pl.multiple_of(i * FL, FL), FL)
        hh = os_ref[rows, :]
        r = lax.rsqrt(jnp.mean(hh * hh, axis=-1, keepdims=True) + RMS_EPS)
        gg = g_ref[rows, :].astype(F32)
        y_ref[rows, :] = (hh * r * nw_ref[...] * (gg * jax.nn.sigmoid(gg))).astype(y_ref.dtype)
        return carry

    lax.fori_loop(0, seq // FL, fin, 0)


def _hgrn(u, uf, lb_logits, norm_w, layer):
    bsz, seq, _ = u.shape
    H, dk = HGRN_HEADS, HGRN_DK
    base = (2 * MLSTM_HEADS * MLSTM_DK + 2 * A_WIDTH) // dk
    slots = lb_logits.shape[1]
    return pl.pallas_call(
        functools.partial(_hgrn_body, seq=seq, layer=layer),
        grid=(bsz, H),
        in_specs=[
            pl.BlockSpec((None, seq, dk), lambda b, h: (b, 0, base + h)),
            pl.BlockSpec((None, seq, dk), lambda b, h: (b, 0, h)),
            pl.BlockSpec((None, seq, dk), lambda b, h: (b, 0, H + h)),
            pl.BlockSpec((None, seq, dk), lambda b, h: (b, 0, base + H + h)),
            pl.BlockSpec((None, seq, dk), lambda b, h: (b, 0, base + 2 * H + h)),
            pl.BlockSpec((2, slots, dk), lambda b, h: (0, 0, h)),
            pl.BlockSpec((1, dk), lambda b, h: (0, h)),
        ],
        out_specs=pl.BlockSpec((None, seq, dk), lambda b, h: (b, 0, h)),
        out_shape=jax.ShapeDtypeStruct((bsz, seq, H * dk), BF16),
        scratch_shapes=[pltpu.VMEM((seq, dk), F32), pltpu.VMEM((2, dk, dk), F32)],
        compiler_params=_params(("parallel", "arbitrary")),
        name="hgrn2",
    )(u, uf, uf, u, u, lb_logits, norm_w.reshape(1, H * dk))


def _na_body(q_ref, k_ref, v_ref, tb_ref, o_ref, bias_ref, *, rows):
    W = GRID_W
    G, U, kh = NA_GROUP, NA_UNION, NA_KH
    ng = rows // G
    scale = NA_DH ** -0.5

    @pl.when(pl.program_id(1) == 0)
    def _():
        for c, (delta, offs) in enumerate(_na_group_classes(rows)):
            for i in range(G):
                qr = pl.ds(i * W, W)
                for k0 in range(0, U, 2):
                    v0 = offs[i] <= k0 < offs[i] + kh
                    v1 = k0 + 1 < U and offs[i] <= k0 + 1 < offs[i] + kh
                    dr = delta + k0 - i + kh - 1
                    if k0 + 1 >= U:
                        blk = tb_ref[1, dr][:, :W] if v0 else jnp.full((W, W), NEG_BIG, F32)
                        bias_ref[c, qr, pl.ds(k0 * W, W)] = blk
                        continue
                    if v0 and v1:
                        blk = tb_ref[0, dr]
                    elif v0:
                        blk = tb_ref[1, dr]
                    elif v1:
                        blk = tb_ref[2, dr + 1]
                    else:
                        blk = jnp.full((W, 2 * W), NEG_BIG, F32)
                    bias_ref[c, qr, pl.ds(k0 * W, 2 * W)] = blk

    def body(gi, carry):
        r0 = gi * G
        us = jnp.clip(r0 - NA_KH // 2, 0, rows - U)
        cls = jnp.where(gi == 0, 0, jnp.where(gi == ng - 1, 2, 1))
        qrows = pl.ds(pl.multiple_of(r0 * W, G * W), G * W)
        kwin = pl.ds(pl.multiple_of(us * W, W), U * W)
        s = _dot_nt(q_ref[qrows, :], k_ref[kwin, :]) * scale + bias_ref[cls]
        m = jnp.max(s, axis=-1, keepdims=True)
        p = jnp.exp(s - m)
        den = jnp.sum(p, axis=-1, keepdims=True)
        o = _dot(p.astype(BF16), v_ref[kwin, :]) / den
        o_ref[qrows, :] = o.astype(o_ref.dtype)
        return carry

    lax.fori_loop(0, ng, body, 0, unroll=16)


def _na_group_classes(rows):
    G, U, kh = NA_GROUP, NA_UNION, NA_KH

    def info(r0):
        us = min(max(r0 - kh // 2, 0), rows - U)
        return us - r0, tuple(min(max(r0 + i - kh // 2, 0), rows - kh) - us for i in range(G))

    ng = rows // G
    infos = [info(G * g) for g in range(ng)]
    classes = [infos[0], infos[1], infos[-1]]
    assert rows % G == 0 and ng >= 3 and all(infos[g] == classes[1] for g in range(1, ng - 1))
    assert all(0 <= o and o + kh <= U for c in classes for o in c[1])
    return classes


def _na_bias_table(rpb):
    W = GRID_W
    ndr, ndc = rpb.shape[1], rpb.shape[2]
    col = np.arange(W)
    col_start = np.clip(col - NA_KW // 2, 0, W - NA_KW)
    inside = (col[None, :] >= col_start[:, None]) & (col[None, :] < col_start[:, None] + NA_KW)
    dc = np.clip(col[None, :] - col[:, None] + NA_KW - 1, 0, ndc - 1)
    sel = np.zeros((ndc, W, W), np.float32)
    qq, kk = np.nonzero(inside)
    sel[dc[qq, kk], qq, kk] = 1.0
    off = np.zeros_like(sel)
    sel_left = jnp.asarray(np.concatenate([sel, off], axis=2))
    sel_right = jnp.asarray(np.concatenate([off, sel], axis=2))
    mask = np.where(inside, 0.0, NEG_BIG).astype(np.float32)
    allneg = np.full_like(mask, NEG_BIG)
    neg_pair = np.tile(np.concatenate([mask, mask], axis=1), (ndr, 1, 1))
    neg_pair[ndr - 1, :, W:] = NEG_BIG
    neg_lo = np.concatenate([mask, allneg], axis=1)
    neg_hi = np.concatenate([allneg, mask], axis=1)
    r = rpb.astype(F32)
    r_next = jnp.concatenate([r[:, 1:], jnp.zeros_like(r[:, :1])], axis=1)

    def expand(rows, sel_half):
        return jnp.einsum('hdm,mqk->hdqk', rows, sel_half, precision=HIGHEST)

    pair = expand(r, sel_left) + expand(r_next, sel_right) + neg_pair
    lo = expand(r, sel_left) + neg_lo
    hi = expand(r, sel_right) + neg_hi
    return jnp.stack([pair, lo, hi], axis=1)


def _na(qkv, bias_tbl):
    bsz, seq, _ = qkv.shape
    rows = seq // GRID_W
    H, dh = NA_HEADS, NA_DH
    return pl.pallas_call(
        functools.partial(_na_body, rows=rows),
        grid=(H, bsz),
        in_specs=[
            pl.BlockSpec((None, seq, dh), lambda h, b: (b, 0, h)),
            pl.BlockSpec((None, seq, dh), lambda h, b: (b, 0, H + h)),
            pl.BlockSpec((None, seq, dh), lambda h, b: (b, 0, 2 * H + h)),
            pl.BlockSpec((None,) + bias_tbl.shape[1:], lambda h, b: (h, 0, 0, 0, 0)),
        ],
        out_specs=pl.BlockSpec((None, seq, dh), lambda h, b: (b, 0, h)),
        out_shape=jax.ShapeDtypeStruct((bsz, seq, H * dh), BF16),
        scratch_shapes=[pltpu.VMEM((3, NA_GROUP * GRID_W, NA_UNION * GRID_W), F32)],
        compiler_params=_params(("arbitrary", "arbitrary")),
        name="natten",
    )(qkv, qkv, qkv, bias_tbl)


def _router_body(x_ref, wh_ref, wl_ref, b_ref, e_ref, g_ref):
    x = x_ref[...]
    xh = x.astype(BF16)
    xl = (x - xh.astype(F32)).astype(BF16)
    logits = _dot(xh, wh_ref[...]) + _dot(xl, wh_ref[...]) + _dot(xh, wl_ref[...]) + b_ref[...]
    lane = lax.broadcasted_iota(jnp.int32, logits.shape, 1)
    nl = logits.shape[1]
    m1 = jnp.max(logits, axis=-1, keepdims=True)
    i1 = jnp.min(jnp.where(logits == m1, lane, nl), axis=-1, keepdims=True)
    rest = jnp.where(lane == i1, -jnp.inf, logits)
    m2 = jnp.max(rest, axis=-1, keepdims=True)
    i2 = jnp.min(jnp.where(rest == m2, lane, nl), axis=-1, keepdims=True)
    ex = jnp.exp(m2 - m1)
    g1 = 1.0 / (1.0 + ex)
    g2 = ex / (1.0 + ex)
    e_ref[...] = jnp.where(lane == 0, i1, jnp.where(lane == 1, i2, 0))
    g_ref[...] = jnp.where(lane == 0, g1, jnp.where(lane == 1, g2, 0.0))


def _router(x, w_router, b_router, tm):
    n, d = x.shape
    ne = w_router.shape[1]
    w = jnp.zeros((d, V7X_LANES), F32).at[:, :ne].set(w_router.astype(F32))
    b = jnp.full((1, V7X_LANES), NEG_BIG, F32).at[0, :ne].set(b_router.astype(F32))
    wh = w.astype(BF16)
    wl = (w - wh.astype(F32)).astype(BF16)
    return pl.pallas_call(
        _router_body,
        grid=(n // tm,),
        in_specs=[
            pl.BlockSpec((tm, d), lambda i: (i, 0)),
            pl.BlockSpec((d, V7X_LANES), lambda i: (0, 0)),
            pl.BlockSpec((d, V7X_LANES), lambda i: (0, 0)),
            pl.BlockSpec((1, V7X_LANES), lambda i: (0, 0)),
        ],
        out_specs=[pl.BlockSpec((tm, V7X_LANES), lambda i: (i, 0)), pl.BlockSpec((tm, V7X_LANES), lambda i: (i, 0))],
        out_shape=[jax.ShapeDtypeStruct((n, V7X_LANES), jnp.int32), jax.ShapeDtypeStruct((n, V7X_LANES), F32)],
        compiler_params=_params(("parallel",)),
        name="router_top2",
    )(x, wh, wl, b)


def _pack_bf16_pairs(xb):
    half = xb.shape[1] // 2
    lo = lax.bitcast_convert_type(xb[:, :half], jnp.uint16).astype(jnp.uint32)
    hi = lax.bitcast_convert_type(xb[:, half:], jnp.uint16).astype(jnp.uint32)
    return lo | (hi << 16)


def _moe_body(te_ref, nv_ref, xp_ref, wg_ref, wu_ref, wd_ref, o_ref, xb_ref, wgb_ref, wub_ref, wdb_ref, *, blocks):
    t = pl.program_id(0)
    f = pl.program_id(1)
    nv = nv_ref[t]
    tm, half = xp_ref.shape
    tf = wg_ref.shape[1]
    d = o_ref.shape[1]
    nc = MXU_COLS

    @pl.when(f == 0)
    def _():
        o_ref[...] = jnp.zeros_like(o_ref)
        for start, size in blocks:
            rows = pl.ds(start, size)
            w = xp_ref[rows, :]
            xb_ref[rows, :half] = lax.bitcast_convert_type(w << 16, F32).astype(BF16)
            xb_ref[rows, half:] = lax.bitcast_convert_type(w & jnp.uint32(0xFFFF0000), F32).astype(BF16)

    for sb, (start, size) in enumerate(blocks):
        rows = pl.ds(start, size)

        @pl.when(start < nv)
        def _():
            hs = []
            for c in range(tf // nc):
                cols = pl.ds(c * nc, nc)
                if sb == 0:
                    wg = wg_ref[:, cols].astype(BF16)
                    wu = wu_ref[:, cols].astype(BF16)
                    wgb_ref[:, cols] = wg
                    wub_ref[:, cols] = wu
                else:
                    wg = wgb_ref[:, cols]
                    wu = wub_ref[:, cols]
                g = _dot(xb_ref[rows, :], wg)
                u = _dot(xb_ref[rows, :], wu)
                hs.append((g * jax.nn.sigmoid(g) * u).astype(BF16))
            h = jnp.concatenate(hs, axis=1)
            for c in range(d // nc):
                cols = pl.ds(c * nc, nc)
                if sb == 0:
                    wd = wd_ref[:, cols].astype(BF16)
                    wdb_ref[:, cols] = wd
                else:
                    wd = wdb_ref[:, cols]
                o_ref[rows, cols] += _dot(h, wd)


def _moe_experts(xp, tile_expert, tile_valid, wg, wu, wd, tm, tf, block_sizes):
    rows, half = xp.shape
    d = 2 * half
    n_tiles = rows // tm
    fdim = wg.shape[2]
    nf = fdim // tf
    assert sum(block_sizes) == tm
    blocks = tuple((sum(block_sizes[:i]), s) for i, s in enumerate(block_sizes))

    def fidx(t, f, nv):
        return jnp.where(nv[t] > 0, f, nf - 1)

    return pl.pallas_call(
        functools.partial(_moe_body, blocks=blocks),
        grid_spec=pltpu.PrefetchScalarGridSpec(
            num_scalar_prefetch=2,
            grid=(n_tiles, nf),
            in_specs=[
                pl.BlockSpec((tm, half), lambda t, f, te, nv: (t, 0), pipeline_mode=pl.Buffered(1)),
                pl.BlockSpec((None, d, tf), lambda t, f, te, nv: (te[t], 0, fidx(t, f, nv))),
                pl.BlockSpec((None, d, tf), lambda t, f, te, nv: (te[t], 0, fidx(t, f, nv))),
                pl.BlockSpec((None, tf, d), lambda t, f, te, nv: (te[t], fidx(t, f, nv), 0)),
            ],
            out_specs=pl.BlockSpec((tm, d), lambda t, f, te, nv: (t, 0)),
            scratch_shapes=[pltpu.VMEM((tm, d), BF16), pltpu.VMEM((d, tf), BF16), pltpu.VMEM((d, tf), BF16), pltpu.VMEM((tf, d), BF16)],
        ),
        out_shape=jax.ShapeDtypeStruct((rows, d), F32),
        compiler_params=pltpu.CompilerParams(dimension_semantics=("arbitrary", "arbitrary"), vmem_limit_bytes=MOE_VMEM_LIMIT),
        name="moe_experts",
    )(tile_expert, tile_valid, xp, wg, wu, wd)


def _combine_ln_body(x_ref, y0_ref, y1_ref, g_ref, w_ref, b_ref, p_ref, wpg_ref, wpp_ref, o_ref, ob_ref, xs_ref):
    g = g_ref[...]
    mix = y0_ref[...] * g[:, 0:1] + y1_ref[...] * g[:, 1:2]
    out = _layer_norm(ALPHA * x_ref[...] + mix, w_ref[...], b_ref[...])
    o_ref[...] = out
    xs_ref[...] = out.astype(BF16)
    _ple_epilogue(p_ref, wpg_ref, wpp_ref, xs_ref, o_ref, ob_ref)


def _combine_ln(x, y0, y1, gates, ln_w, ln_b, ple, tm):
    n, d = x.shape
    p, layer, wpg, wpp = ple
    row_spec = pl.BlockSpec((tm, d), lambda i: (i, 0))
    vec_spec = pl.BlockSpec((1, d), lambda i: (0, 0))

    def whole(arr):
        return pl.BlockSpec(arr.shape, lambda i: (0, 0), pipeline_mode=pl.Buffered(1))

    return pl.pallas_call(
        _combine_ln_body,
        grid=(n // tm,),
        in_specs=[row_spec, row_spec, row_spec, pl.BlockSpec((tm, V7X_LANES), lambda i: (i, 0)), vec_spec, vec_spec,
                  pl.BlockSpec((None, tm, p.shape[2]), lambda i: (layer, i, 0)), whole(wpg), whole(wpp)],
        out_specs=[row_spec, row_spec],
        out_shape=[jax.ShapeDtypeStruct((n, d), F32), jax.ShapeDtypeStruct((n, d), BF16)],
        scratch_shapes=[pltpu.VMEM((tm, d), BF16)],
        compiler_params=_params(("parallel",)),
        name="combine_ln",
    )(x, y0, y1, gates, ln_w.reshape(1, d), ln_b.reshape(1, d), p, wpg, wpp)


def _moe(x, xb, w_router, b_router, wg, wu, wd, ln_w, ln_b, ple):
    n, d = x.shape
    tm = MOE_TILE
    nk = n * TOP_K
    e_out, g_out = _router(x, w_router, b_router, ROUTER_TM)
    e_flat = e_out[:, :TOP_K].reshape(-1)
    onehot = (e_flat[:, None] == jnp.arange(N_EXPERTS, dtype=jnp.int32)[None, :]).astype(jnp.int32)
    csum = jnp.cumsum(onehot, axis=0)
    counts = csum[-1]
    pos = jnp.take_along_axis(csum, e_flat[:, None], axis=1)[:, 0] - 1
    padded = (counts + tm - 1) // tm * tm
    pad_end = jnp.cumsum(padded)
    pad_start = pad_end - padded
    dest = pad_start[e_flat] + pos
    n_tiles = -(-nk // tm) + N_EXPERTS
    row_tok = (jnp.arange(n_tiles * tm, dtype=jnp.int32) % n).at[dest].set(jnp.arange(nk, dtype=jnp.int32) // TOP_K, unique_indices=True)
    n_active = (pad_end[-1] // tm).astype(jnp.int32)
    tile_all = jnp.arange(n_tiles, dtype=jnp.int32)
    tile_ids = jnp.minimum(tile_all, n_active - 1)
    tile_expert = jnp.sum((pad_end[None, :] <= (tile_ids * tm)[:, None]).astype(jnp.int32), axis=1)
    tile_expert = jnp.minimum(tile_expert, N_EXPERTS - 1)
    tile_valid = jnp.clip((pad_start + counts)[tile_expert] - tile_ids * tm, 0, tm)
    tile_valid = jnp.where(tile_all < n_active, tile_valid, 0).astype(jnp.int32)
    xs = _pack_bf16_pairs(xb)[row_tok]
    ys = _moe_experts(xs, tile_expert, tile_valid, wg, wu, wd, tm, MOE_TF, MOE_BLOCKS)
    dest2 = dest.reshape(n, TOP_K)
    return _combine_ln(x, ys[dest2[:, 0]], ys[dest2[:, 1]], g_out, ln_w, ln_b, ple, COMBINE_TM)


def kernel(x, p, ln_w, ln_b, rec_w_in, mlstm_gate_bias, mlstm_norm_w, hgrn_lb_logits, hgrn_norm_w, rec_w_out, ffn_w_gate, ffn_w_up, ffn_w_down, na_w_qkv, na_rpb, na_w_out, moe_w_router, moe_b_router, moe_w_gate, moe_w_up, moe_w_down, ple_w_gate, ple_w_proj):
    bsz, seq, d = x.shape
    n = bsz * seq
    depth = ln_w.shape[0]
    xf = x.reshape(n, d).astype(F32)
    xb = None
    H = MLSTM_HEADS
    gate_lo = 2 * H * MLSTM_DK + 2 * A_WIDTH
    gate_hi = gate_lo + 4 * H
    for i in range(depth):
        j = i // 2
        ple = (p.reshape(depth, n, -1), i, ple_w_gate[i].astype(BF16), ple_w_proj[i].astype(BF16))
        if i % 2 == 0:
            w_in = rec_w_in[j]
            qb_lo, ff_lo, ib_lo = gate_hi, gate_hi + B_WIDTH, gate_hi + 3 * B_WIDTH
            w_main = jnp.concatenate([w_in[:, :gate_lo], w_in[:, qb_lo:ff_lo], w_in[:, ib_lo:]], axis=1).astype(BF16)
            w_forget = w_in[:, ff_lo:ib_lo].astype(BF16)
            w_gate = jnp.zeros((d, V7X_LANES), BF16).at[:, :4 * H].set(w_in[:, gate_lo:gate_hi].astype(BF16))
            if xb is None:
                xb, graw = _cast_and_gates(xf, w_gate, ROUTER_TM)
            else:
                graw = _matmul(xb, w_gate, F32, MM_TM, V7X_LANES)
            graw = graw[:, :4 * H].reshape(bsz, seq, 4, H)
            u = _matmul(xb, w_main, BF16, MM_TM, MM_TN).reshape(bsz, seq, -1)
            uf = _matmul(xb, w_forget, F32, MM_TM, MM_TN).reshape(bsz, seq, -1)
            L = MLSTM_CHUNK
            gc = graw.transpose(0, 3, 1, 2).reshape(bsz, H, seq // L, L, 4)
            gr = gc.transpose(0, 1, 2, 4, 3)
            bias = mlstm_gate_bias[j].astype(F32).T
            y_a = _mlstm(u, gc, gr, bias.reshape(H, 1, 4), bias.reshape(H, 4, 1), mlstm_norm_w[j].astype(F32))
            y_b = _hgrn(u, uf, hgrn_lb_logits.astype(F32), hgrn_norm_w[j].astype(F32), j)
            mix_in = (y_a.reshape(n, -1), y_b.reshape(n, -1))
            xf, xb = _matmul_ln(mix_in, rec_w_out[j].astype(BF16), xf, ln_w[i, 0], ln_b[i, 0], LN_TM)
            hid = _swiglu_up(xb, ffn_w_gate[j], ffn_w_up[j], MM_TM, SWIGLU_TF)
            xf, xb = _matmul_ln((hid,), ffn_w_down[j].astype(BF16), xf, ln_w[i, 1], ln_b[i, 1], FFN_DOWN_TM, ple=ple)
        else:
            if xb is None:
                xb = xf.astype(BF16)
            qkv = _matmul(xb, na_w_qkv[j], BF16, MM_TM, MM_TN).reshape(bsz, seq, -1)
            att = _na(qkv, _na_bias_table(na_rpb[j])).reshape(n, -1)
            xf, xb = _matmul_ln((att,), na_w_out[j].astype(BF16), xf, ln_w[i, 0], ln_b[i, 0], LN_TM)
            xf, xb = _moe(xf, xb, moe_w_router[j], moe_b_router[j], moe_w_gate[j], moe_w_up[j], moe_w_down[j], ln_w[i, 1], ln_b[i, 1], ple)
    return xf.reshape(bsz, seq, d)
```

```python
import functools

import jax
import jax.numpy as jnp
import numpy as np
from jax import lax
from jax.experimental import pallas as pl
from jax.experimental.pallas import tpu as pltpu

F32 = jnp.float32
BF16 = jnp.bfloat16
HIGHEST = lax.Precision.HIGHEST

DEPTH = 2
ALPHA = (2 * DEPTH) ** 0.25
LN_EPS = 1e-5
RMS_EPS = 1e-6
GRID_W = 64

MLSTM_HEADS = 4
MLSTM_DK = 128
MLSTM_DV = 256
GATE_CAP = 15.0
HGRN_HEADS = 8
HGRN_DK = 128
A_WIDTH = MLSTM_HEADS * MLSTM_DV
B_WIDTH = HGRN_HEADS * HGRN_DK

NA_DH = 128
NA_HEADS = 16
NA_KH = 8
NA_KW = 16

N_EXPERTS = 8
TOP_K = 2

MLSTM_CHUNK = 256
HGRN_CHUNK = 128
HGRN_SUB = 32
MOE_TILE = 1536
MOE_BLOCKS = (512, 512, 512)
MOE_TF = 512
LN_ROW_GROUP = 128
MM_TM = 1024
MM_TN = 1024
SWIGLU_TF = 512
LN_TM = 512
FFN_DOWN_TM = 256
ROUTER_TM = 512
COMBINE_TM = 256
NA_GROUP = 4
NA_UNION = NA_KH + NA_GROUP - 1

V7X_LANES = 128
MXU_COLS = 256
VMEM_LIMIT = 56 * 1024 * 1024
MOE_VMEM_LIMIT = 60 * 1024 * 1024
NEG_BIG = -1e30


def _params(sem):
    return pltpu.CompilerParams(dimension_semantics=sem, vmem_limit_bytes=VMEM_LIMIT)


def _dot(a, b, **kw):
    return jnp.dot(a, b, preferred_element_type=F32, **kw)


def _dot_nt(a, b):
    return lax.dot_general(a, b, (((1,), (1,)), ((), ())), preferred_element_type=F32)


def _dot_tn(a, b):
    return lax.dot_general(a, b, (((0,), (0,)), ((), ())), preferred_element_type=F32)


def _layer_norm(y, w, b):
    mu = jnp.mean(y, axis=-1, keepdims=True)
    yc = y - mu
    var = jnp.mean(yc * yc, axis=-1, keepdims=True)
    return yc * lax.rsqrt(var + LN_EPS) * w + b


def _log_sigmoid(z):
    return jnp.minimum(z, 0.0) - jnp.log(1.0 + jnp.exp(-jnp.abs(z)))


def _mm_body(a_ref, w_ref, o_ref, wb_ref):
    @pl.when(pl.program_id(1) == 0)
    def _():
        wb_ref[...] = w_ref[...].astype(BF16)

    o_ref[...] = _dot(a_ref[...], wb_ref[...]).astype(o_ref.dtype)


def _matmul(a, w, out_dtype, tm, tn):
    m, k = a.shape
    n = w.shape[1]
    return pl.pallas_call(
        _mm_body,
        grid=(n // tn, m // tm),
        in_specs=[pl.BlockSpec((tm, k), lambda j, i: (i, 0)), pl.BlockSpec((k, tn), lambda j, i: (0, j))],
        out_specs=pl.BlockSpec((tm, tn), lambda j, i: (i, j)),
        out_shape=jax.ShapeDtypeStruct((m, n), out_dtype),
        scratch_shapes=[pltpu.VMEM((k, tn), BF16)],
        compiler_params=_params(("arbitrary", "arbitrary")),
        name="matmul",
    )(a, w)


def _cast_gates_body(x_ref, w_ref, xb_ref, g_ref):
    xb = x_ref[...].astype(BF16)
    xb_ref[...] = xb
    g_ref[...] = _dot(xb, w_ref[...])


def _cast_and_gates(x, w_gate, tm):
    m, d = x.shape
    n = w_gate.shape[1]
    return pl.pallas_call(
        _cast_gates_body,
        grid=(m // tm,),
        in_specs=[pl.BlockSpec((tm, d), lambda i: (i, 0)), pl.BlockSpec((d, n), lambda i: (0, 0))],
        out_specs=[pl.BlockSpec((tm, d), lambda i: (i, 0)), pl.BlockSpec((tm, n), lambda i: (i, 0))],
        out_shape=[jax.ShapeDtypeStruct((m, d), BF16), jax.ShapeDtypeStruct((m, n), F32)],
        compiler_params=_params(("parallel",)),
        name="cast_gates",
    )(x, w_gate)


def _ple_epilogue(p_ref, wpg_ref, wpp_ref, xs_ref, o_ref, ob_ref):
    for c in range(o_ref.shape[1] // MXU_COLS):
        cols = pl.ds(c * MXU_COLS, MXU_COLS)
        gate = jax.nn.sigmoid(_dot(xs_ref[...], wpg_ref[:, cols]))
        proj = _dot(p_ref[...].astype(BF16), wpp_ref[:, cols])
        new = o_ref[:, cols] + gate * proj
        o_ref[:, cols] = new
        ob_ref[:, cols] = new.astype(BF16)


def _mm_ln_body(*refs, na, ple):
    a_refs = refs[:na]
    w_ref, res_ref, g_ref, b_ref = refs[na:na + 4]
    rest = refs[na + 4:]
    if ple:
        p_ref, wpg_ref, wpp_ref, o_ref, ob_ref, xs_ref = rest
    else:
        o_ref, ob_ref = rest
    part = LN_ROW_GROUP
    for r in range(res_ref.shape[0] // part):
        rows = pl.ds(r * part, part)
        y = ALPHA * res_ref[rows, :]
        k0 = 0
        for a_ref in a_refs:
            ka = a_ref.shape[1]
            y = y + _dot(a_ref[rows, :], w_ref[k0:k0 + ka, :])
            k0 += ka
        out = _layer_norm(y, g_ref[...], b_ref[...])
        o_ref[rows, :] = out
        (xs_ref if ple else ob_ref)[rows, :] = out.astype(BF16)
    if ple:
        _ple_epilogue(p_ref, wpg_ref, wpp_ref, xs_ref, o_ref, ob_ref)


def _matmul_ln(a_parts, w, res, ln_w, ln_b, tm, ple=None):
    m = res.shape[0]
    k, n = w.shape
    assert sum(a.shape[1] for a in a_parts) == k
    row_spec = pl.BlockSpec((tm, n), lambda i: (i, 0))
    vec_spec = pl.BlockSpec((1, n), lambda i: (0, 0))

    def whole(arr):
        return pl.BlockSpec(arr.shape, lambda i: (0, 0), pipeline_mode=pl.Buffered(1))

    in_specs = [pl.BlockSpec((tm, a.shape[1]), lambda i: (i, 0)) for a in a_parts] + [whole(w), row_spec, vec_spec, vec_spec]
    args = [*a_parts, w, res, ln_w.reshape(1, n), ln_b.reshape(1, n)]
    scratch = []
    if ple is not None:
        p, layer, wpg, wpp = ple
        in_specs += [pl.BlockSpec((None, tm, p.shape[2]), lambda i: (layer, i, 0)), whole(wpg), whole(wpp)]
        args += [p, wpg, wpp]
        scratch = [pltpu.VMEM((tm, n), BF16)]
    return pl.pallas_call(
        functools.partial(_mm_ln_body, na=len(a_parts), ple=ple is not None),
        grid=(m // tm,),
        in_specs=in_specs,
        out_specs=[row_spec, row_spec],
        out_shape=[jax.ShapeDtypeStruct((m, n), F32), jax.ShapeDtypeStruct((m, n), BF16)],
        scratch_shapes=scratch,
        compiler_params=_params(("parallel",)),
        name="matmul_ln",
    )(*args)


def _swiglu_up_body(a_ref, wg_ref, wu_ref, o_ref, wgb_ref, wub_ref):
    @pl.when(pl.program_id(1) == 0)
    def _():
        wgb_ref[...] = wg_ref[...].astype(BF16)
        wub_ref[...] = wu_ref[...].astype(BF16)

    for c in range(o_ref.shape[1] // MXU_COLS):
        cols = pl.ds(c * MXU_COLS, MXU_COLS)
        g = _dot(a_ref[...], wgb_ref[:, cols])
        u = _dot(a_ref[...], wub_ref[:, cols])
        o_ref[:, cols] = (g * jax.nn.sigmoid(g) * u).astype(o_ref.dtype)


def _swiglu_up(a, wg, wu, tm, tf):
    m, k = a.shape
    f = wg.shape[1]
    return pl.pallas_call(
        _swiglu_up_body,
        grid=(f // tf, m // tm),
        in_specs=[
            pl.BlockSpec((tm, k), lambda j, i: (i, 0)),
            pl.BlockSpec((k, tf), lambda j, i: (0, j)),
            pl.BlockSpec((k, tf), lambda j, i: (0, j)),
        ],
        out_specs=pl.BlockSpec((tm, tf), lambda j, i: (i, j)),
        out_shape=jax.ShapeDtypeStruct((m, f), BF16),
        scratch_shapes=[pltpu.VMEM((k, tf), BF16), pltpu.VMEM((k, tf), BF16)],
        compiler_params=_params(("arbitrary", "arbitrary")),
        name="swiglu_up",
    )(a, wg, wu)


def _mlstm_body(q_ref, k_ref, v_ref, og_ref, gc_ref, gr_ref, bc_ref, br_ref, nw_ref, y_ref, hs_ref, c_ref, *, seq):
    L = MLSTM_CHUNK
    nc = seq // L
    row = lax.broadcasted_iota(jnp.int32, (L, L), 0)
    col = lax.broadcasted_iota(jnp.int32, (L, L), 1)
    lower = row >= col
    upper = row <= col
    tril = lower.astype(F32)
    triu = upper.astype(F32)
    hs_ref[...] = jnp.zeros_like(hs_ref)
    c_ref[...] = jnp.zeros_like(c_ref)

    def cap(z):
        return GATE_CAP * jnp.tanh(z / GATE_CAP)

    def one_dir(cidx, d, n, m):
        rows = pl.ds(pl.multiple_of(cidx * L, L), L)
        gcol = cap(gc_ref[cidx] + bc_ref[...])
        grow = cap(gr_ref[cidx] + br_ref[...])
        lcol = _log_sigmoid(gcol)
        lrow = _log_sigmoid(grow)
        if d == 0:
            brow = _dot(lrow, triu, precision=HIGHEST)
            mask = lower
        else:
            brow = _dot(lrow, tril, precision=HIGHEST)
            mask = upper
        li_col = gcol[:, 2 * d:2 * d + 1]
        lf_wide = jnp.broadcast_to(lcol[:, 2 * d + 1:2 * d + 2], (L, V7X_LANES))
        b_col = _cumsum_rows(lf_wide, reverse=(d == 1))[:, 0:1]
        li_row = grow[2 * d:2 * d + 1, :]
        b_row = brow[2 * d + 1:2 * d + 2, :]
        g = b_col[L - 1:L, :] if d == 0 else b_col[0:1, :]

        q = q_ref[rows, :].astype(F32) * (MLSTM_DK ** -0.5)
        k = k_ref[rows, :].astype(F32)
        qb = q.astype(BF16)
        kb = k.astype(BF16)
        vb = v_ref[rows, :].astype(BF16)

        logd = jnp.where(mask, b_col + (li_row - b_row), -jnp.inf)
        m_t = jnp.maximum(jnp.max(logd, axis=1, keepdims=True), b_col + m)
        s = _dot_nt(qb, kb) * jnp.exp(logd - m_t)
        inter_w = jnp.exp(b_col + m - m_t)
        c_old = c_ref[d]
        num = _dot(s.astype(BF16), vb) + inter_w * _dot(qb, c_old.astype(BF16))
        den = jnp.sum(s, axis=1, keepdims=True) + inter_w * jnp.sum(q * n, axis=1, keepdims=True)
        h = num / jnp.maximum(jnp.abs(den), jnp.exp(-m_t))
        hs_ref[rows, :] += h

        a_col = g - b_col + li_col
        m_loc = jnp.max(a_col, axis=0, keepdims=True)
        kw = k * jnp.exp(a_col - m_loc)
        c_loc = _dot_tn(kw.astype(BF16), vb)
        n_loc = jnp.sum(kw, axis=0, keepdims=True)
        m_new = jnp.maximum(g + m, m_loc)
        decay = jnp.exp(g + m - m_new)
        inj = jnp.exp(m_loc - m_new)
        c_ref[d] = decay * c_old + inj * c_loc
        return decay * n + inj * n_loc, m_new

    def body(i, carry):
        nf, mf, nb, mb = carry
        nf, mf = one_dir(i, 0, nf, mf)
        nb, mb = one_dir(nc - 1 - i, 1, nb, mb)
        return nf, mf, nb, mb

    zn = jnp.zeros((1, MLSTM_DK), F32)
    zm = jnp.zeros((1, 1), F32)
    lax.fori_loop(0, nc, body, (zn, zm, zn, zm), unroll=4)

    def fin(i, carry):
        rows = pl.ds(pl.multiple_of(i * L, L), L)
        hh = hs_ref[rows, :]
        r = lax.rsqrt(jnp.mean(hh * hh, axis=-1, keepdims=True) + RMS_EPS)
        y_ref[rows, :] = (hh * r * nw_ref[...] * jax.nn.sigmoid(og_ref[rows, :].astype(F32))).astype(y_ref.dtype)
        return carry

    lax.fori_loop(0, nc, fin, 0)


def _mlstm(u, gc, gr, bias_c, bias_r, norm_w):
    bsz, seq, _ = u.shape
    L = MLSTM_CHUNK
    nc = seq // L
    H, dk, dv = MLSTM_HEADS, MLSTM_DK, MLSTM_DV
    k_off = H * dk // dk
    v_off = 2 * H * dk // dv
    o_off = v_off + H
    return pl.pallas_call(
        functools.partial(_mlstm_body, seq=seq),
        grid=(bsz, H),
        in_specs=[
            pl.BlockSpec((None, seq, dk), lambda b, h: (b, 0, h)),
            pl.BlockSpec((None, seq, dk), lambda b, h: (b, 0, k_off + h)),
            pl.BlockSpec((None, seq, dv), lambda b, h: (b, 0, v_off + h)),
            pl.BlockSpec((None, seq, dv), lambda b, h: (b, 0, o_off + h)),
            pl.BlockSpec((None, None, nc, L, 4), lambda b, h: (b, h, 0, 0, 0)),
            pl.BlockSpec((None, None, nc, 4, L), lambda b, h: (b, h, 0, 0, 0)),
            pl.BlockSpec((None, 1, 4), lambda b, h: (h, 0, 0)),
            pl.BlockSpec((None, 4, 1), lambda b, h: (h, 0, 0)),
            pl.BlockSpec((1, dv), lambda b, h: (0, h)),
        ],
        out_specs=pl.BlockSpec((None, seq, dv), lambda b, h: (b, 0, h)),
        out_shape=jax.ShapeDtypeStruct((bsz, seq, H * dv), BF16),
        scratch_shapes=[pltpu.VMEM((seq, dv), F32), pltpu.VMEM((2, dk, dv), F32)],
        compiler_params=_params(("parallel", "arbitrary")),
        name="mlstm",
    )(u, u, u, u, gc, gr, bias_c, bias_r, norm_w.reshape(1, H * dv))


def _cumsum_rows(x, reverse):
    n = x.shape[0]
    ridx = lax.broadcasted_iota(jnp.int32, x.shape, 0)
    s = 1
    while s < n:
        if reverse:
            x = x + jnp.where(ridx < n - s, pltpu.roll(x, n - s, axis=0), 0.0)
        else:
            x = x + jnp.where(ridx >= s, pltpu.roll(x, s, axis=0), 0.0)
        s *= 2
    return x


def _hgrn_body(q_ref, ff_ref, fb_ref, i_ref, g_ref, lbl_ref, nw_ref, y_ref, os_ref, st_ref, *, seq, layer):
    L = HGRN_CHUNK
    SB = HGRN_SUB
    nc = seq // L
    os_ref[...] = jnp.zeros_like(os_ref)
    st_ref[...] = jnp.zeros_like(st_ref)

    def lower_bound(d):
        lg = lbl_ref[d]
        e = jnp.exp(lg - jnp.max(lg, axis=0, keepdims=True))
        sm = e / jnp.sum(e, axis=0, keepdims=True)
        return jnp.sum(sm[:layer + 1, :], axis=0, keepdims=True)

    lbs = (lower_bound(0), lower_bound(1))

    def one_dir(cidx, d):
        rows = pl.ds(pl.multiple_of(cidx * L, L), L)
        qr = q_ref[rows, :].astype(F32)
        q = qr * jax.nn.sigmoid(qr)
        vb = i_ref[rows, :].astype(BF16)
        fr = (ff_ref if d == 0 else fb_ref)[rows, :]
        lb = lbs[d]
        f = lb + (1.0 - lb) * jax.nn.sigmoid(fr)
        k = 1.0 - f
        lf = jnp.log(f)
        b = _cumsum_rows(lf, reverse=(d == 1))
        g = b[L - 1:L, :] if d == 0 else b[0:1, :]
        qi = (q * jnp.exp(b)).astype(BF16)
        ke = (k * jnp.exp(g - b)).astype(BF16)
        parts = []
        for jb in range(L // SB):
            lo, hi = jb * SB, (jb + 1) * SB
            bm = b[lo + SB // 2:lo + SB // 2 + 1, :]
            qm = (q[lo:hi, :] * jnp.exp(b[lo:hi, :] - bm)).astype(BF16)
            ks = slice(0, hi) if d == 0 else slice(lo, L)
            km = (k[ks, :] * jnp.exp(bm - b[ks, :])).astype(BF16)
            if km.shape[0] < L:
                pad = jnp.zeros((L - km.shape[0], HGRN_DK), BF16)
                km = jnp.concatenate([km, pad] if d == 0 else [pad, km], axis=0)
            tq = lo + lax.broadcasted_iota(jnp.int32, (SB, L), 0)
            ts = lax.broadcasted_iota(jnp.int32, (SB, L), 1)
            keep = (ts <= tq) if d == 0 else (ts >= tq)
            parts.append(jnp.where(keep, _dot_nt(qm, km), 0.0))
        a = jnp.concatenate(parts, axis=0)
        st = st_ref[d]
        o = _dot(a.astype(BF16), vb) + _dot_nt(qi, st.astype(BF16))
        st_ref[d] = st * jnp.exp(g) + _dot_tn(vb, ke)
        os_ref[rows, :] += o

    def body(i, carry):
        one_dir(i, 0)
        one_dir(nc - 1 - i, 1)
        return carry

    lax.fori_loop(0, nc, body, 0, unroll=8)

    FL = 256

    def fin(i, carry):
        rows = pl.ds(pl.multiple_of(i * FL, FL), FL)
        hh = os_ref[rows, :]
        r = lax.rsqrt(jnp.mean(hh * hh, axis=-1, keepdims=True) + RMS_EPS)
        gg = g_ref[rows, :].astype(F32)
        y_ref[rows, :] = (hh * r * nw_ref[...] * (gg * jax.nn.sigmoid(gg))).astype(y_ref.dtype)
        return carry

    lax.fori_loop(0, seq // FL, fin, 0)


def _hgrn(u, uf, lb_logits, norm_w, layer):
    bsz, seq, _ = u.shape
    H, dk = HGRN_HEADS, HGRN_DK
    base = (2 * MLSTM_HEADS * MLSTM_DK + 2 * A_WIDTH) // dk
    slots = lb_logits.shape[1]
    return pl.pallas_call(
        functools.partial(_hgrn_body, seq=seq, layer=layer),
        grid=(bsz, H),
        in_specs=[
            pl.BlockSpec((None, seq, dk), lambda b, h: (b, 0, base + h)),
            pl.BlockSpec((None, seq, dk), lambda b, h: (b, 0, h)),
            pl.BlockSpec((None, seq, dk), lambda b, h: (b, 0, H + h)),
            pl.BlockSpec((None, seq, dk), lambda b, h: (b, 0, base + H + h)),
            pl.BlockSpec((None, seq, dk), lambda b, h: (b, 0, base + 2 * H + h)),
            pl.BlockSpec((2, slots, dk), lambda b, h: (0, 0, h)),
            pl.BlockSpec((1, dk), lambda b, h: (0, h)),
        ],
        out_specs=pl.BlockSpec((None, seq, dk), lambda b, h: (b, 0, h)),
        out_shape=jax.ShapeDtypeStruct((bsz, seq, H * dk), BF16),
        scratch_shapes=[pltpu.VMEM((seq, dk), F32), pltpu.VMEM((2, dk, dk), F32)],
        compiler_params=_params(("parallel", "arbitrary")),
        name="hgrn2",
    )(u, uf, uf, u, u, lb_logits, norm_w.reshape(1, H * dk))


def _na_body(q_ref, k_ref, v_ref, tb_ref, o_ref, bias_ref, *, rows):
    W = GRID_W
    G, U, kh = NA_GROUP, NA_UNION, NA_KH
    ng = rows // G
    scale = NA_DH ** -0.5

    @pl.when(pl.program_id(1) == 0)
    def _():
        for c, (delta, offs) in enumerate(_na_group_classes(rows)):
            for i in range(G):
                qr = pl.ds(i * W, W)
                for k0 in range(0, U, 2):
                    v0 = offs[i] <= k0 < offs[i] + kh
                    v1 = k0 + 1 < U and offs[i] <= k0 + 1 < offs[i] + kh
                    dr = delta + k0 - i + kh - 1
                    if k0 + 1 >= U:
                        blk = tb_ref[1, dr][:, :W] if v0 else jnp.full((W, W), NEG_BIG, F32)
                        bias_ref[c, qr, pl.ds(k0 * W, W)] = blk
                        continue
                    if v0 and v1:
                        blk = tb_ref[0, dr]
                    elif v0:
                        blk = tb_ref[1, dr]
                    elif v1:
                        blk = tb_ref[2, dr + 1]
                    else:
                        blk = jnp.full((W, 2 * W), NEG_BIG, F32)
                    bias_ref[c, qr, pl.ds(k0 * W, 2 * W)] = blk

    def body(gi, carry):
        r0 = gi * G
        us = jnp.clip(r0 - NA_KH // 2, 0, rows - U)
        cls = jnp.where(gi == 0, 0, jnp.where(gi == ng - 1, 2, 1))
        qrows = pl.ds(pl.multiple_of(r0 * W, G * W), G * W)
        kwin = pl.ds(pl.multiple_of(us * W, W), U * W)
        s = _dot_nt(q_ref[qrows, :], k_ref[kwin, :]) * scale + bias_ref[cls]
        m = jnp.max(s, axis=-1, keepdims=True)
        p = jnp.exp(s - m)
        den = jnp.sum(p, axis=-1, keepdims=True)
        o = _dot(p.astype(BF16), v_ref[kwin, :]) / den
        o_ref[qrows, :] = o.astype(o_ref.dtype)
        return carry

    lax.fori_loop(0, ng, body, 0, unroll=16)


def _na_group_classes(rows):
    G, U, kh = NA_GROUP, NA_UNION, NA_KH

    def info(r0):
        us = min(max(r0 - kh // 2, 0), rows - U)
        return us - r0, tuple(min(max(r0 + i - kh // 2, 0), rows - kh) - us for i in range(G))

    ng = rows // G
    infos = [info(G * g) for g in range(ng)]
    classes = [infos[0], infos[1], infos[-1]]
    assert rows % G == 0 and ng >= 3 and all(infos[g] == classes[1] for g in range(1, ng - 1))
    assert all(0 <= o and o + kh <= U for c in classes for o in c[1])
    return classes


def _na_bias_table(rpb):
    W = GRID_W
    ndr, ndc = rpb.shape[1], rpb.shape[2]
    col = np.arange(W)
    col_start = np.clip(col - NA_KW // 2, 0, W - NA_KW)
    inside = (col[None, :] >= col_start[:, None]) & (col[None, :] < col_start[:, None] + NA_KW)
    dc = np.clip(col[None, :] - col[:, None] + NA_KW - 1, 0, ndc - 1)
    sel = np.zeros((ndc, W, W), np.float32)
    qq, kk = np.nonzero(inside)
    sel[dc[qq, kk], qq, kk] = 1.0
    off = np.zeros_like(sel)
    sel_left = jnp.asarray(np.concatenate([sel, off], axis=2))
    sel_right = jnp.asarray(np.concatenate([off, sel], axis=2))
    mask = np.where(inside, 0.0, NEG_BIG).astype(np.float32)
    allneg = np.full_like(mask, NEG_BIG)
    neg_pair = np.tile(np.concatenate([mask, mask], axis=1), (ndr, 1, 1))
    neg_pair[ndr - 1, :, W:] = NEG_BIG
    neg_lo = np.concatenate([mask, allneg], axis=1)
    neg_hi = np.concatenate([allneg, mask], axis=1)
    r = rpb.astype(F32)
    r_next = jnp.concatenate([r[:, 1:], jnp.zeros_like(r[:, :1])], axis=1)

    def expand(rows, sel_half):
        return jnp.einsum('hdm,mqk->hdqk', rows, sel_half, precision=HIGHEST)

    pair = expand(r, sel_left) + expand(r_next, sel_right) + neg_pair
    lo = expand(r, sel_left) + neg_lo
    hi = expand(r, sel_right) + neg_hi
    return jnp.stack([pair, lo, hi], axis=1)


def _na(qkv, bias_tbl):
    bsz, seq, _ = qkv.shape
    rows = seq // GRID_W
    H, dh = NA_HEADS, NA_DH
    return pl.pallas_call(
        functools.partial(_na_body, rows=rows),
        grid=(H, bsz),
        in_specs=[
            pl.BlockSpec((None, seq, dh), lambda h, b: (b, 0, h)),
            pl.BlockSpec((None, seq, dh), lambda h, b: (b, 0, H + h)),
            pl.BlockSpec((None, seq, dh), lambda h, b: (b, 0, 2 * H + h)),
            pl.BlockSpec((None,) + bias_tbl.shape[1:], lambda h, b: (h, 0, 0, 0, 0)),
        ],
        out_specs=pl.BlockSpec((None, seq, dh), lambda h, b: (b, 0, h)),
        out_shape=jax.ShapeDtypeStruct((bsz, seq, H * dh), BF16),
        scratch_shapes=[pltpu.VMEM((3, NA_GROUP * GRID_W, NA_UNION * GRID_W), F32)],
        compiler_params=_params(("arbitrary", "arbitrary")),
        name="natten",
    )(qkv, qkv, qkv, bias_tbl)


def _router_body(x_ref, wh_ref, wl_ref, b_ref, e_ref, g_ref):
    x = x_ref[...]
    xh = x.astype(BF16)
    xl = (x - xh.astype(F32)).astype(BF16)
    logits = _dot(xh, wh_ref[...]) + _dot(xl, wh_ref[...]) + _dot(xh, wl_ref[...]) + b_ref[...]
    lane = lax.broadcasted_iota(jnp.int32, logits.shape, 1)
    nl = logits.shape[1]
    m1 = jnp.max(logits, axis=-1, keepdims=True)
    i1 = jnp.min(jnp.where(logits == m1, lane, nl), axis=-1, keepdims=True)
    rest = jnp.where(lane == i1, -jnp.inf, logits)
    m2 = jnp.max(rest, axis=-1, keepdims=True)
    i2 = jnp.min(jnp.where(rest == m2, lane, nl), axis=-1, keepdims=True)
    ex = jnp.exp(m2 - m1)
    g1 = 1.0 / (1.0 + ex)
    g2 = ex / (1.0 + ex)
    e_ref[...] = jnp.where(lane == 0, i1, jnp.where(lane == 1, i2, 0))
    g_ref[...] = jnp.where(lane == 0, g1, jnp.where(lane == 1, g2, 0.0))


def _router(x, w_router, b_router, tm):
    n, d = x.shape
    ne = w_router.shape[1]
    w = jnp.zeros((d, V7X_LANES), F32).at[:, :ne].set(w_router.astype(F32))
    b = jnp.full((1, V7X_LANES), NEG_BIG, F32).at[0, :ne].set(b_router.astype(F32))
    wh = w.astype(BF16)
    wl = (w - wh.astype(F32)).astype(BF16)
    return pl.pallas_call(
        _router_body,
        grid=(n // tm,),
        in_specs=[
            pl.BlockSpec((tm, d), lambda i: (i, 0)),
            pl.BlockSpec((d, V7X_LANES), lambda i: (0, 0)),
            pl.BlockSpec((d, V7X_LANES), lambda i: (0, 0)),
            pl.BlockSpec((1, V7X_LANES), lambda i: (0, 0)),
        ],
        out_specs=[pl.BlockSpec((tm, V7X_LANES), lambda i: (i, 0)), pl.BlockSpec((tm, V7X_LANES), lambda i: (i, 0))],
        out_shape=[jax.ShapeDtypeStruct((n, V7X_LANES), jnp.int32), jax.ShapeDtypeStruct((n, V7X_LANES), F32)],
        compiler_params=_params(("parallel",)),
        name="router_top2",
    )(x, wh, wl, b)


def _pack_bf16_pairs(xb):
    half = xb.shape[1] // 2
    lo = lax.bitcast_convert_type(xb[:, :half], jnp.uint16).astype(jnp.uint32)
    hi = lax.bitcast_convert_type(xb[:, half:], jnp.uint16).astype(jnp.uint32)
    return lo | (hi << 16)


def _moe_body(te_ref, nv_ref, xp_ref, wg_ref, wu_ref, wd_ref, o_ref, xb_ref, wgb_ref, wub_ref, wdb_ref, *, blocks):
    t = pl.program_id(0)
    f = pl.program_id(1)
    nv = nv_ref[t]
    tm, half = xp_ref.shape
    tf = wg_ref.shape[1]
    d = o_ref.shape[1]
    nc = MXU_COLS

    @pl.when(f == 0)
    def _():
        o_ref[...] = jnp.zeros_like(o_ref)
        for start, size in blocks:
            rows = pl.ds(start, size)
            w = xp_ref[rows, :]
            xb_ref[rows, :half] = lax.bitcast_convert_type(w << 16, F32).astype(BF16)
            xb_ref[rows, half:] = lax.bitcast_convert_type(w & jnp.uint32(0xFFFF0000), F32).astype(BF16)

    for sb, (start, size) in enumerate(blocks):
        rows = pl.ds(start, size)

        @pl.when(start < nv)
        def _():
            hs = []
            for c in range(tf // nc):
                cols = pl.ds(c * nc, nc)
                if sb == 0:
                    wg = wg_ref[:, cols].astype(BF16)
                    wu = wu_ref[:, cols].astype(BF16)
                    wgb_ref[:, cols] = wg
                    wub_ref[:, cols] = wu
                else:
                    wg = wgb_ref[:, cols]
                    wu = wub_ref[:, cols]
                g = _dot(xb_ref[rows, :], wg)
                u = _dot(xb_ref[rows, :], wu)
                hs.append((g * jax.nn.sigmoid(g) * u).astype(BF16))
            h = jnp.concatenate(hs, axis=1)
            for c in range(d // nc):
                cols = pl.ds(c * nc, nc)
                if sb == 0:
                    wd = wd_ref[:, cols].astype(BF16)
                    wdb_ref[:, cols] = wd
                else:
                    wd = wdb_ref[:, cols]
                o_ref[rows, cols] += _dot(h, wd)


def _moe_experts(xp, tile_expert, tile_valid, wg, wu, wd, tm, tf, block_sizes):
    rows, half = xp.shape
    d = 2 * half
    n_tiles = rows // tm
    fdim = wg.shape[2]
    nf = fdim // tf
    assert sum(block_sizes) == tm
    blocks = tuple((sum(block_sizes[:i]), s) for i, s in enumerate(block_sizes))

    def fidx(t, f, nv):
        return jnp.where(nv[t] > 0, f, nf - 1)

    return pl.pallas_call(
        functools.partial(_moe_body, blocks=blocks),
        grid_spec=pltpu.PrefetchScalarGridSpec(
            num_scalar_prefetch=2,
            grid=(n_tiles, nf),
            in_specs=[
                pl.BlockSpec((tm, half), lambda t, f, te, nv: (t, 0), pipeline_mode=pl.Buffered(1)),
                pl.BlockSpec((None, d, tf), lambda t, f, te, nv: (te[t], 0, fidx(t, f, nv))),
                pl.BlockSpec((None, d, tf), lambda t, f, te, nv: (te[t], 0, fidx(t, f, nv))),
                pl.BlockSpec((None, tf, d), lambda t, f, te, nv: (te[t], fidx(t, f, nv), 0)),
            ],
            out_specs=pl.BlockSpec((tm, d), lambda t, f, te, nv: (t, 0), pipeline_mode=pl.Buffered(1)),
            scratch_shapes=[pltpu.VMEM((tm, d), BF16), pltpu.VMEM((d, tf), BF16), pltpu.VMEM((d, tf), BF16), pltpu.VMEM((tf, d), BF16)],
        ),
        out_shape=jax.ShapeDtypeStruct((rows, d), F32),
        compiler_params=pltpu.CompilerParams(dimension_semantics=("arbitrary", "arbitrary"), vmem_limit_bytes=MOE_VMEM_LIMIT),
        name="moe_experts",
    )(tile_expert, tile_valid, xp, wg, wu, wd)


def _combine_ln_body(x_ref, y0_ref, y1_ref, g_ref, w_ref, b_ref, p_ref, wpg_ref, wpp_ref, o_ref, ob_ref, xs_ref):
    g = g_ref[...]
    mix = y0_ref[...] * g[:, 0:1] + y1_ref[...] * g[:, 1:2]
    out = _layer_norm(ALPHA * x_ref[...] + mix, w_ref[...], b_ref[...])
    o_ref[...] = out
    xs_ref[...] = out.astype(BF16)
    _ple_epilogue(p_ref, wpg_ref, wpp_ref, xs_ref, o_ref, ob_ref)


def _combine_ln(x, y0, y1, gates, ln_w, ln_b, ple, tm):
    n, d = x.shape
    p, layer, wpg, wpp = ple
    row_spec = pl.BlockSpec((tm, d), lambda i: (i, 0))
    vec_spec = pl.BlockSpec((1, d), lambda i: (0, 0))

    def whole(arr):
        return pl.BlockSpec(arr.shape, lambda i: (0, 0), pipeline_mode=pl.Buffered(1))

    return pl.pallas_call(
        _combine_ln_body,
        grid=(n // tm,),
        in_specs=[row_spec, row_spec, row_spec, pl.BlockSpec((tm, V7X_LANES), lambda i: (i, 0)), vec_spec, vec_spec,
                  pl.BlockSpec((None, tm, p.shape[2]), lambda i: (layer, i, 0)), whole(wpg), whole(wpp)],
        out_specs=[row_spec, row_spec],
        out_shape=[jax.ShapeDtypeStruct((n, d), F32), jax.ShapeDtypeStruct((n, d), BF16)],
        scratch_shapes=[pltpu.VMEM((tm, d), BF16)],
        compiler_params=_params(("parallel",)),
        name="combine_ln",
    )(x, y0, y1, gates, ln_w.reshape(1, d), ln_b.reshape(1, d), p, wpg, wpp)


def _moe(x, xb, w_router, b_router, wg, wu, wd, ln_w, ln_b, ple):
    n, d = x.shape
    tm = MOE_TILE
    nk = n * TOP_K
    e_out, g_out = _router(x, w_router, b_router, ROUTER_TM)
    e_flat = e_out[:, :TOP_K].reshape(-1)
    onehot = (e_flat[:, None] == jnp.arange(N_EXPERTS, dtype=jnp.int32)[None, :]).astype(jnp.int32)
    csum = jnp.cumsum(onehot, axis=0)
    counts = csum[-1]
    pos = jnp.take_along_axis(csum, e_flat[:, None], axis=1)[:, 0] - 1
    padded = (counts + tm - 1) // tm * tm
    pad_end = jnp.cumsum(padded)
    pad_start = pad_end - padded
    dest = pad_start[e_flat] + pos
    n_tiles = -(-nk // tm) + N_EXPERTS
    row_tok = (jnp.arange(n_tiles * tm, dtype=jnp.int32) % n).at[dest].set(jnp.arange(nk, dtype=jnp.int32) // TOP_K, unique_indices=True)
    n_active = (pad_end[-1] // tm).astype(jnp.int32)
    tile_all = jnp.arange(n_tiles, dtype=jnp.int32)
    tile_ids = jnp.minimum(tile_all, n_active - 1)
    tile_expert = jnp.sum((pad_end[None, :] <= (tile_ids * tm)[:, None]).astype(jnp.int32), axis=1)
    tile_expert = jnp.minimum(tile_expert, N_EXPERTS - 1)
    tile_valid = jnp.clip((pad_start + counts)[tile_expert] - tile_ids * tm, 0, tm)
    tile_valid = jnp.where(tile_all < n_active, tile_valid, 0).astype(jnp.int32)
    xs = _pack_bf16_pairs(xb)[row_tok]
    ys = _moe_experts(xs, tile_expert, tile_valid, wg, wu, wd, tm, MOE_TF, MOE_BLOCKS)
    dest2 = dest.reshape(n, TOP_K)
    return _combine_ln(x, ys[dest2[:, 0]], ys[dest2[:, 1]], g_out, ln_w, ln_b, ple, COMBINE_TM)


def kernel(x, p, ln_w, ln_b, rec_w_in, mlstm_gate_bias, mlstm_norm_w, hgrn_lb_logits, hgrn_norm_w, rec_w_out, ffn_w_gate, ffn_w_up, ffn_w_down, na_w_qkv, na_rpb, na_w_out, moe_w_router, moe_b_router, moe_w_gate, moe_w_up, moe_w_down, ple_w_gate, ple_w_proj):
    bsz, seq, d = x.shape
    n = bsz * seq
    depth = ln_w.shape[0]
    assert depth == DEPTH, "ALPHA is derived from the layer count"
    xf = x.reshape(n, d).astype(F32)
    xb = None
    H = MLSTM_HEADS
    gate_lo = 2 * H * MLSTM_DK + 2 * A_WIDTH
    gate_hi = gate_lo + 4 * H
    for i in range(depth):
        j = i // 2
        ple = (p.reshape(depth, n, -1), i, ple_w_gate[i].astype(BF16), ple_w_proj[i].astype(BF16))
        if i % 2 == 0:
            w_in = rec_w_in[j]
            qb_lo, ff_lo, ib_lo = gate_hi, gate_hi + B_WIDTH, gate_hi + 3 * B_WIDTH
            w_main = jnp.concatenate([w_in[:, :gate_lo], w_in[:, qb_lo:ff_lo], w_in[:, ib_lo:]], axis=1).astype(BF16)
            w_forget = w_in[:, ff_lo:ib_lo].astype(BF16)
            w_gate = jnp.zeros((d, V7X_LANES), BF16).at[:, :4 * H].set(w_in[:, gate_lo:gate_hi].astype(BF16))
            if xb is None:
                xb, graw = _cast_and_gates(xf, w_gate, ROUTER_TM)
            else:
                graw = _matmul(xb, w_gate, F32, MM_TM, V7X_LANES)
            graw = graw[:, :4 * H].reshape(bsz, seq, 4, H)
            u = _matmul(xb, w_main, BF16, MM_TM, MM_TN).reshape(bsz, seq, -1)
            uf = _matmul(xb, w_forget, F32, MM_TM, MM_TN).reshape(bsz, seq, -1)
            L = MLSTM_CHUNK
            gc = graw.transpose(0, 3, 1, 2).reshape(bsz, H, seq // L, L, 4)
            gr = gc.transpose(0, 1, 2, 4, 3)
            bias = mlstm_gate_bias[j].astype(F32).T
            y_a = _mlstm(u, gc, gr, bias.reshape(H, 1, 4), bias.reshape(H, 4, 1), mlstm_norm_w[j].astype(F32))
            y_b = _hgrn(u, uf, hgrn_lb_logits.astype(F32), hgrn_norm_w[j].astype(F32), j)
            mix_in = (y_a.reshape(n, -1), y_b.reshape(n, -1))
            xf, xb = _matmul_ln(mix_in, rec_w_out[j].astype(BF16), xf, ln_w[i, 0], ln_b[i, 0], LN_TM)
            hid = _swiglu_up(xb, ffn_w_gate[j], ffn_w_up[j], MM_TM, SWIGLU_TF)
            xf, xb = _matmul_ln((hid,), ffn_w_down[j].astype(BF16), xf, ln_w[i, 1], ln_b[i, 1], FFN_DOWN_TM, ple=ple)
        else:
            if xb is None:
                xb = xf.astype(BF16)
            qkv = _matmul(xb, na_w_qkv[j], BF16, MM_TM, MM_TN).reshape(bsz, seq, -1)
            att = _na(qkv, _na_bias_table(na_rpb[j])).reshape(n, -1)
            xf, xb = _matmul_ln((att,), na_w_out[j].astype(BF16), xf, ln_w[i, 0], ln_b[i, 0], LN_TM)
            xf, xb = _moe(xf, xb, moe_w_router[j], moe_b_router[j], moe_w_gate[j], moe_w_up[j], moe_w_down[j], ln_w[i, 1], ln_b[i, 1], ple)
    return xf.reshape(bsz, seq, d)
```

```python
import functools

import jax
import jax.numpy as jnp
import numpy as np
from jax import lax
from jax.experimental import pallas as pl
from jax.experimental.pallas import tpu as pltpu

F32 = jnp.float32
BF16 = jnp.bfloat16
HIGHEST = lax.Precision.HIGHEST

DEPTH = 2
ALPHA = (2 * DEPTH) ** 0.25
LN_EPS = 1e-5
RMS_EPS = 1e-6
GRID_W = 64

MLSTM_HEADS = 4
MLSTM_DK = 128
MLSTM_DV = 256
GATE_CAP = 15.0
HGRN_HEADS = 8
HGRN_DK = 128
A_WIDTH = MLSTM_HEADS * MLSTM_DV
B_WIDTH = HGRN_HEADS * HGRN_DK

NA_DH = 128
NA_HEADS = 16
NA_KH = 8
NA_KW = 16

N_EXPERTS = 8
TOP_K = 2

MLSTM_CHUNK = 256
HGRN_CHUNK = 128
HGRN_SUB = 32
MOE_TILE = 1536
MOE_BLOCKS = (512, 512, 512)
MOE_TF = 512
LN_ROW_GROUP = 128
MM_TM = 1024
MM_TN = 1024
SWIGLU_TF = 512
LN_TM = 512
FFN_DOWN_TM = 256
ROUTER_TM = 512
COMBINE_TM = 512
NA_GROUP = 4
NA_UNION = NA_KH + NA_GROUP - 1

V7X_LANES = 128
MXU_COLS = 256
VMEM_LIMIT = 56 * 1024 * 1024
MOE_VMEM_LIMIT = 60 * 1024 * 1024
NEG_BIG = -1e30


def _params(sem):
    return pltpu.CompilerParams(dimension_semantics=sem, vmem_limit_bytes=VMEM_LIMIT)


def _dot(a, b, **kw):
    return jnp.dot(a, b, preferred_element_type=F32, **kw)


def _dot_nt(a, b):
    return lax.dot_general(a, b, (((1,), (1,)), ((), ())), preferred_element_type=F32)


def _dot_tn(a, b):
    return lax.dot_general(a, b, (((0,), (0,)), ((), ())), preferred_element_type=F32)


def _layer_norm(y, w, b):
    mu = jnp.mean(y, axis=-1, keepdims=True)
    yc = y - mu
    var = jnp.mean(yc * yc, axis=-1, keepdims=True)
    return yc * lax.rsqrt(var + LN_EPS) * w + b


def _log_sigmoid(z):
    return jnp.minimum(z, 0.0) - jnp.log(1.0 + jnp.exp(-jnp.abs(z)))


def _mm_body(a_ref, w_ref, o_ref, wb_ref):
    @pl.when(pl.program_id(1) == 0)
    def _():
        wb_ref[...] = w_ref[...].astype(BF16)

    o_ref[...] = _dot(a_ref[...], wb_ref[...]).astype(o_ref.dtype)


def _matmul(a, w, out_dtype, tm, tn):
    m, k = a.shape
    n = w.shape[1]
    return pl.pallas_call(
        _mm_body,
        grid=(n // tn, m // tm),
        in_specs=[pl.BlockSpec((tm, k), lambda j, i: (i, 0)), pl.BlockSpec((k, tn), lambda j, i: (0, j))],
        out_specs=pl.BlockSpec((tm, tn), lambda j, i: (i, j)),
        out_shape=jax.ShapeDtypeStruct((m, n), out_dtype),
        scratch_shapes=[pltpu.VMEM((k, tn), BF16)],
        compiler_params=_params(("arbitrary", "arbitrary")),
        name="matmul",
    )(a, w)


def _cast_gates_body(x_ref, w_ref, xb_ref, g_ref):
    xb = x_ref[...].astype(BF16)
    xb_ref[...] = xb
    g_ref[...] = _dot(xb, w_ref[...])


def _cast_and_gates(x, w_gate, tm):
    m, d = x.shape
    n = w_gate.shape[1]
    return pl.pallas_call(
        _cast_gates_body,
        grid=(m // tm,),
        in_specs=[pl.BlockSpec((tm, d), lambda i: (i, 0)), pl.BlockSpec((d, n), lambda i: (0, 0))],
        out_specs=[pl.BlockSpec((tm, d), lambda i: (i, 0)), pl.BlockSpec((tm, n), lambda i: (i, 0))],
        out_shape=[jax.ShapeDtypeStruct((m, d), BF16), jax.ShapeDtypeStruct((m, n), F32)],
        compiler_params=_params(("parallel",)),
        name="cast_gates",
    )(x, w_gate)


def _ple_epilogue(p_ref, wpg_ref, wpp_ref, xs_ref, o_ref, ob_ref, rows=slice(None)):
    for c in range(o_ref.shape[1] // MXU_COLS):
        cols = pl.ds(c * MXU_COLS, MXU_COLS)
        gate = jax.nn.sigmoid(_dot(xs_ref[rows, :], wpg_ref[:, cols]))
        proj = _dot(p_ref[rows, :].astype(BF16), wpp_ref[:, cols])
        new = o_ref[rows, cols] + gate * proj
        o_ref[rows, cols] = new
        ob_ref[rows, cols] = new.astype(BF16)


def _mm_ln_body(*refs, na, ple):
    a_refs = refs[:na]
    w_ref, res_ref, g_ref, b_ref = refs[na:na + 4]
    rest = refs[na + 4:]
    if ple:
        p_ref, wpg_ref, wpp_ref, o_ref, ob_ref, xs_ref = rest
    else:
        o_ref, ob_ref = rest
    part = LN_ROW_GROUP
    for r in range(res_ref.shape[0] // part):
        rows = pl.ds(r * part, part)
        y = ALPHA * res_ref[rows, :]
        k0 = 0
        for a_ref in a_refs:
            ka = a_ref.shape[1]
            y = y + _dot(a_ref[rows, :], w_ref[k0:k0 + ka, :])
            k0 += ka
        out = _layer_norm(y, g_ref[...], b_ref[...])
        o_ref[rows, :] = out
        (xs_ref if ple else ob_ref)[rows, :] = out.astype(BF16)
    if ple:
        _ple_epilogue(p_ref, wpg_ref, wpp_ref, xs_ref, o_ref, ob_ref)


def _matmul_ln(a_parts, w, res, ln_w, ln_b, tm, ple=None):
    m = res.shape[0]
    k, n = w.shape
    assert sum(a.shape[1] for a in a_parts) == k
    row_spec = pl.BlockSpec((tm, n), lambda i: (i, 0))
    vec_spec = pl.BlockSpec((1, n), lambda i: (0, 0))

    def whole(arr):
        return pl.BlockSpec(arr.shape, lambda i: (0, 0), pipeline_mode=pl.Buffered(1))

    in_specs = [pl.BlockSpec((tm, a.shape[1]), lambda i: (i, 0)) for a in a_parts] + [whole(w), row_spec, vec_spec, vec_spec]
    args = [*a_parts, w, res, ln_w.reshape(1, n), ln_b.reshape(1, n)]
    scratch = []
    if ple is not None:
        p, layer, wpg, wpp = ple
        in_specs += [pl.BlockSpec((None, tm, p.shape[2]), lambda i: (layer, i, 0)), whole(wpg), whole(wpp)]
        args += [p, wpg, wpp]
        scratch = [pltpu.VMEM((tm, n), BF16)]
    return pl.pallas_call(
        functools.partial(_mm_ln_body, na=len(a_parts), ple=ple is not None),
        grid=(m // tm,),
        in_specs=in_specs,
        out_specs=[row_spec, row_spec],
        out_shape=[jax.ShapeDtypeStruct((m, n), F32), jax.ShapeDtypeStruct((m, n), BF16)],
        scratch_shapes=scratch,
        compiler_params=_params(("parallel",)),
        name="matmul_ln",
    )(*args)


def _swiglu_up_body(a_ref, wg_ref, wu_ref, o_ref, wgb_ref, wub_ref):
    @pl.when(pl.program_id(1) == 0)
    def _():
        wgb_ref[...] = wg_ref[...].astype(BF16)
        wub_ref[...] = wu_ref[...].astype(BF16)

    for c in range(o_ref.shape[1] // MXU_COLS):
        cols = pl.ds(c * MXU_COLS, MXU_COLS)
        g = _dot(a_ref[...], wgb_ref[:, cols])
        u = _dot(a_ref[...], wub_ref[:, cols])
        o_ref[:, cols] = (g * jax.nn.sigmoid(g) * u).astype(o_ref.dtype)


def _swiglu_up(a, wg, wu, tm, tf):
    m, k = a.shape
    f = wg.shape[1]
    return pl.pallas_call(
        _swiglu_up_body,
        grid=(f // tf, m // tm),
        in_specs=[
            pl.BlockSpec((tm, k), lambda j, i: (i, 0)),
            pl.BlockSpec((k, tf), lambda j, i: (0, j)),
            pl.BlockSpec((k, tf), lambda j, i: (0, j)),
        ],
        out_specs=pl.BlockSpec((tm, tf), lambda j, i: (i, j)),
        out_shape=jax.ShapeDtypeStruct((m, f), BF16),
        scratch_shapes=[pltpu.VMEM((k, tf), BF16), pltpu.VMEM((k, tf), BF16)],
        compiler_params=_params(("arbitrary", "arbitrary")),
        name="swiglu_up",
    )(a, wg, wu)


def _mlstm_body(q_ref, k_ref, v_ref, og_ref, gc_ref, gr_ref, bc_ref, br_ref, nw_ref, y_ref, hs_ref, c_ref, *, seq):
    L = MLSTM_CHUNK
    nc = seq // L
    row = lax.broadcasted_iota(jnp.int32, (L, L), 0)
    col = lax.broadcasted_iota(jnp.int32, (L, L), 1)
    lower = row >= col
    upper = row <= col
    tril = lower.astype(F32)
    triu = upper.astype(F32)
    hs_ref[...] = jnp.zeros_like(hs_ref)
    c_ref[...] = jnp.zeros_like(c_ref)

    def cap(z):
        return GATE_CAP * jnp.tanh(z / GATE_CAP)

    def one_dir(cidx, d, n, m):
        rows = pl.ds(pl.multiple_of(cidx * L, L), L)
        gcol = cap(gc_ref[cidx] + bc_ref[...])
        grow = cap(gr_ref[cidx] + br_ref[...])
        lcol = _log_sigmoid(gcol)
        lrow = _log_sigmoid(grow)
        if d == 0:
            brow = _dot(lrow, triu, precision=HIGHEST)
            mask = lower
        else:
            brow = _dot(lrow, tril, precision=HIGHEST)
            mask = upper
        li_col = gcol[:, 2 * d:2 * d + 1]
        lf_wide = jnp.broadcast_to(lcol[:, 2 * d + 1:2 * d + 2], (L, V7X_LANES))
        b_col = _cumsum_rows(lf_wide, reverse=(d == 1))[:, 0:1]
        li_row = grow[2 * d:2 * d + 1, :]
        b_row = brow[2 * d + 1:2 * d + 2, :]
        g = b_col[L - 1:L, :] if d == 0 else b_col[0:1, :]

        q = q_ref[rows, :].astype(F32) * (MLSTM_DK ** -0.5)
        k = k_ref[rows, :].astype(F32)
        qb = q.astype(BF16)
        kb = k.astype(BF16)
        vb = v_ref[rows, :].astype(BF16)

        logd = jnp.where(mask, b_col + (li_row - b_row), -jnp.inf)
        m_t = jnp.maximum(jnp.max(logd, axis=1, keepdims=True), b_col + m)
        s = _dot_nt(qb, kb) * jnp.exp(logd - m_t)
        inter_w = jnp.exp(b_col + m - m_t)
        c_old = c_ref[d]
        num = _dot(s.astype(BF16), vb) + inter_w * _dot(qb, c_old.astype(BF16))
        den = jnp.sum(s, axis=1, keepdims=True) + inter_w * jnp.sum(q * n, axis=1, keepdims=True)
        h = num / jnp.maximum(jnp.abs(den), jnp.exp(-m_t))
        hs_ref[rows, :] += h

        a_col = g - b_col + li_col
        m_loc = jnp.max(a_col, axis=0, keepdims=True)
        kw = k * jnp.exp(a_col - m_loc)
        c_loc = _dot_tn(kw.astype(BF16), vb)
        n_loc = jnp.sum(kw, axis=0, keepdims=True)
        m_new = jnp.maximum(g + m, m_loc)
        decay = jnp.exp(g + m - m_new)
        inj = jnp.exp(m_loc - m_new)
        c_ref[d] = decay * c_old + inj * c_loc
        return decay * n + inj * n_loc, m_new

    def body(i, carry):
        nf, mf, nb, mb = carry
        nf, mf = one_dir(i, 0, nf, mf)
        nb, mb = one_dir(nc - 1 - i, 1, nb, mb)
        return nf, mf, nb, mb

    zn = jnp.zeros((1, MLSTM_DK), F32)
    zm = jnp.zeros((1, 1), F32)
    lax.fori_loop(0, nc, body, (zn, zm, zn, zm), unroll=4)

    def fin(i, carry):
        rows = pl.ds(pl.multiple_of(i * L, L), L)
        hh = hs_ref[rows, :]
        r = lax.rsqrt(jnp.mean(hh * hh, axis=-1, keepdims=True) + RMS_EPS)
        y_ref[rows, :] = (hh * r * nw_ref[...] * jax.nn.sigmoid(og_ref[rows, :].astype(F32))).astype(y_ref.dtype)
        return carry

    lax.fori_loop(0, nc, fin, 0)


def _mlstm(u, gc, gr, bias_c, bias_r, norm_w):
    bsz, seq, _ = u.shape
    L = MLSTM_CHUNK
    nc = seq // L
    H, dk, dv = MLSTM_HEADS, MLSTM_DK, MLSTM_DV
    k_off = H * dk // dk
    v_off = 2 * H * dk // dv
    o_off = v_off + H
    return pl.pallas_call(
        functools.partial(_mlstm_body, seq=seq),
        grid=(bsz, H),
        in_specs=[
            pl.BlockSpec((None, seq, dk), lambda b, h: (b, 0, h)),
            pl.BlockSpec((None, seq, dk), lambda b, h: (b, 0, k_off + h)),
            pl.BlockSpec((None, seq, dv), lambda b, h: (b, 0, v_off + h)),
            pl.BlockSpec((None, seq, dv), lambda b, h: (b, 0, o_off + h)),
            pl.BlockSpec((None, None, nc, L, 4), lambda b, h: (b, h, 0, 0, 0)),
            pl.BlockSpec((None, None, nc, 4, L), lambda b, h: (b, h, 0, 0, 0)),
            pl.BlockSpec((None, 1, 4), lambda b, h: (h, 0, 0)),
            pl.BlockSpec((None, 4, 1), lambda b, h: (h, 0, 0)),
            pl.BlockSpec((1, dv), lambda b, h: (0, h)),
        ],
        out_specs=pl.BlockSpec((None, seq, dv), lambda b, h: (b, 0, h)),
        out_shape=jax.ShapeDtypeStruct((bsz, seq, H * dv), BF16),
        scratch_shapes=[pltpu.VMEM((seq, dv), F32), pltpu.VMEM((2, dk, dv), F32)],
        compiler_params=_params(("parallel", "arbitrary")),
        name="mlstm",
    )(u, u, u, u, gc, gr, bias_c, bias_r, norm_w.reshape(1, H * dv))


def _cumsum_rows(x, reverse):
    n = x.shape[0]
    ridx = lax.broadcasted_iota(jnp.int32, x.shape, 0)
    s = 1
    while s < n:
        if reverse:
            x = x + jnp.where(ridx < n - s, pltpu.roll(x, n - s, axis=0), 0.0)
        else:
            x = x + jnp.where(ridx >= s, pltpu.roll(x, s, axis=0), 0.0)
        s *= 2
    return x


def _hgrn_body(q_ref, ff_ref, fb_ref, i_ref, g_ref, lbl_ref, nw_ref, y_ref, os_ref, st_ref, *, seq, layer):
    L = HGRN_CHUNK
    SB = HGRN_SUB
    nc = seq // L
    os_ref[...] = jnp.zeros_like(os_ref)
    st_ref[...] = jnp.zeros_like(st_ref)

    def lower_bound(d):
        lg = lbl_ref[d]
        e = jnp.exp(lg - jnp.max(lg, axis=0, keepdims=True))
        sm = e / jnp.sum(e, axis=0, keepdims=True)
        return jnp.sum(sm[:layer + 1, :], axis=0, keepdims=True)

    lbs = (lower_bound(0), lower_bound(1))

    def one_dir(cidx, d):
        rows = pl.ds(pl.multiple_of(cidx * L, L), L)
        qr = q_ref[rows, :].astype(F32)
        q = qr * jax.nn.sigmoid(qr)
        vb = i_ref[rows, :].astype(BF16)
        fr = (ff_ref if d == 0 else fb_ref)[rows, :]
        lb = lbs[d]
        f = lb + (1.0 - lb) * jax.nn.sigmoid(fr)
        k = 1.0 - f
        lf = jnp.log(f)
        b = _cumsum_rows(lf, reverse=(d == 1))
        g = b[L - 1:L, :] if d == 0 else b[0:1, :]
        qi = (q * jnp.exp(b)).astype(BF16)
        ke = (k * jnp.exp(g - b)).astype(BF16)
        parts = []
        for jb in range(L // SB):
            lo, hi = jb * SB, (jb + 1) * SB
            bm = b[lo + SB // 2:lo + SB // 2 + 1, :]
            qm = (q[lo:hi, :] * jnp.exp(b[lo:hi, :] - bm)).astype(BF16)
            ks = slice(0, hi) if d == 0 else slice(lo, L)
            km = (k[ks, :] * jnp.exp(bm - b[ks, :])).astype(BF16)
            if km.shape[0] < L:
                pad = jnp.zeros((L - km.shape[0], HGRN_DK), BF16)
                km = jnp.concatenate([km, pad] if d == 0 else [pad, km], axis=0)
            tq = lo + lax.broadcasted_iota(jnp.int32, (SB, L), 0)
            ts = lax.broadcasted_iota(jnp.int32, (SB, L), 1)
            keep = (ts <= tq) if d == 0 else (ts >= tq)
            parts.append(jnp.where(keep, _dot_nt(qm, km), 0.0))
        a = jnp.concatenate(parts, axis=0)
        st = st_ref[d]
        o = _dot(a.astype(BF16), vb) + _dot_nt(qi, st.astype(BF16))
        st_ref[d] = st * jnp.exp(g) + _dot_tn(vb, ke)
        os_ref[rows, :] += o

    def body(i, carry):
        one_dir(i, 0)
        one_dir(nc - 1 - i, 1)
        return carry

    lax.fori_loop(0, nc, body, 0, unroll=16)

    FL = 256

    def fin(i, carry):
        rows = pl.ds(pl.multiple_of(i * FL, FL), FL)
        hh = os_ref[rows, :]
        r = lax.rsqrt(jnp.mean(hh * hh, axis=-1, keepdims=True) + RMS_EPS)
        gg = g_ref[rows, :].astype(F32)
        y_ref[rows, :] = (hh * r * nw_ref[...] * (gg * jax.nn.sigmoid(gg))).astype(y_ref.dtype)
        return carry

    lax.fori_loop(0, seq // FL, fin, 0)


def _hgrn(u, uf, lb_logits, norm_w, layer):
    bsz, seq, _ = u.shape
    H, dk = HGRN_HEADS, HGRN_DK
    base = (2 * MLSTM_HEADS * MLSTM_DK + 2 * A_WIDTH) // dk
    slots = lb_logits.shape[1]
    return pl.pallas_call(
        functools.partial(_hgrn_body, seq=seq, layer=layer),
        grid=(bsz, H),
        in_specs=[
            pl.BlockSpec((None, seq, dk), lambda b, h: (b, 0, base + h)),
            pl.BlockSpec((None, seq, dk), lambda b, h: (b, 0, h)),
            pl.BlockSpec((None, seq, dk), lambda b, h: (b, 0, H + h)),
            pl.BlockSpec((None, seq, dk), lambda b, h: (b, 0, base + H + h)),
            pl.BlockSpec((None, seq, dk), lambda b, h: (b, 0, base + 2 * H + h)),
            pl.BlockSpec((2, slots, dk), lambda b, h: (0, 0, h)),
            pl.BlockSpec((1, dk), lambda b, h: (0, h)),
        ],
        out_specs=pl.BlockSpec((None, seq, dk), lambda b, h: (b, 0, h)),
        out_shape=jax.ShapeDtypeStruct((bsz, seq, H * dk), BF16),
        scratch_shapes=[pltpu.VMEM((seq, dk), F32), pltpu.VMEM((2, dk, dk), F32)],
        compiler_params=_params(("parallel", "arbitrary")),
        name="hgrn2",
    )(u, uf, uf, u, u, lb_logits, norm_w.reshape(1, H * dk))


def _na_body(q_ref, k_ref, v_ref, tb_ref, o_ref, bias_ref, *, rows):
    W = GRID_W
    G, U, kh = NA_GROUP, NA_UNION, NA_KH
    ng = rows // G
    scale = NA_DH ** -0.5

    @pl.when(pl.program_id(1) == 0)
    def _():
        for c, (delta, offs) in enumerate(_na_group_classes(rows)):
            for i in range(G):
                qr = pl.ds(i * W, W)
                for k0 in range(0, U, 2):
                    v0 = offs[i] <= k0 < offs[i] + kh
                    v1 = k0 + 1 < U and offs[i] <= k0 + 1 < offs[i] + kh
                    dr = delta + k0 - i + kh - 1
                    if k0 + 1 >= U:
                        blk = tb_ref[1, dr][:, :W] if v0 else jnp.full((W, W), NEG_BIG, F32)
                        bias_ref[c, qr, pl.ds(k0 * W, W)] = blk
                        continue
                    if v0 and v1:
                        blk = tb_ref[0, dr]
                    elif v0:
                        blk = tb_ref[1, dr]
                    elif v1:
                        blk = tb_ref[2, dr + 1]
                    else:
                        blk = jnp.full((W, 2 * W), NEG_BIG, F32)
                    bias_ref[c, qr, pl.ds(k0 * W, 2 * W)] = blk

    def body(gi, carry):
        r0 = gi * G
        us = jnp.clip(r0 - NA_KH // 2, 0, rows - U)
        cls = jnp.where(gi == 0, 0, jnp.where(gi == ng - 1, 2, 1))
        qrows = pl.ds(pl.multiple_of(r0 * W, G * W), G * W)
        kwin = pl.ds(pl.multiple_of(us * W, W), U * W)
        s = _dot_nt(q_ref[qrows, :], k_ref[kwin, :]) * scale + bias_ref[cls]
        m = jnp.max(s, axis=-1, keepdims=True)
        p = jnp.exp(s - m)
        den = jnp.sum(p, axis=-1, keepdims=True)
        o = _dot(p.astype(BF16), v_ref[kwin, :]) / den
        o_ref[qrows, :] = o.astype(o_ref.dtype)
        return carry

    lax.fori_loop(0, ng, body, 0, unroll=16)


def _na_group_classes(rows):
    G, U, kh = NA_GROUP, NA_UNION, NA_KH

    def info(r0):
        us = min(max(r0 - kh // 2, 0), rows - U)
        return us - r0, tuple(min(max(r0 + i - kh // 2, 0), rows - kh) - us for i in range(G))

    ng = rows // G
    infos = [info(G * g) for g in range(ng)]
    classes = [infos[0], infos[1], infos[-1]]
    assert rows % G == 0 and ng >= 3 and all(infos[g] == classes[1] for g in range(1, ng - 1))
    assert all(0 <= o and o + kh <= U for c in classes for o in c[1])
    return classes


def _na_bias_table(rpb):
    W = GRID_W
    ndr, ndc = rpb.shape[1], rpb.shape[2]
    col = np.arange(W)
    col_start = np.clip(col - NA_KW // 2, 0, W - NA_KW)
    inside = (col[None, :] >= col_start[:, None]) & (col[None, :] < col_start[:, None] + NA_KW)
    dc = np.clip(col[None, :] - col[:, None] + NA_KW - 1, 0, ndc - 1)
    sel = np.zeros((ndc, W, W), np.float32)
    qq, kk = np.nonzero(inside)
    sel[dc[qq, kk], qq, kk] = 1.0
    off = np.zeros_like(sel)
    sel_left = jnp.asarray(np.concatenate([sel, off], axis=2))
    sel_right = jnp.asarray(np.concatenate([off, sel], axis=2))
    mask = np.where(inside, 0.0, NEG_BIG).astype(np.float32)
    allneg = np.full_like(mask, NEG_BIG)
    neg_pair = np.tile(np.concatenate([mask, mask], axis=1), (ndr, 1, 1))
    neg_pair[ndr - 1, :, W:] = NEG_BIG
    neg_lo = np.concatenate([mask, allneg], axis=1)
    neg_hi = np.concatenate([allneg, mask], axis=1)
    r = rpb.astype(F32)
    r_next = jnp.concatenate([r[:, 1:], jnp.zeros_like(r[:, :1])], axis=1)

    def expand(rows, sel_half):
        return jnp.einsum('hdm,mqk->hdqk', rows, sel_half, precision=HIGHEST)

    pair = expand(r, sel_left) + expand(r_next, sel_right) + neg_pair
    lo = expand(r, sel_left) + neg_lo
    hi = expand(r, sel_right) + neg_hi
    return jnp.stack([pair, lo, hi], axis=1)


def _na(qkv, bias_tbl):
    bsz, seq, _ = qkv.shape
    rows = seq // GRID_W
    H, dh = NA_HEADS, NA_DH
    return pl.pallas_call(
        functools.partial(_na_body, rows=rows),
        grid=(H, bsz),
        in_specs=[
            pl.BlockSpec((None, seq, dh), lambda h, b: (b, 0, h)),
            pl.BlockSpec((None, seq, dh), lambda h, b: (b, 0, H + h)),
            pl.BlockSpec((None, seq, dh), lambda h, b: (b, 0, 2 * H + h)),
            pl.BlockSpec((None,) + bias_tbl.shape[1:], lambda h, b: (h, 0, 0, 0, 0)),
        ],
        out_specs=pl.BlockSpec((None, seq, dh), lambda h, b: (b, 0, h)),
        out_shape=jax.ShapeDtypeStruct((bsz, seq, H * dh), BF16),
        scratch_shapes=[pltpu.VMEM((3, NA_GROUP * GRID_W, NA_UNION * GRID_W), F32)],
        compiler_params=_params(("arbitrary", "arbitrary")),
        name="natten",
    )(qkv, qkv, qkv, bias_tbl)


def _router_body(x_ref, wh_ref, wl_ref, b_ref, e_ref, g_ref):
    x = x_ref[...]
    xh = x.astype(BF16)
    xl = (x - xh.astype(F32)).astype(BF16)
    logits = _dot(xh, wh_ref[...]) + _dot(xl, wh_ref[...]) + _dot(xh, wl_ref[...]) + b_ref[...]
    lane = lax.broadcasted_iota(jnp.int32, logits.shape, 1)
    nl = logits.shape[1]
    m1 = jnp.max(logits, axis=-1, keepdims=True)
    i1 = jnp.min(jnp.where(logits == m1, lane, nl), axis=-1, keepdims=True)
    rest = jnp.where(lane == i1, -jnp.inf, logits)
    m2 = jnp.max(rest, axis=-1, keepdims=True)
    i2 = jnp.min(jnp.where(rest == m2, lane, nl), axis=-1, keepdims=True)
    ex = jnp.exp(m2 - m1)
    g1 = 1.0 / (1.0 + ex)
    g2 = ex / (1.0 + ex)
    e_ref[...] = jnp.where(lane == 0, i1, jnp.where(lane == 1, i2, 0))
    g_ref[...] = jnp.where(lane == 0, g1, jnp.where(lane == 1, g2, 0.0))


def _router(x, w_router, b_router, tm):
    n, d = x.shape
    ne = w_router.shape[1]
    w = jnp.zeros((d, V7X_LANES), F32).at[:, :ne].set(w_router.astype(F32))
    b = jnp.full((1, V7X_LANES), NEG_BIG, F32).at[0, :ne].set(b_router.astype(F32))
    wh = w.astype(BF16)
    wl = (w - wh.astype(F32)).astype(BF16)
    return pl.pallas_call(
        _router_body,
        grid=(n // tm,),
        in_specs=[
            pl.BlockSpec((tm, d), lambda i: (i, 0)),
            pl.BlockSpec((d, V7X_LANES), lambda i: (0, 0)),
            pl.BlockSpec((d, V7X_LANES), lambda i: (0, 0)),
            pl.BlockSpec((1, V7X_LANES), lambda i: (0, 0)),
        ],
        out_specs=[pl.BlockSpec((tm, V7X_LANES), lambda i: (i, 0)), pl.BlockSpec((tm, V7X_LANES), lambda i: (i, 0))],
        out_shape=[jax.ShapeDtypeStruct((n, V7X_LANES), jnp.int32), jax.ShapeDtypeStruct((n, V7X_LANES), F32)],
        compiler_params=_params(("parallel",)),
        name="router_top2",
    )(x, wh, wl, b)


def _pack_bf16_pairs(xb):
    half = xb.shape[1] // 2
    lo = lax.bitcast_convert_type(xb[:, :half], jnp.uint16).astype(jnp.uint32)
    hi = lax.bitcast_convert_type(xb[:, half:], jnp.uint16).astype(jnp.uint32)
    return lo | (hi << 16)


def _moe_body(te_ref, nv_ref, xp_ref, wg_ref, wu_ref, wd_ref, o_ref, xb_ref, wgb_ref, wub_ref, wdb_ref, *, blocks):
    t = pl.program_id(0)
    f = pl.program_id(1)
    nv = nv_ref[t]
    tm, half = xp_ref.shape
    tf = wg_ref.shape[1]
    d = o_ref.shape[1]
    nc = MXU_COLS

    @pl.when(f == 0)
    def _():
        o_ref[...] = jnp.zeros_like(o_ref)
        for start, size in blocks:
            rows = pl.ds(start, size)
            w = xp_ref[rows, :]
            xb_ref[rows, :half] = lax.bitcast_convert_type(w << 16, F32).astype(BF16)
            xb_ref[rows, half:] = lax.bitcast_convert_type(w & jnp.uint32(0xFFFF0000), F32).astype(BF16)

    for sb, (start, size) in enumerate(blocks):
        rows = pl.ds(start, size)

        @pl.when(start < nv)
        def _():
            hs = []
            for c in range(tf // nc):
                cols = pl.ds(c * nc, nc)
                if sb == 0:
                    wg = wg_ref[:, cols].astype(BF16)
                    wu = wu_ref[:, cols].astype(BF16)
                    wgb_ref[:, cols] = wg
                    wub_ref[:, cols] = wu
                else:
                    wg = wgb_ref[:, cols]
                    wu = wub_ref[:, cols]
                g = _dot(xb_ref[rows, :], wg)
                u = _dot(xb_ref[rows, :], wu)
                hs.append((g * jax.nn.sigmoid(g) * u).astype(BF16))
            h = jnp.concatenate(hs, axis=1)
            for c in range(d // nc):
                cols = pl.ds(c * nc, nc)
                if sb == 0:
                    wd = wd_ref[:, cols].astype(BF16)
                    wdb_ref[:, cols] = wd
                else:
                    wd = wdb_ref[:, cols]
                o_ref[rows, cols] += _dot(h, wd)


def _moe_experts(xp, tile_expert, tile_valid, wg, wu, wd, tm, tf, block_sizes):
    rows, half = xp.shape
    d = 2 * half
    n_tiles = rows // tm
    fdim = wg.shape[2]
    nf = fdim // tf
    assert sum(block_sizes) == tm
    blocks = tuple((sum(block_sizes[:i]), s) for i, s in enumerate(block_sizes))

    def fidx(t, f, nv):
        return jnp.where(nv[t] > 0, f, nf - 1)

    return pl.pallas_call(
        functools.partial(_moe_body, blocks=blocks),
        grid_spec=pltpu.PrefetchScalarGridSpec(
            num_scalar_prefetch=2,
            grid=(n_tiles, nf),
            in_specs=[
                pl.BlockSpec((tm, half), lambda t, f, te, nv: (t, 0), pipeline_mode=pl.Buffered(1)),
                pl.BlockSpec((None, d, tf), lambda t, f, te, nv: (te[t], 0, fidx(t, f, nv))),
                pl.BlockSpec((None, d, tf), lambda t, f, te, nv: (te[t], 0, fidx(t, f, nv))),
                pl.BlockSpec((None, tf, d), lambda t, f, te, nv: (te[t], fidx(t, f, nv), 0)),
            ],
            out_specs=pl.BlockSpec((tm, d), lambda t, f, te, nv: (t, 0), pipeline_mode=pl.Buffered(1)),
            scratch_shapes=[pltpu.VMEM((tm, d), BF16), pltpu.VMEM((d, tf), BF16), pltpu.VMEM((d, tf), BF16), pltpu.VMEM((tf, d), BF16)],
        ),
        out_shape=jax.ShapeDtypeStruct((rows, d), F32),
        compiler_params=pltpu.CompilerParams(dimension_semantics=("arbitrary", "arbitrary"), vmem_limit_bytes=MOE_VMEM_LIMIT),
        name="moe_experts",
    )(tile_expert, tile_valid, xp, wg, wu, wd)


def _combine_ln_body(x_ref, y0_ref, y1_ref, g_ref, w_ref, b_ref, p_ref, wpg_ref, wpp_ref, o_ref, ob_ref, xs_ref):
    half = x_ref.shape[0] // 2
    for r in range(2):
        rows = pl.ds(r * half, half)
        g = g_ref[rows, :]
        mix = y0_ref[rows, :] * g[:, 0:1] + y1_ref[rows, :] * g[:, 1:2]
        out = _layer_norm(ALPHA * x_ref[rows, :] + mix, w_ref[...], b_ref[...])
        o_ref[rows, :] = out
        xs_ref[rows, :] = out.astype(BF16)
        _ple_epilogue(p_ref, wpg_ref, wpp_ref, xs_ref, o_ref, ob_ref, rows)


def _combine_ln(x, y0, y1, gates, ln_w, ln_b, ple, tm):
    n, d = x.shape
    p, layer, wpg, wpp = ple
    row_spec = pl.BlockSpec((tm, d), lambda i: (i, 0))
    vec_spec = pl.BlockSpec((1, d), lambda i: (0, 0))

    def whole(arr):
        return pl.BlockSpec(arr.shape, lambda i: (0, 0), pipeline_mode=pl.Buffered(1))

    return pl.pallas_call(
        _combine_ln_body,
        grid=(n // tm,),
        in_specs=[row_spec, row_spec, row_spec, pl.BlockSpec((tm, V7X_LANES), lambda i: (i, 0)), vec_spec, vec_spec,
                  pl.BlockSpec((None, tm, p.shape[2]), lambda i: (layer, i, 0)), whole(wpg), whole(wpp)],
        out_specs=[row_spec, row_spec],
        out_shape=[jax.ShapeDtypeStruct((n, d), F32), jax.ShapeDtypeStruct((n, d), BF16)],
        scratch_shapes=[pltpu.VMEM((tm, d), BF16)],
        compiler_params=_params(("parallel",)),
        name="combine_ln",
    )(x, y0, y1, gates, ln_w.reshape(1, d), ln_b.reshape(1, d), p, wpg, wpp)


def _moe(x, xb, w_router, b_router, wg, wu, wd, ln_w, ln_b, ple):
    n, d = x.shape
    tm = MOE_TILE
    nk = n * TOP_K
    e_out, g_out = _router(x, w_router, b_router, ROUTER_TM)
    e_flat = e_out[:, :TOP_K].reshape(-1)
    onehot = (e_flat[:, None] == jnp.arange(N_EXPERTS, dtype=jnp.int32)[None, :]).astype(jnp.int32)
    csum = jnp.cumsum(onehot, axis=0)
    counts = csum[-1]
    pos = jnp.take_along_axis(csum, e_flat[:, None], axis=1)[:, 0] - 1
    padded = (counts + tm - 1) // tm * tm
    pad_end = jnp.cumsum(padded)
    pad_start = pad_end - padded
    dest = pad_start[e_flat] + pos
    n_tiles = -(-nk // tm) + N_EXPERTS
    row_tok = (jnp.arange(n_tiles * tm, dtype=jnp.int32) % n).at[dest].set(jnp.arange(nk, dtype=jnp.int32) // TOP_K, unique_indices=True)
    n_active = (pad_end[-1] // tm).astype(jnp.int32)
    tile_all = jnp.arange(n_tiles, dtype=jnp.int32)
    tile_ids = jnp.minimum(tile_all, n_active - 1)
    tile_expert = jnp.sum((pad_end[None, :] <= (tile_ids * tm)[:, None]).astype(jnp.int32), axis=1)
    tile_expert = jnp.minimum(tile_expert, N_EXPERTS - 1)
    tile_valid = jnp.clip((pad_start + counts)[tile_expert] - tile_ids * tm, 0, tm)
    tile_valid = jnp.where(tile_all < n_active, tile_valid, 0).astype(jnp.int32)
    xs = _pack_bf16_pairs(xb)[row_tok]
    ys = _moe_experts(xs, tile_expert, tile_valid, wg, wu, wd, tm, MOE_TF, MOE_BLOCKS)
    dest2 = dest.reshape(n, TOP_K)
    return _combine_ln(x, ys[dest2[:, 0]], ys[dest2[:, 1]], g_out, ln_w, ln_b, ple, COMBINE_TM)


def kernel(x, p, ln_w, ln_b, rec_w_in, mlstm_gate_bias, mlstm_norm_w, hgrn_lb_logits, hgrn_norm_w, rec_w_out, ffn_w_gate, ffn_w_up, ffn_w_down, na_w_qkv, na_rpb, na_w_out, moe_w_router, moe_b_router, moe_w_gate, moe_w_up, moe_w_down, ple_w_gate, ple_w_proj):
    bsz, seq, d = x.shape
    n = bsz * seq
    depth = ln_w.shape[0]
    assert depth == DEPTH, "ALPHA is derived from the layer count"
    xf = x.reshape(n, d).astype(F32)
    xb = None
    H = MLSTM_HEADS
    gate_lo = 2 * H * MLSTM_DK + 2 * A_WIDTH
    gate_hi = gate_lo + 4 * H
    for i in range(depth):
        j = i // 2
        ple = (p.reshape(depth, n, -1), i, ple_w_gate[i].astype(BF16), ple_w_proj[i].astype(BF16))
        if i % 2 == 0:
            w_in = rec_w_in[j]
            qb_lo, ff_lo, ib_lo = gate_hi, gate_hi + B_WIDTH, gate_hi + 3 * B_WIDTH
            w_main = jnp.concatenate([w_in[:, :gate_lo], w_in[:, qb_lo:ff_lo], w_in[:, ib_lo:]], axis=1).astype(BF16)
            w_forget = w_in[:, ff_lo:ib_lo].astype(BF16)
            w_gate = jnp.zeros((d, V7X_LANES), BF16).at[:, :4 * H].set(w_in[:, gate_lo:gate_hi].astype(BF16))
            if xb is None:
                xb, graw = _cast_and_gates(xf, w_gate, ROUTER_TM)
            else:
                graw = _matmul(xb, w_gate, F32, MM_TM, V7X_LANES)
            graw = graw[:, :4 * H].reshape(bsz, seq, 4, H)
            u = _matmul(xb, w_main, BF16, MM_TM, MM_TN).reshape(bsz, seq, -1)
            uf = _matmul(xb, w_forget, F32, MM_TM, MM_TN).reshape(bsz, seq, -1)
            L = MLSTM_CHUNK
            gc = graw.transpose(0, 3, 1, 2).reshape(bsz, H, seq // L, L, 4)
            gr = gc.transpose(0, 1, 2, 4, 3)
            bias = mlstm_gate_bias[j].astype(F32).T
            y_a = _mlstm(u, gc, gr, bias.reshape(H, 1, 4), bias.reshape(H, 4, 1), mlstm_norm_w[j].astype(F32))
            y_b = _hgrn(u, uf, hgrn_lb_logits.astype(F32), hgrn_norm_w[j].astype(F32), j)
            mix_in = (y_a.reshape(n, -1), y_b.reshape(n, -1))
            xf, xb = _matmul_ln(mix_in, rec_w_out[j].astype(BF16), xf, ln_w[i, 0], ln_b[i, 0], LN_TM)
            hid = _swiglu_up(xb, ffn_w_gate[j], ffn_w_up[j], MM_TM, SWIGLU_TF)
            xf, xb = _matmul_ln((hid,), ffn_w_down[j].astype(BF16), xf, ln_w[i, 1], ln_b[i, 1], FFN_DOWN_TM, ple=ple)
        else:
            if xb is None:
                xb = xf.astype(BF16)
            qkv = _matmul(xb, na_w_qkv[j], BF16, MM_TM, MM_TN).reshape(bsz, seq, -1)
            att = _na(qkv, _na_bias_table(na_rpb[j])).reshape(n, -1)
            xf, xb = _matmul_ln((att,), na_w_out[j].astype(BF16), xf, ln_w[i, 0], ln_b[i, 0], LN_TM)
            xf, xb = _moe(xf, xb, moe_w_router[j], moe_b_router[j], moe_w_gate[j], moe_w_up[j], moe_w_down[j], ln_w[i, 1], ln_b[i, 1], ple)
    return xf.reshape(bsz, seq, d)
```

```python
import functools

import jax
import jax.numpy as jnp
import numpy as np
from jax import lax
from jax.experimental import pallas as pl
from jax.experimental.pallas import tpu as pltpu

F32 = jnp.float32
BF16 = jnp.bfloat16
HIGHEST = lax.Precision.HIGHEST

DEPTH = 2
ALPHA = (2 * DEPTH) ** 0.25
LN_EPS = 1e-5
RMS_EPS = 1e-6
GRID_W = 64

MLSTM_HEADS = 4
MLSTM_DK = 128
MLSTM_DV = 256
GATE_CAP = 15.0
HGRN_HEADS = 8
HGRN_DK = 128
A_WIDTH = MLSTM_HEADS * MLSTM_DV
B_WIDTH = HGRN_HEADS * HGRN_DK

NA_DH = 128
NA_HEADS = 16
NA_KH = 8
NA_KW = 16

N_EXPERTS = 8
TOP_K = 2

MLSTM_CHUNK = 256
HGRN_CHUNK = 128
HGRN_SUB = 32
MOE_TILE = 1536
MOE_BLOCKS = (512, 512, 512)
MOE_TF = 512
LN_ROW_GROUP = 128
MM_TM = 1024
MM_TN = 1024
SWIGLU_TF = 512
LN_TM = 512
FFN_DOWN_TM = 256
ROUTER_TM = 512
COMBINE_TM = 512
NA_GROUP = 4
NA_UNION = NA_KH + NA_GROUP - 1

V7X_LANES = 128
MXU_COLS = 256
VMEM_LIMIT = 56 * 1024 * 1024
MOE_VMEM_LIMIT = 60 * 1024 * 1024
NEG_BIG = -1e30


def _params(sem):
    return pltpu.CompilerParams(dimension_semantics=sem, vmem_limit_bytes=VMEM_LIMIT)


def _dot(a, b, **kw):
    return jnp.dot(a, b, preferred_element_type=F32, **kw)


def _dot_nt(a, b):
    return lax.dot_general(a, b, (((1,), (1,)), ((), ())), preferred_element_type=F32)


def _dot_tn(a, b):
    return lax.dot_general(a, b, (((0,), (0,)), ((), ())), preferred_element_type=F32)


def _layer_norm(y, w, b):
    mu = jnp.mean(y, axis=-1, keepdims=True)
    yc = y - mu
    var = jnp.mean(yc * yc, axis=-1, keepdims=True)
    return yc * lax.rsqrt(var + LN_EPS) * w + b


def _log_sigmoid(z):
    return jnp.minimum(z, 0.0) - jnp.log(1.0 + jnp.exp(-jnp.abs(z)))


def _mm_body(a_ref, w_ref, o_ref, wb_ref):
    @pl.when(pl.program_id(1) == 0)
    def _():
        wb_ref[...] = w_ref[...].astype(BF16)

    o_ref[...] = _dot(a_ref[...], wb_ref[...]).astype(o_ref.dtype)


def _matmul(a, w, out_dtype, tm, tn):
    m, k = a.shape
    n = w.shape[1]
    return pl.pallas_call(
        _mm_body,
        grid=(n // tn, m // tm),
        in_specs=[pl.BlockSpec((tm, k), lambda j, i: (i, 0)), pl.BlockSpec((k, tn), lambda j, i: (0, j))],
        out_specs=pl.BlockSpec((tm, tn), lambda j, i: (i, j)),
        out_shape=jax.ShapeDtypeStruct((m, n), out_dtype),
        scratch_shapes=[pltpu.VMEM((k, tn), BF16)],
        compiler_params=_params(("arbitrary", "arbitrary")),
        name="matmul",
    )(a, w)


def _cast_gates_body(x_ref, w_ref, xb_ref, g_ref):
    xb = x_ref[...].astype(BF16)
    xb_ref[...] = xb
    g_ref[...] = _dot(xb, w_ref[...])


def _cast_and_gates(x, w_gate, tm):
    m, d = x.shape
    n = w_gate.shape[1]
    return pl.pallas_call(
        _cast_gates_body,
        grid=(m // tm,),
        in_specs=[pl.BlockSpec((tm, d), lambda i: (i, 0)), pl.BlockSpec((d, n), lambda i: (0, 0))],
        out_specs=[pl.BlockSpec((tm, d), lambda i: (i, 0)), pl.BlockSpec((tm, n), lambda i: (i, 0))],
        out_shape=[jax.ShapeDtypeStruct((m, d), BF16), jax.ShapeDtypeStruct((m, n), F32)],
        compiler_params=_params(("parallel",)),
        name="cast_gates",
    )(x, w_gate)


def _ple_epilogue(p_ref, wpg_ref, wpp_ref, xs_ref, o_ref, ob_ref, rows=slice(None)):
    for c in range(o_ref.shape[1] // MXU_COLS):
        cols = pl.ds(c * MXU_COLS, MXU_COLS)
        gate = jax.nn.sigmoid(_dot(xs_ref[rows, :], wpg_ref[:, cols]))
        proj = _dot(p_ref[rows, :].astype(BF16), wpp_ref[:, cols])
        new = o_ref[rows, cols] + gate * proj
        o_ref[rows, cols] = new
        ob_ref[rows, cols] = new.astype(BF16)


def _mm_ln_body(*refs, na, ple):
    a_refs = refs[:na]
    w_ref, res_ref, g_ref, b_ref = refs[na:na + 4]
    rest = refs[na + 4:]
    if ple:
        p_ref, wpg_ref, wpp_ref, o_ref, ob_ref, xs_ref = rest
    else:
        o_ref, ob_ref = rest
    part = LN_ROW_GROUP
    for r in range(res_ref.shape[0] // part):
        rows = pl.ds(r * part, part)
        y = ALPHA * res_ref[rows, :]
        k0 = 0
        for a_ref in a_refs:
            ka = a_ref.shape[1]
            y = y + _dot(a_ref[rows, :], w_ref[k0:k0 + ka, :])
            k0 += ka
        out = _layer_norm(y, g_ref[...], b_ref[...])
        o_ref[rows, :] = out
        (xs_ref if ple else ob_ref)[rows, :] = out.astype(BF16)
    if ple:
        _ple_epilogue(p_ref, wpg_ref, wpp_ref, xs_ref, o_ref, ob_ref)


def _matmul_ln(a_parts, w, res, ln_w, ln_b, tm, ple=None):
    m = res.shape[0]
    k, n = w.shape
    assert sum(a.shape[1] for a in a_parts) == k
    row_spec = pl.BlockSpec((tm, n), lambda i: (i, 0))
    vec_spec = pl.BlockSpec((1, n), lambda i: (0, 0))

    def whole(arr):
        return pl.BlockSpec(arr.shape, lambda i: (0, 0), pipeline_mode=pl.Buffered(1))

    in_specs = [pl.BlockSpec((tm, a.shape[1]), lambda i: (i, 0)) for a in a_parts] + [whole(w), row_spec, vec_spec, vec_spec]
    args = [*a_parts, w, res, ln_w.reshape(1, n), ln_b.reshape(1, n)]
    scratch = []
    if ple is not None:
        p, layer, wpg, wpp = ple
        in_specs += [pl.BlockSpec((None, tm, p.shape[2]), lambda i: (layer, i, 0)), whole(wpg), whole(wpp)]
        args += [p, wpg, wpp]
        scratch = [pltpu.VMEM((tm, n), BF16)]
    return pl.pallas_call(
        functools.partial(_mm_ln_body, na=len(a_parts), ple=ple is not None),
        grid=(m // tm,),
        in_specs=in_specs,
        out_specs=[row_spec, row_spec],
        out_shape=[jax.ShapeDtypeStruct((m, n), F32), jax.ShapeDtypeStruct((m, n), BF16)],
        scratch_shapes=scratch,
        compiler_params=_params(("parallel",)),
        name="matmul_ln",
    )(*args)


def _swiglu_up_body(a_ref, wg_ref, wu_ref, o_ref, wgb_ref, wub_ref):
    @pl.when(pl.program_id(1) == 0)
    def _():
        wgb_ref[...] = wg_ref[...].astype(BF16)
        wub_ref[...] = wu_ref[...].astype(BF16)

    for c in range(o_ref.shape[1] // MXU_COLS):
        cols = pl.ds(c * MXU_COLS, MXU_COLS)
        g = _dot(a_ref[...], wgb_ref[:, cols])
        u = _dot(a_ref[...], wub_ref[:, cols])
        o_ref[:, cols] = (g * jax.nn.sigmoid(g) * u).astype(o_ref.dtype)


def _swiglu_up(a, wg, wu, tm, tf):
    m, k = a.shape
    f = wg.shape[1]
    return pl.pallas_call(
        _swiglu_up_body,
        grid=(f // tf, m // tm),
        in_specs=[
            pl.BlockSpec((tm, k), lambda j, i: (i, 0)),
            pl.BlockSpec((k, tf), lambda j, i: (0, j)),
            pl.BlockSpec((k, tf), lambda j, i: (0, j)),
        ],
        out_specs=pl.BlockSpec((tm, tf), lambda j, i: (i, j)),
        out_shape=jax.ShapeDtypeStruct((m, f), BF16),
        scratch_shapes=[pltpu.VMEM((k, tf), BF16), pltpu.VMEM((k, tf), BF16)],
        compiler_params=_params(("arbitrary", "arbitrary")),
        name="swiglu_up",
    )(a, wg, wu)


def _mlstm_body(q_ref, k_ref, v_ref, og_ref, gc_ref, gr_ref, bc_ref, br_ref, nw_ref, y_ref, hs_ref, c_ref, *, seq):
    L = MLSTM_CHUNK
    nc = seq // L
    row = lax.broadcasted_iota(jnp.int32, (L, L), 0)
    col = lax.broadcasted_iota(jnp.int32, (L, L), 1)
    lower = row >= col
    upper = row <= col
    tril = lower.astype(F32)
    triu = upper.astype(F32)
    hs_ref[...] = jnp.zeros_like(hs_ref)
    c_ref[...] = jnp.zeros_like(c_ref)

    def cap(z):
        return GATE_CAP * jnp.tanh(z / GATE_CAP)

    def one_dir(cidx, d, n, m):
        rows = pl.ds(pl.multiple_of(cidx * L, L), L)
        gcol = cap(gc_ref[cidx] + bc_ref[...])
        grow = cap(gr_ref[cidx] + br_ref[...])
        lcol = _log_sigmoid(gcol)
        lrow = _log_sigmoid(grow)
        if d == 0:
            brow = _dot(lrow, triu, precision=HIGHEST)
            mask = lower
        else:
            brow = _dot(lrow, tril, precision=HIGHEST)
            mask = upper
        li_col = gcol[:, 2 * d:2 * d + 1]
        lf_wide = jnp.broadcast_to(lcol[:, 2 * d + 1:2 * d + 2], (L, V7X_LANES))
        b_col = _cumsum_rows(lf_wide, reverse=(d == 1))[:, 0:1]
        li_row = grow[2 * d:2 * d + 1, :]
        b_row = brow[2 * d + 1:2 * d + 2, :]
        g = b_col[L - 1:L, :] if d == 0 else b_col[0:1, :]

        q = q_ref[rows, :].astype(F32) * (MLSTM_DK ** -0.5)
        k = k_ref[rows, :].astype(F32)
        qb = q.astype(BF16)
        kb = k.astype(BF16)
        vb = v_ref[rows, :].astype(BF16)

        logd = jnp.where(mask, b_col + (li_row - b_row), -jnp.inf)
        m_t = jnp.maximum(jnp.max(logd, axis=1, keepdims=True), b_col + m)
        s = _dot_nt(qb, kb) * jnp.exp(logd - m_t)
        inter_w = jnp.exp(b_col + m - m_t)
        c_old = c_ref[d]
        num = _dot(s.astype(BF16), vb) + inter_w * _dot(qb, c_old.astype(BF16))
        den = jnp.sum(s, axis=1, keepdims=True) + inter_w * jnp.sum(q * n, axis=1, keepdims=True)
        h = num / jnp.maximum(jnp.abs(den), jnp.exp(-m_t))
        hs_ref[rows, :] += h

        a_col = g - b_col + li_col
        m_loc = jnp.max(a_col, axis=0, keepdims=True)
        kw = k * jnp.exp(a_col - m_loc)
        c_loc = _dot_tn(kw.astype(BF16), vb)
        n_loc = jnp.sum(kw, axis=0, keepdims=True)
        m_new = jnp.maximum(g + m, m_loc)
        decay = jnp.exp(g + m - m_new)
        inj = jnp.exp(m_loc - m_new)
        c_ref[d] = decay * c_old + inj * c_loc
        return decay * n + inj * n_loc, m_new

    def body(i, carry):
        nf, mf, nb, mb = carry
        nf, mf = one_dir(i, 0, nf, mf)
        nb, mb = one_dir(nc - 1 - i, 1, nb, mb)
        return nf, mf, nb, mb

    zn = jnp.zeros((1, MLSTM_DK), F32)
    zm = jnp.zeros((1, 1), F32)
    lax.fori_loop(0, nc, body, (zn, zm, zn, zm), unroll=4)

    def fin(i, carry):
        rows = pl.ds(pl.multiple_of(i * L, L), L)
        hh = hs_ref[rows, :]
        r = lax.rsqrt(jnp.mean(hh * hh, axis=-1, keepdims=True) + RMS_EPS)
        y_ref[rows, :] = (hh * r * nw_ref[...] * jax.nn.sigmoid(og_ref[rows, :].astype(F32))).astype(y_ref.dtype)
        return carry

    lax.fori_loop(0, nc, fin, 0)


def _mlstm(u, gc, gr, bias_c, bias_r, norm_w):
    bsz, seq, _ = u.shape
    L = MLSTM_CHUNK
    nc = seq // L
    H, dk, dv = MLSTM_HEADS, MLSTM_DK, MLSTM_DV
    k_off = H * dk // dk
    v_off = 2 * H * dk // dv
    o_off = v_off + H
    return pl.pallas_call(
        functools.partial(_mlstm_body, seq=seq),
        grid=(bsz, H),
        in_specs=[
            pl.BlockSpec((None, seq, dk), lambda b, h: (b, 0, h)),
            pl.BlockSpec((None, seq, dk), lambda b, h: (b, 0, k_off + h)),
            pl.BlockSpec((None, seq, dv), lambda b, h: (b, 0, v_off + h)),
            pl.BlockSpec((None, seq, dv), lambda b, h: (b, 0, o_off + h)),
            pl.BlockSpec((None, None, nc, L, 4), lambda b, h: (b, h, 0, 0, 0)),
            pl.BlockSpec((None, None, nc, 4, L), lambda b, h: (b, h, 0, 0, 0)),
            pl.BlockSpec((None, 1, 4), lambda b, h: (h, 0, 0)),
            pl.BlockSpec((None, 4, 1), lambda b, h: (h, 0, 0)),
            pl.BlockSpec((1, dv), lambda b, h: (0, h)),
        ],
        out_specs=pl.BlockSpec((None, seq, dv), lambda b, h: (b, 0, h)),
        out_shape=jax.ShapeDtypeStruct((bsz, seq, H * dv), BF16),
        scratch_shapes=[pltpu.VMEM((seq, dv), F32), pltpu.VMEM((2, dk, dv), F32)],
        compiler_params=_params(("parallel", "arbitrary")),
        name="mlstm",
    )(u, u, u, u, gc, gr, bias_c, bias_r, norm_w.reshape(1, H * dv))


def _cumsum_rows(x, reverse):
    n = x.shape[0]
    ridx = lax.broadcasted_iota(jnp.int32, x.shape, 0)
    s = 1
    while s < n:
        if reverse:
            x = x + jnp.where(ridx < n - s, pltpu.roll(x, n - s, axis=0), 0.0)
        else:
            x = x + jnp.where(ridx >= s, pltpu.roll(x, s, axis=0), 0.0)
        s *= 2
    return x


def _hgrn_body(q_ref, ff_ref, fb_ref, i_ref, g_ref, lbl_ref, nw_ref, y_ref, os_ref, st_ref, *, seq, layer):
    L = HGRN_CHUNK
    SB = HGRN_SUB
    nc = seq // L
    os_ref[...] = jnp.zeros_like(os_ref)
    st_ref[...] = jnp.zeros_like(st_ref)

    def lower_bound(d):
        lg = lbl_ref[d]
        e = jnp.exp(lg - jnp.max(lg, axis=0, keepdims=True))
        sm = e / jnp.sum(e, axis=0, keepdims=True)
        return jnp.sum(sm[:layer + 1, :], axis=0, keepdims=True)

    lbs = (lower_bound(0), lower_bound(1))

    def one_dir(cidx, d):
        rows = pl.ds(pl.multiple_of(cidx * L, L), L)
        qr = q_ref[rows, :].astype(F32)
        q = qr * jax.nn.sigmoid(qr)
        vb = i_ref[rows, :].astype(BF16)
        fr = (ff_ref if d == 0 else fb_ref)[rows, :]
        lb = lbs[d]
        f = lb + (1.0 - lb) * jax.nn.sigmoid(fr)
        k = 1.0 - f
        lf = jnp.log(f)
        b = _cumsum_rows(lf, reverse=(d == 1))
        g = b[L - 1:L, :] if d == 0 else b[0:1, :]
        qi = (q * jnp.exp(b)).astype(BF16)
        ke = (k * jnp.exp(g - b)).astype(BF16)
        parts = []
        for jb in range(L // SB):
            lo, hi = jb * SB, (jb + 1) * SB
            bm = b[lo + SB // 2:lo + SB // 2 + 1, :]
            qm = (q[lo:hi, :] * jnp.exp(b[lo:hi, :] - bm)).astype(BF16)
            ks = slice(0, hi) if d == 0 else slice(lo, L)
            km = (k[ks, :] * jnp.exp(bm - b[ks, :])).astype(BF16)
            if km.shape[0] < L:
                pad = jnp.zeros((L - km.shape[0], HGRN_DK), BF16)
                km = jnp.concatenate([km, pad] if d == 0 else [pad, km], axis=0)
            tq = lo + lax.broadcasted_iota(jnp.int32, (SB, L), 0)
            ts = lax.broadcasted_iota(jnp.int32, (SB, L), 1)
            keep = (ts <= tq) if d == 0 else (ts >= tq)
            parts.append(jnp.where(keep, _dot_nt(qm, km), 0.0))
        a = jnp.concatenate(parts, axis=0)
        st = st_ref[d]
        o = _dot(a.astype(BF16), vb) + _dot_nt(qi, st.astype(BF16))
        st_ref[d] = st * jnp.exp(g) + _dot_tn(vb, ke)
        os_ref[rows, :] += o

    def body(i, carry):
        one_dir(i, 0)
        one_dir(nc - 1 - i, 1)
        return carry

    lax.fori_loop(0, nc, body, 0, unroll=16)

    FL = 256

    def fin(i, carry):
        rows = pl.ds(pl.multiple_of(i * FL, FL), FL)
        hh = os_ref[rows, :]
        r = lax.rsqrt(jnp.mean(hh * hh, axis=-1, keepdims=True) + RMS_EPS)
        gg = g_ref[rows, :].astype(F32)
        y_ref[rows, :] = (hh * r * nw_ref[...] * (gg * jax.nn.sigmoid(gg))).astype(y_ref.dtype)
        return carry

    lax.fori_loop(0, seq // FL, fin, 0)


def _hgrn(u, uf, lb_logits, norm_w, layer):
    bsz, seq, _ = u.shape
    H, dk = HGRN_HEADS, HGRN_DK
    base = (2 * MLSTM_HEADS * MLSTM_DK + 2 * A_WIDTH) // dk
    slots = lb_logits.shape[1]
    return pl.pallas_call(
        functools.partial(_hgrn_body, seq=seq, layer=layer),
        grid=(bsz, H),
        in_specs=[
            pl.BlockSpec((None, seq, dk), lambda b, h: (b, 0, base + h)),
            pl.BlockSpec((None, seq, dk), lambda b, h: (b, 0, h)),
            pl.BlockSpec((None, seq, dk), lambda b, h: (b, 0, H + h)),
            pl.BlockSpec((None, seq, dk), lambda b, h: (b, 0, base + H + h)),
            pl.BlockSpec((None, seq, dk), lambda b, h: (b, 0, base + 2 * H + h)),
            pl.BlockSpec((2, slots, dk), lambda b, h: (0, 0, h)),
            pl.BlockSpec((1, dk), lambda b, h: (0, h)),
        ],
        out_specs=pl.BlockSpec((None, seq, dk), lambda b, h: (b, 0, h)),
        out_shape=jax.ShapeDtypeStruct((bsz, seq, H * dk), BF16),
        scratch_shapes=[pltpu.VMEM((seq, dk), F32), pltpu.VMEM((2, dk, dk), F32)],
        compiler_params=_params(("parallel", "arbitrary")),
        name="hgrn2",
    )(u, uf, uf, u, u, lb_logits, norm_w.reshape(1, H * dk))


def _na_body(q_ref, k_ref, v_ref, tb_ref, o_ref, bias_ref, *, rows):
    W = GRID_W
    G, U, kh = NA_GROUP, NA_UNION, NA_KH
    ng = rows // G
    scale = NA_DH ** -0.5

    @pl.when(pl.program_id(1) == 0)
    def _():
        for c, (delta, offs) in enumerate(_na_group_classes(rows)):
            for i in range(G):
                qr = pl.ds(i * W, W)
                for k0 in range(0, U, 2):
                    v0 = offs[i] <= k0 < offs[i] + kh
                    v1 = k0 + 1 < U and offs[i] <= k0 + 1 < offs[i] + kh
                    dr = delta + k0 - i + kh - 1
                    if k0 + 1 >= U:
                        blk = tb_ref[1, dr][:, :W] if v0 else jnp.full((W, W), NEG_BIG, F32)
                        bias_ref[c, qr, pl.ds(k0 * W, W)] = blk
                        continue
                    if v0 and v1:
                        blk = tb_ref[0, dr]
                    elif v0:
                        blk = tb_ref[1, dr]
                    elif v1:
                        blk = tb_ref[2, dr + 1]
                    else:
                        blk = jnp.full((W, 2 * W), NEG_BIG, F32)
                    bias_ref[c, qr, pl.ds(k0 * W, 2 * W)] = blk

    def body(gi, carry):
        r0 = gi * G
        us = jnp.clip(r0 - NA_KH // 2, 0, rows - U)
        cls = jnp.where(gi == 0, 0, jnp.where(gi == ng - 1, 2, 1))
        qrows = pl.ds(pl.multiple_of(r0 * W, G * W), G * W)
        kwin = pl.ds(pl.multiple_of(us * W, W), U * W)
        s = _dot_nt(q_ref[qrows, :], k_ref[kwin, :]) * scale + bias_ref[cls]
        m = jnp.max(s, axis=-1, keepdims=True)
        p = jnp.exp(s - m)
        den = jnp.sum(p, axis=-1, keepdims=True)
        o = _dot(p.astype(BF16), v_ref[kwin, :]) / den
        o_ref[qrows, :] = o.astype(o_ref.dtype)
        return carry

    lax.fori_loop(0, ng, body, 0, unroll=16)


def _na_group_classes(rows):
    G, U, kh = NA_GROUP, NA_UNION, NA_KH

    def info(r0):
        us = min(max(r0 - kh // 2, 0), rows - U)
        return us - r0, tuple(min(max(r0 + i - kh // 2, 0), rows - kh) - us for i in range(G))

    ng = rows // G
    infos = [info(G * g) for g in range(ng)]
    classes = [infos[0], infos[1], infos[-1]]
    assert rows % G == 0 and ng >= 3 and all(infos[g] == classes[1] for g in range(1, ng - 1))
    assert all(0 <= o and o + kh <= U for c in classes for o in c[1])
    return classes


def _na_bias_table(rpb):
    W = GRID_W
    ndr, ndc = rpb.shape[1], rpb.shape[2]
    col = np.arange(W)
    col_start = np.clip(col - NA_KW // 2, 0, W - NA_KW)
    inside = (col[None, :] >= col_start[:, None]) & (col[None, :] < col_start[:, None] + NA_KW)
    dc = np.clip(col[None, :] - col[:, None] + NA_KW - 1, 0, ndc - 1)
    sel = np.zeros((ndc, W, W), np.float32)
    qq, kk = np.nonzero(inside)
    sel[dc[qq, kk], qq, kk] = 1.0
    off = np.zeros_like(sel)
    sel_left = jnp.asarray(np.concatenate([sel, off], axis=2))
    sel_right = jnp.asarray(np.concatenate([off, sel], axis=2))
    mask = np.where(inside, 0.0, NEG_BIG).astype(np.float32)
    allneg = np.full_like(mask, NEG_BIG)
    neg_pair = np.tile(np.concatenate([mask, mask], axis=1), (ndr, 1, 1))
    neg_pair[ndr - 1, :, W:] = NEG_BIG
    neg_lo = np.concatenate([mask, allneg], axis=1)
    neg_hi = np.concatenate([allneg, mask], axis=1)
    r = rpb.astype(F32)
    r_next = jnp.concatenate([r[:, 1:], jnp.zeros_like(r[:, :1])], axis=1)

    def expand(rows, sel_half):
        return jnp.einsum('hdm,mqk->hdqk', rows, sel_half, precision=HIGHEST)

    pair = expand(r, sel_left) + expand(r_next, sel_right) + neg_pair
    lo = expand(r, sel_left) + neg_lo
    hi = expand(r, sel_right) + neg_hi
    return jnp.stack([pair, lo, hi], axis=1)


def _na(qkv, bias_tbl):
    bsz, seq, _ = qkv.shape
    rows = seq // GRID_W
    H, dh = NA_HEADS, NA_DH
    return pl.pallas_call(
        functools.partial(_na_body, rows=rows),
        grid=(H, bsz),
        in_specs=[
            pl.BlockSpec((None, seq, dh), lambda h, b: (b, 0, h)),
            pl.BlockSpec((None, seq, dh), lambda h, b: (b, 0, H + h)),
            pl.BlockSpec((None, seq, dh), lambda h, b: (b, 0, 2 * H + h)),
            pl.BlockSpec((None,) + bias_tbl.shape[1:], lambda h, b: (h, 0, 0, 0, 0)),
        ],
        out_specs=pl.BlockSpec((None, seq, dh), lambda h, b: (b, 0, h)),
        out_shape=jax.ShapeDtypeStruct((bsz, seq, H * dh), BF16),
        scratch_shapes=[pltpu.VMEM((3, NA_GROUP * GRID_W, NA_UNION * GRID_W), F32)],
        compiler_params=_params(("arbitrary", "arbitrary")),
        name="natten",
    )(qkv, qkv, qkv, bias_tbl)


def _router_body(x_ref, wh_ref, wl_ref, b_ref, e_ref, g_ref):
    x = x_ref[...]
    xh = x.astype(BF16)
    xl = (x - xh.astype(F32)).astype(BF16)
    logits = _dot(xh, wh_ref[...]) + _dot(xl, wh_ref[...]) + _dot(xh, wl_ref[...]) + b_ref[...]
    lane = lax.broadcasted_iota(jnp.int32, logits.shape, 1)
    nl = logits.shape[1]
    m1 = jnp.max(logits, axis=-1, keepdims=True)
    i1 = jnp.min(jnp.where(logits == m1, lane, nl), axis=-1, keepdims=True)
    rest = jnp.where(lane == i1, -jnp.inf, logits)
    m2 = jnp.max(rest, axis=-1, keepdims=True)
    i2 = jnp.min(jnp.where(rest == m2, lane, nl), axis=-1, keepdims=True)
    ex = jnp.exp(m2 - m1)
    g1 = 1.0 / (1.0 + ex)
    g2 = ex / (1.0 + ex)
    e_ref[...] = jnp.where(lane == 0, i1, jnp.where(lane == 1, i2, 0))
    g_ref[...] = jnp.where(lane == 0, g1, jnp.where(lane == 1, g2, 0.0))


def _router(x, w_router, b_router, tm):
    n, d = x.shape
    ne = w_router.shape[1]
    w = jnp.zeros((d, V7X_LANES), F32).at[:, :ne].set(w_router.astype(F32))
    b = jnp.full((1, V7X_LANES), NEG_BIG, F32).at[0, :ne].set(b_router.astype(F32))
    wh = w.astype(BF16)
    wl = (w - wh.astype(F32)).astype(BF16)
    return pl.pallas_call(
        _router_body,
        grid=(n // tm,),
        in_specs=[
            pl.BlockSpec((tm, d), lambda i: (i, 0)),
            pl.BlockSpec((d, V7X_LANES), lambda i: (0, 0)),
            pl.BlockSpec((d, V7X_LANES), lambda i: (0, 0)),
            pl.BlockSpec((1, V7X_LANES), lambda i: (0, 0)),
        ],
        out_specs=[pl.BlockSpec((tm, V7X_LANES), lambda i: (i, 0)), pl.BlockSpec((tm, V7X_LANES), lambda i: (i, 0))],
        out_shape=[jax.ShapeDtypeStruct((n, V7X_LANES), jnp.int32), jax.ShapeDtypeStruct((n, V7X_LANES), F32)],
        compiler_params=_params(("parallel",)),
        name="router_top2",
    )(x, wh, wl, b)


def _pack_bf16_pairs(xb):
    half = xb.shape[1] // 2
    lo = lax.bitcast_convert_type(xb[:, :half], jnp.uint16).astype(jnp.uint32)
    hi = lax.bitcast_convert_type(xb[:, half:], jnp.uint16).astype(jnp.uint32)
    return lo | (hi << 16)


def _moe_body(te_ref, nv_ref, xp_ref, wg_ref, wu_ref, wd_ref, o_ref, xb_ref, wgb_ref, wub_ref, wdb_ref, *, blocks):
    t = pl.program_id(0)
    f = pl.program_id(1)
    nv = nv_ref[t]
    tm, half = xp_ref.shape
    tf = wg_ref.shape[1]
    d = o_ref.shape[1]
    nc = MXU_COLS

    @pl.when(f == 0)
    def _():
        o_ref[...] = jnp.zeros_like(o_ref)
        for start, size in blocks:
            rows = pl.ds(start, size)
            w = xp_ref[rows, :]
            xb_ref[rows, :half] = lax.bitcast_convert_type(w << 16, F32).astype(BF16)
            xb_ref[rows, half:] = lax.bitcast_convert_type(w & jnp.uint32(0xFFFF0000), F32).astype(BF16)

    for sb, (start, size) in enumerate(blocks):
        rows = pl.ds(start, size)

        @pl.when(start < nv)
        def _():
            hs = []
            for c in range(tf // nc):
                cols = pl.ds(c * nc, nc)
                if sb == 0:
                    wg = wg_ref[:, cols].astype(BF16)
                    wu = wu_ref[:, cols].astype(BF16)
                    wgb_ref[:, cols] = wg
                    wub_ref[:, cols] = wu
                else:
                    wg = wgb_ref[:, cols]
                    wu = wub_ref[:, cols]
                g = _dot(xb_ref[rows, :], wg)
                u = _dot(xb_ref[rows, :], wu)
                hs.append((g * jax.nn.sigmoid(g) * u).astype(BF16))
            h = jnp.concatenate(hs, axis=1)
            for c in range(d // nc):
                cols = pl.ds(c * nc, nc)
                if sb == 0:
                    wd = wd_ref[:, cols].astype(BF16)
                    wdb_ref[:, cols] = wd
                else:
                    wd = wdb_ref[:, cols]
                o_ref[rows, cols] += _dot(h, wd)


def _moe_experts(xp, tile_expert, tile_valid, wg, wu, wd, tm, tf, block_sizes):
    rows, half = xp.shape
    d = 2 * half
    n_tiles = rows // tm
    fdim = wg.shape[2]
    nf = fdim // tf
    assert sum(block_sizes) == tm
    blocks = tuple((sum(block_sizes[:i]), s) for i, s in enumerate(block_sizes))

    def fidx(t, f, nv):
        return jnp.where(nv[t] > 0, f, nf - 1)

    return pl.pallas_call(
        functools.partial(_moe_body, blocks=blocks),
        grid_spec=pltpu.PrefetchScalarGridSpec(
            num_scalar_prefetch=2,
            grid=(n_tiles, nf),
            in_specs=[
                pl.BlockSpec((tm, half), lambda t, f, te, nv: (t, 0), pipeline_mode=pl.Buffered(1)),
                pl.BlockSpec((None, d, tf), lambda t, f, te, nv: (te[t], 0, fidx(t, f, nv))),
                pl.BlockSpec((None, d, tf), lambda t, f, te, nv: (te[t], 0, fidx(t, f, nv))),
                pl.BlockSpec((None, tf, d), lambda t, f, te, nv: (te[t], fidx(t, f, nv), 0)),
            ],
            out_specs=pl.BlockSpec((tm, d), lambda t, f, te, nv: (t, 0), pipeline_mode=pl.Buffered(1)),
            scratch_shapes=[pltpu.VMEM((tm, d), BF16), pltpu.VMEM((d, tf), BF16), pltpu.VMEM((d, tf), BF16), pltpu.VMEM((tf, d), BF16)],
        ),
        out_shape=jax.ShapeDtypeStruct((rows, d), F32),
        compiler_params=pltpu.CompilerParams(dimension_semantics=("arbitrary", "arbitrary"), vmem_limit_bytes=MOE_VMEM_LIMIT),
        name="moe_experts",
    )(tile_expert, tile_valid, xp, wg, wu, wd)


def _combine_ln_body(x_ref, y0_ref, y1_ref, g_ref, w_ref, b_ref, p_ref, wpg_ref, wpp_ref, o_ref, ob_ref, xs_ref):
    half = x_ref.shape[0] // 2
    for r in range(2):
        rows = pl.ds(r * half, half)
        g = g_ref[rows, :]
        mix = y0_ref[rows, :] * g[:, 0:1] + y1_ref[rows, :] * g[:, 1:2]
        out = _layer_norm(ALPHA * x_ref[rows, :] + mix, w_ref[...], b_ref[...])
        o_ref[rows, :] = out
        xs_ref[rows, :] = out.astype(BF16)
        _ple_epilogue(p_ref, wpg_ref, wpp_ref, xs_ref, o_ref, ob_ref, rows)


def _combine_ln(x, y0, y1, gates, ln_w, ln_b, ple, tm):
    n, d = x.shape
    p, layer, wpg, wpp = ple
    row_spec = pl.BlockSpec((tm, d), lambda i: (i, 0))
    vec_spec = pl.BlockSpec((1, d), lambda i: (0, 0))

    def whole(arr):
        return pl.BlockSpec(arr.shape, lambda i: (0, 0), pipeline_mode=pl.Buffered(1))

    return pl.pallas_call(
        _combine_ln_body,
        grid=(n // tm,),
        in_specs=[row_spec, row_spec, row_spec, pl.BlockSpec((tm, V7X_LANES), lambda i: (i, 0)), vec_spec, vec_spec,
                  pl.BlockSpec((None, tm, p.shape[2]), lambda i: (layer, i, 0)), whole(wpg), whole(wpp)],
        out_specs=[row_spec, row_spec],
        out_shape=[jax.ShapeDtypeStruct((n, d), F32), jax.ShapeDtypeStruct((n, d), BF16)],
        scratch_shapes=[pltpu.VMEM((tm, d), BF16)],
        compiler_params=_params(("parallel",)),
        name="combine_ln",
    )(x, y0, y1, gates, ln_w.reshape(1, d), ln_b.reshape(1, d), p, wpg, wpp)


def _moe(x, xb, w_router, b_router, wg, wu, wd, ln_w, ln_b, ple):
    n, d = x.shape
    tm = MOE_TILE
    nk = n * TOP_K
    e_out, g_out = _router(x, w_router, b_router, ROUTER_TM)
    experts = jnp.arange(N_EXPERTS, dtype=jnp.int32)[None, :]
    hot = [(e_out[:, k:k + 1] == experts).astype(jnp.int32) for k in range(TOP_K)]
    both = sum(hot)
    csum = jnp.cumsum(both, axis=0)
    counts = csum[-1]
    earlier = csum - both
    padded = (counts + tm - 1) // tm * tm
    pad_end = jnp.cumsum(padded)
    pad_start = pad_end - padded
    dests = [jnp.sum((earlier + pad_start[None, :]) * h, axis=1) for h in hot]
    n_tiles = -(-nk // tm) + N_EXPERTS
    tok = jnp.arange(n, dtype=jnp.int32)
    row_tok = (jnp.arange(n_tiles * tm, dtype=jnp.int32) % n).at[jnp.concatenate(dests)].set(jnp.concatenate([tok] * TOP_K), unique_indices=True)
    n_active = (pad_end[-1] // tm).astype(jnp.int32)
    tile_all = jnp.arange(n_tiles, dtype=jnp.int32)
    tile_ids = jnp.minimum(tile_all, n_active - 1)
    tile_expert = jnp.sum((pad_end[None, :] <= (tile_ids * tm)[:, None]).astype(jnp.int32), axis=1)
    tile_expert = jnp.minimum(tile_expert, N_EXPERTS - 1)
    tile_valid = jnp.clip((pad_start + counts)[tile_expert] - tile_ids * tm, 0, tm)
    tile_valid = jnp.where(tile_all < n_active, tile_valid, 0).astype(jnp.int32)
    xs = _pack_bf16_pairs(xb)[row_tok]
    ys = _moe_experts(xs, tile_expert, tile_valid, wg, wu, wd, tm, MOE_TF, MOE_BLOCKS)
    return _combine_ln(x, ys[dests[0]], ys[dests[1]], g_out, ln_w, ln_b, ple, COMBINE_TM)


def kernel(x, p, ln_w, ln_b, rec_w_in, mlstm_gate_bias, mlstm_norm_w, hgrn_lb_logits, hgrn_norm_w, rec_w_out, ffn_w_gate, ffn_w_up, ffn_w_down, na_w_qkv, na_rpb, na_w_out, moe_w_router, moe_b_router, moe_w_gate, moe_w_up, moe_w_down, ple_w_gate, ple_w_proj):
    bsz, seq, d = x.shape
    n = bsz * seq
    depth = ln_w.shape[0]
    assert depth == DEPTH, "ALPHA is derived from the layer count"
    xf = x.reshape(n, d).astype(F32)
    xb = None
    H = MLSTM_HEADS
    gate_lo = 2 * H * MLSTM_DK + 2 * A_WIDTH
    gate_hi = gate_lo + 4 * H
    for i in range(depth):
        j = i // 2
        ple = (p.reshape(depth, n, -1), i, ple_w_gate[i].astype(BF16), ple_w_proj[i].astype(BF16))
        if i % 2 == 0:
            w_in = rec_w_in[j]
            qb_lo, ff_lo, ib_lo = gate_hi, gate_hi + B_WIDTH, gate_hi + 3 * B_WIDTH
            w_main = jnp.concatenate([w_in[:, :gate_lo], w_in[:, qb_lo:ff_lo], w_in[:, ib_lo:]], axis=1).astype(BF16)
            w_forget = w_in[:, ff_lo:ib_lo].astype(BF16)
            w_gate = jnp.zeros((d, V7X_LANES), BF16).at[:, :4 * H].set(w_in[:, gate_lo:gate_hi].astype(BF16))
            if xb is None:
                xb, graw = _cast_and_gates(xf, w_gate, ROUTER_TM)
            else:
                graw = _matmul(xb, w_gate, F32, MM_TM, V7X_LANES)
            graw = graw[:, :4 * H].reshape(bsz, seq, 4, H)
            u = _matmul(xb, w_main, BF16, MM_TM, MM_TN).reshape(bsz, seq, -1)
            uf = _matmul(xb, w_forget, F32, MM_TM, MM_TN).reshape(bsz, seq, -1)
            L = MLSTM_CHUNK
            gc = graw.transpose(0, 3, 1, 2).reshape(bsz, H, seq // L, L, 4)
            gr = gc.transpose(0, 1, 2, 4, 3)
            bias = mlstm_gate_bias[j].astype(F32).T
            y_a = _mlstm(u, gc, gr, bias.reshape(H, 1, 4), bias.reshape(H, 4, 1), mlstm_norm_w[j].astype(F32))
            y_b = _hgrn(u, uf, hgrn_lb_logits.astype(F32), hgrn_norm_w[j].astype(F32), j)
            mix_in = (y_a.reshape(n, -1), y_b.reshape(n, -1))
            xf, xb = _matmul_ln(mix_in, rec_w_out[j].astype(BF16), xf, ln_w[i, 0], ln_b[i, 0], LN_TM)
            hid = _swiglu_up(xb, ffn_w_gate[j], ffn_w_up[j], MM_TM, SWIGLU_TF)
            xf, xb = _matmul_ln((hid,), ffn_w_down[j].astype(BF16), xf, ln_w[i, 1], ln_b[i, 1], FFN_DOWN_TM, ple=ple)
        else:
            if xb is None:
                xb = xf.astype(BF16)
            qkv = _matmul(xb, na_w_qkv[j], BF16, MM_TM, MM_TN).reshape(bsz, seq, -1)
            att = _na(qkv, _na_bias_table(na_rpb[j])).reshape(n, -1)
            xf, xb = _matmul_ln((att,), na_w_out[j].astype(BF16), xf, ln_w[i, 0], ln_b[i, 0], LN_TM)
            xf, xb = _moe(xf, xb, moe_w_router[j], moe_b_router[j], moe_w_gate[j], moe_w_up[j], moe_w_down[j], ln_w[i, 1], ln_b[i, 1], ple)
    return xf.reshape(bsz, seq, d)
```

```python
import functools

import jax
import jax.numpy as jnp
import numpy as np
from jax import lax
from jax.experimental import pallas as pl
from jax.experimental.pallas import tpu as pltpu

F32 = jnp.float32
BF16 = jnp.bfloat16
HIGHEST = lax.Precision.HIGHEST

DEPTH = 2
ALPHA = (2 * DEPTH) ** 0.25
LN_EPS = 1e-5
RMS_EPS = 1e-6
GRID_W = 64

MLSTM_HEADS = 4
MLSTM_DK = 128
MLSTM_DV = 256
GATE_CAP = 15.0
HGRN_HEADS = 8
HGRN_DK = 128
A_WIDTH = MLSTM_HEADS * MLSTM_DV
B_WIDTH = HGRN_HEADS * HGRN_DK

NA_DH = 128
NA_HEADS = 16
NA_KH = 8
NA_KW = 16

N_EXPERTS = 8
TOP_K = 2

MLSTM_CHUNK = 256
HGRN_CHUNK = 128
HGRN_SUB = 32
MOE_TILE = 1536
MOE_BLOCKS = (512, 512, 512)
MOE_TF = 512
LN_ROW_GROUP = 128
MM_TM = 1024
MM_TN = 1024
SWIGLU_TF = 512
LN_TM = 512
FFN_DOWN_TM = 256
ROUTER_TM = 512
COMBINE_TM = 512
NA_GROUP = 4
NA_UNION = NA_KH + NA_GROUP - 1

V7X_LANES = 128
MXU_COLS = 256
VMEM_LIMIT = 56 * 1024 * 1024
MOE_VMEM_LIMIT = 60 * 1024 * 1024
NEG_BIG = -1e30


def _params(sem):
    return pltpu.CompilerParams(dimension_semantics=sem, vmem_limit_bytes=VMEM_LIMIT)


def _dot(a, b, **kw):
    return jnp.dot(a, b, preferred_element_type=F32, **kw)


def _dot_nt(a, b):
    return lax.dot_general(a, b, (((1,), (1,)), ((), ())), preferred_element_type=F32)


def _dot_tn(a, b):
    return lax.dot_general(a, b, (((0,), (0,)), ((), ())), preferred_element_type=F32)


def _layer_norm(y, w, b):
    mu = jnp.mean(y, axis=-1, keepdims=True)
    yc = y - mu
    var = jnp.mean(yc * yc, axis=-1, keepdims=True)
    return yc * lax.rsqrt(var + LN_EPS) * w + b


def _log_sigmoid(z):
    return jnp.minimum(z, 0.0) - jnp.log(1.0 + jnp.exp(-jnp.abs(z)))


def _mm_body(a_ref, w_ref, o_ref, wb_ref):
    @pl.when(pl.program_id(1) == 0)
    def _():
        wb_ref[...] = w_ref[...].astype(BF16)

    o_ref[...] = _dot(a_ref[...], wb_ref[...]).astype(o_ref.dtype)


def _matmul(a, w, out_dtype, tm, tn):
    m, k = a.shape
    n = w.shape[1]
    return pl.pallas_call(
        _mm_body,
        grid=(n // tn, m // tm),
        in_specs=[pl.BlockSpec((tm, k), lambda j, i: (i, 0)), pl.BlockSpec((k, tn), lambda j, i: (0, j))],
        out_specs=pl.BlockSpec((tm, tn), lambda j, i: (i, j)),
        out_shape=jax.ShapeDtypeStruct((m, n), out_dtype),
        scratch_shapes=[pltpu.VMEM((k, tn), BF16)],
        compiler_params=_params(("arbitrary", "arbitrary")),
        name="matmul",
    )(a, w)


def _cast_gates_body(x_ref, w_ref, xb_ref, g_ref):
    xb = x_ref[...].astype(BF16)
    xb_ref[...] = xb
    g_ref[...] = _dot(xb, w_ref[...])


def _cast_and_gates(x, w_gate, tm):
    m, d = x.shape
    n = w_gate.shape[1]
    return pl.pallas_call(
        _cast_gates_body,
        grid=(m // tm,),
        in_specs=[pl.BlockSpec((tm, d), lambda i: (i, 0)), pl.BlockSpec((d, n), lambda i: (0, 0))],
        out_specs=[pl.BlockSpec((tm, d), lambda i: (i, 0)), pl.BlockSpec((tm, n), lambda i: (i, 0))],
        out_shape=[jax.ShapeDtypeStruct((m, d), BF16), jax.ShapeDtypeStruct((m, n), F32)],
        compiler_params=_params(("parallel",)),
        name="cast_gates",
    )(x, w_gate)


def _ple_epilogue(p_ref, wpg_ref, wpp_ref, xs_ref, o_ref, ob_ref, rows=slice(None)):
    for c in range(o_ref.shape[1] // MXU_COLS):
        cols = pl.ds(c * MXU_COLS, MXU_COLS)
        gate = jax.nn.sigmoid(_dot(xs_ref[rows, :], wpg_ref[:, cols]))
        proj = _dot(p_ref[rows, :].astype(BF16), wpp_ref[:, cols])
        new = o_ref[rows, cols] + gate * proj
        o_ref[rows, cols] = new
        ob_ref[rows, cols] = new.astype(BF16)


def _mm_ln_body(*refs, na, ple):
    a_refs = refs[:na]
    w_ref, res_ref, g_ref, b_ref = refs[na:na + 4]
    rest = refs[na + 4:]
    if ple:
        p_ref, wpg_ref, wpp_ref, o_ref, ob_ref, xs_ref = rest
    else:
        o_ref, ob_ref = rest
    part = LN_ROW_GROUP
    for r in range(res_ref.shape[0] // part):
        rows = pl.ds(r * part, part)
        y = ALPHA * res_ref[rows, :]
        k0 = 0
        for a_ref in a_refs:
            ka = a_ref.shape[1]
            y = y + _dot(a_ref[rows, :], w_ref[k0:k0 + ka, :])
            k0 += ka
        out = _layer_norm(y, g_ref[...], b_ref[...])
        o_ref[rows, :] = out
        (xs_ref if ple else ob_ref)[rows, :] = out.astype(BF16)
    if ple:
        _ple_epilogue(p_ref, wpg_ref, wpp_ref, xs_ref, o_ref, ob_ref)


def _matmul_ln(a_parts, w, res, ln_w, ln_b, tm, ple=None):
    m = res.shape[0]
    k, n = w.shape
    assert sum(a.shape[1] for a in a_parts) == k
    row_spec = pl.BlockSpec((tm, n), lambda i: (i, 0))
    vec_spec = pl.BlockSpec((1, n), lambda i: (0, 0))

    def whole(arr):
        return pl.BlockSpec(arr.shape, lambda i: (0, 0), pipeline_mode=pl.Buffered(1))

    in_specs = [pl.BlockSpec((tm, a.shape[1]), lambda i: (i, 0)) for a in a_parts] + [whole(w), row_spec, vec_spec, vec_spec]
    args = [*a_parts, w, res, ln_w.reshape(1, n), ln_b.reshape(1, n)]
    scratch = []
    if ple is not None:
        p, layer, wpg, wpp = ple
        in_specs += [pl.BlockSpec((None, tm, p.shape[2]), lambda i: (layer, i, 0)), whole(wpg), whole(wpp)]
        args += [p, wpg, wpp]
        scratch = [pltpu.VMEM((tm, n), BF16)]
    return pl.pallas_call(
        functools.partial(_mm_ln_body, na=len(a_parts), ple=ple is not None),
        grid=(m // tm,),
        in_specs=in_specs,
        out_specs=[row_spec, row_spec],
        out_shape=[jax.ShapeDtypeStruct((m, n), F32), jax.ShapeDtypeStruct((m, n), BF16)],
        scratch_shapes=scratch,
        compiler_params=_params(("parallel",)),
        name="matmul_ln",
    )(*args)


def _swiglu_up_body(a_ref, wg_ref, wu_ref, o_ref, wgb_ref, wub_ref):
    @pl.when(pl.program_id(1) == 0)
    def _():
        wgb_ref[...] = wg_ref[...].astype(BF16)
        wub_ref[...] = wu_ref[...].astype(BF16)

    for c in range(o_ref.shape[1] // MXU_COLS):
        cols = pl.ds(c * MXU_COLS, MXU_COLS)
        g = _dot(a_ref[...], wgb_ref[:, cols])
        u = _dot(a_ref[...], wub_ref[:, cols])
        o_ref[:, cols] = (g * jax.nn.sigmoid(g) * u).astype(o_ref.dtype)


def _swiglu_up(a, wg, wu, tm, tf):
    m, k = a.shape
    f = wg.shape[1]
    return pl.pallas_call(
        _swiglu_up_body,
        grid=(f // tf, m // tm),
        in_specs=[
            pl.BlockSpec((tm, k), lambda j, i: (i, 0)),
            pl.BlockSpec((k, tf), lambda j, i: (0, j)),
            pl.BlockSpec((k, tf), lambda j, i: (0, j)),
        ],
        out_specs=pl.BlockSpec((tm, tf), lambda j, i: (i, j)),
        out_shape=jax.ShapeDtypeStruct((m, f), BF16),
        scratch_shapes=[pltpu.VMEM((k, tf), BF16), pltpu.VMEM((k, tf), BF16)],
        compiler_params=_params(("arbitrary", "arbitrary")),
        name="swiglu_up",
    )(a, wg, wu)


def _mlstm_body(q_ref, k_ref, v_ref, og_ref, gc_ref, gr_ref, bc_ref, br_ref, nw_ref, y_ref, hs_ref, c_ref, *, seq):
    L = MLSTM_CHUNK
    nc = seq // L
    row = lax.broadcasted_iota(jnp.int32, (L, L), 0)
    col = lax.broadcasted_iota(jnp.int32, (L, L), 1)
    lower = row >= col
    upper = row <= col
    tril = lower.astype(F32)
    triu = upper.astype(F32)
    hs_ref[...] = jnp.zeros_like(hs_ref)
    c_ref[...] = jnp.zeros_like(c_ref)

    def cap(z):
        return GATE_CAP * jnp.tanh(z / GATE_CAP)

    def one_dir(cidx, d, n, m):
        rows = pl.ds(pl.multiple_of(cidx * L, L), L)
        gcol = cap(gc_ref[cidx] + bc_ref[...])
        grow = cap(gr_ref[cidx] + br_ref[...])
        lcol = _log_sigmoid(gcol)
        lrow = _log_sigmoid(grow)
        if d == 0:
            brow = _dot(lrow, triu, precision=HIGHEST)
            mask = lower
        else:
            brow = _dot(lrow, tril, precision=HIGHEST)
            mask = upper
        li_col = gcol[:, 2 * d:2 * d + 1]
        lf_wide = jnp.broadcast_to(lcol[:, 2 * d + 1:2 * d + 2], (L, V7X_LANES))
        b_col = _cumsum_rows(lf_wide, reverse=(d == 1))[:, 0:1]
        li_row = grow[2 * d:2 * d + 1, :]
        b_row = brow[2 * d + 1:2 * d + 2, :]
        g = b_col[L - 1:L, :] if d == 0 else b_col[0:1, :]

        q = q_ref[rows, :].astype(F32) * (MLSTM_DK ** -0.5)
        k = k_ref[rows, :].astype(F32)
        qb = q.astype(BF16)
        kb = k.astype(BF16)
        vb = v_ref[rows, :].astype(BF16)

        logd = jnp.where(mask, b_col + (li_row - b_row), -jnp.inf)
        m_t = jnp.maximum(jnp.max(logd, axis=1, keepdims=True), b_col + m)
        s = _dot_nt(qb, kb) * jnp.exp(logd - m_t)
        inter_w = jnp.exp(b_col + m - m_t)
        c_old = c_ref[d]
        num = _dot(s.astype(BF16), vb) + inter_w * _dot(qb, c_old.astype(BF16))
        den = jnp.sum(s, axis=1, keepdims=True) + inter_w * jnp.sum(q * n, axis=1, keepdims=True)
        h = num / jnp.maximum(jnp.abs(den), jnp.exp(-m_t))
        hs_ref[rows, :] += h

        a_col = g - b_col + li_col
        m_loc = jnp.max(a_col, axis=0, keepdims=True)
        kw = k * jnp.exp(a_col - m_loc)
        c_loc = _dot_tn(kw.astype(BF16), vb)
        n_loc = jnp.sum(kw, axis=0, keepdims=True)
        m_new = jnp.maximum(g + m, m_loc)
        decay = jnp.exp(g + m - m_new)
        inj = jnp.exp(m_loc - m_new)
        c_ref[d] = decay * c_old + inj * c_loc
        return decay * n + inj * n_loc, m_new

    def body(i, carry):
        nf, mf, nb, mb = carry
        nf, mf = one_dir(i, 0, nf, mf)
        nb, mb = one_dir(nc - 1 - i, 1, nb, mb)
        return nf, mf, nb, mb

    zn = jnp.zeros((1, MLSTM_DK), F32)
    zm = jnp.zeros((1, 1), F32)
    lax.fori_loop(0, nc, body, (zn, zm, zn, zm), unroll=4)

    def fin(i, carry):
        rows = pl.ds(pl.multiple_of(i * L, L), L)
        hh = hs_ref[rows, :]
        r = lax.rsqrt(jnp.mean(hh * hh, axis=-1, keepdims=True) + RMS_EPS)
        y_ref[rows, :] = (hh * r * nw_ref[...] * jax.nn.sigmoid(og_ref[rows, :].astype(F32))).astype(y_ref.dtype)
        return carry

    lax.fori_loop(0, nc, fin, 0)


def _mlstm(u, gc, gr, bias_c, bias_r, norm_w):
    bsz, seq, _ = u.shape
    L = MLSTM_CHUNK
    nc = seq // L
    H, dk, dv = MLSTM_HEADS, MLSTM_DK, MLSTM_DV
    k_off = H * dk // dk
    v_off = 2 * H * dk // dv
    o_off = v_off + H
    return pl.pallas_call(
        functools.partial(_mlstm_body, seq=seq),
        grid=(bsz, H),
        in_specs=[
            pl.BlockSpec((None, seq, dk), lambda b, h: (b, 0, h)),
            pl.BlockSpec((None, seq, dk), lambda b, h: (b, 0, k_off + h)),
            pl.BlockSpec((None, seq, dv), lambda b, h: (b, 0, v_off + h)),
            pl.BlockSpec((None, seq, dv), lambda b, h: (b, 0, o_off + h)),
            pl.BlockSpec((None, None, nc, L, 4), lambda b, h: (b, h, 0, 0, 0)),
            pl.BlockSpec((None, None, nc, 4, L), lambda b, h: (b, h, 0, 0, 0)),
            pl.BlockSpec((None, 1, 4), lambda b, h: (h, 0, 0)),
            pl.BlockSpec((None, 4, 1), lambda b, h: (h, 0, 0)),
            pl.BlockSpec((1, dv), lambda b, h: (0, h)),
        ],
        out_specs=pl.BlockSpec((None, seq, dv), lambda b, h: (b, 0, h)),
        out_shape=jax.ShapeDtypeStruct((bsz, seq, H * dv), BF16),
        scratch_shapes=[pltpu.VMEM((seq, dv), F32), pltpu.VMEM((2, dk, dv), F32)],
        compiler_params=_params(("parallel", "arbitrary")),
        name="mlstm",
    )(u, u, u, u, gc, gr, bias_c, bias_r, norm_w.reshape(1, H * dv))


def _cumsum_rows(x, reverse):
    n = x.shape[0]
    ridx = lax.broadcasted_iota(jnp.int32, x.shape, 0)
    s = 1
    while s < n:
        if reverse:
            x = x + jnp.where(ridx < n - s, pltpu.roll(x, n - s, axis=0), 0.0)
        else:
            x = x + jnp.where(ridx >= s, pltpu.roll(x, s, axis=0), 0.0)
        s *= 2
    return x


def _hgrn_body(q_ref, ff_ref, fb_ref, i_ref, g_ref, lbl_ref, nw_ref, y_ref, os_ref, st_ref, *, seq, layer):
    L = HGRN_CHUNK
    SB = HGRN_SUB
    nc = seq // L
    os_ref[...] = jnp.zeros_like(os_ref)
    st_ref[...] = jnp.zeros_like(st_ref)

    def lower_bound(d):
        lg = lbl_ref[d]
        e = jnp.exp(lg - jnp.max(lg, axis=0, keepdims=True))
        sm = e / jnp.sum(e, axis=0, keepdims=True)
        return jnp.sum(sm[:layer + 1, :], axis=0, keepdims=True)

    lbs = (lower_bound(0), lower_bound(1))

    def one_dir(cidx, d):
        rows = pl.ds(pl.multiple_of(cidx * L, L), L)
        qr = q_ref[rows, :].astype(F32)
        q = qr * jax.nn.sigmoid(qr)
        vb = i_ref[rows, :].astype(BF16)
        fr = (ff_ref if d == 0 else fb_ref)[rows, :]
        lb = lbs[d]
        f = lb + (1.0 - lb) * jax.nn.sigmoid(fr)
        k = 1.0 - f
        lf = jnp.log(f)
        b = _cumsum_rows(lf, reverse=(d == 1))
        g = b[L - 1:L, :] if d == 0 else b[0:1, :]
        qi = (q * jnp.exp(b)).astype(BF16)
        ke = (k * jnp.exp(g - b)).astype(BF16)
        parts = []
        for jb in range(L // SB):
            lo, hi = jb * SB, (jb + 1) * SB
            bm = b[lo + SB // 2:lo + SB // 2 + 1, :]
            qm = (q[lo:hi, :] * jnp.exp(b[lo:hi, :] - bm)).astype(BF16)
            ks = slice(0, hi) if d == 0 else slice(lo, L)
            km = (k[ks, :] * jnp.exp(bm - b[ks, :])).astype(BF16)
            if km.shape[0] < L:
                pad = jnp.zeros((L - km.shape[0], HGRN_DK), BF16)
                km = jnp.concatenate([km, pad] if d == 0 else [pad, km], axis=0)
            tq = lo + lax.broadcasted_iota(jnp.int32, (SB, L), 0)
            ts = lax.broadcasted_iota(jnp.int32, (SB, L), 1)
            keep = (ts <= tq) if d == 0 else (ts >= tq)
            parts.append(jnp.where(keep, _dot_nt(qm, km), 0.0))
        a = jnp.concatenate(parts, axis=0)
        st = st_ref[d]
        o = _dot(a.astype(BF16), vb) + _dot_nt(qi, st.astype(BF16))
        st_ref[d] = st * jnp.exp(g) + _dot_tn(vb, ke)
        os_ref[rows, :] += o

    def body(i, carry):
        one_dir(i, 0)
        one_dir(nc - 1 - i, 1)
        return carry

    lax.fori_loop(0, nc, body, 0, unroll=16)

    FL = 256

    def fin(i, carry):
        rows = pl.ds(pl.multiple_of(i * FL, FL), FL)
        hh = os_ref[rows, :]
        r = lax.rsqrt(jnp.mean(hh * hh, axis=-1, keepdims=True) + RMS_EPS)
        gg = g_ref[rows, :].astype(F32)
        y_ref[rows, :] = (hh * r * nw_ref[...] * (gg * jax.nn.sigmoid(gg))).astype(y_ref.dtype)
        return carry

    lax.fori_loop(0, seq // FL, fin, 0)


def _hgrn(u, uf, lb_logits, norm_w, layer):
    bsz, seq, _ = u.shape
    H, dk = HGRN_HEADS, HGRN_DK
    base = (2 * MLSTM_HEADS * MLSTM_DK + 2 * A_WIDTH) // dk
    slots = lb_logits.shape[1]
    return pl.pallas_call(
        functools.partial(_hgrn_body, seq=seq, layer=layer),
        grid=(bsz, H),
        in_specs=[
            pl.BlockSpec((None, seq, dk), lambda b, h: (b, 0, base + h)),
            pl.BlockSpec((None, seq, dk), lambda b, h: (b, 0, h)),
            pl.BlockSpec((None, seq, dk), lambda b, h: (b, 0, H + h)),
            pl.BlockSpec((None, seq, dk), lambda b, h: (b, 0, base + H + h)),
            pl.BlockSpec((None, seq, dk), lambda b, h: (b, 0, base + 2 * H + h)),
            pl.BlockSpec((2, slots, dk), lambda b, h: (0, 0, h)),
            pl.BlockSpec((1, dk), lambda b, h: (0, h)),
        ],
        out_specs=pl.BlockSpec((None, seq, dk), lambda b, h: (b, 0, h)),
        out_shape=jax.ShapeDtypeStruct((bsz, seq, H * dk), BF16),
        scratch_shapes=[pltpu.VMEM((seq, dk), F32), pltpu.VMEM((2, dk, dk), F32)],
        compiler_params=_params(("parallel", "arbitrary")),
        name="hgrn2",
    )(u, uf, uf, u, u, lb_logits, norm_w.reshape(1, H * dk))


def _na_body(q_ref, k_ref, v_ref, tb_ref, o_ref, bias_ref, *, rows):
    W = GRID_W
    G, U, kh = NA_GROUP, NA_UNION, NA_KH
    ng = rows // G
    scale = NA_DH ** -0.5

    @pl.when(pl.program_id(1) == 0)
    def _():
        for c, (delta, offs) in enumerate(_na_group_classes(rows)):
            for i in range(G):
                qr = pl.ds(i * W, W)
                for k0 in range(0, U, 2):
                    v0 = offs[i] <= k0 < offs[i] + kh
                    v1 = k0 + 1 < U and offs[i] <= k0 + 1 < offs[i] + kh
                    dr = delta + k0 - i + kh - 1
                    if k0 + 1 >= U:
                        blk = tb_ref[1, dr][:, :W] if v0 else jnp.full((W, W), NEG_BIG, F32)
                        bias_ref[c, qr, pl.ds(k0 * W, W)] = blk
                        continue
                    if v0 and v1:
                        blk = tb_ref[0, dr]
                    elif v0:
                        blk = tb_ref[1, dr]
                    elif v1:
                        blk = tb_ref[2, dr + 1]
                    else:
                        blk = jnp.full((W, 2 * W), NEG_BIG, F32)
                    bias_ref[c, qr, pl.ds(k0 * W, 2 * W)] = blk

    def body(gi, carry):
        r0 = gi * G
        us = jnp.clip(r0 - NA_KH // 2, 0, rows - U)
        cls = jnp.where(gi == 0, 0, jnp.where(gi == ng - 1, 2, 1))
        qrows = pl.ds(pl.multiple_of(r0 * W, G * W), G * W)
        kwin = pl.ds(pl.multiple_of(us * W, W), U * W)
        s = _dot_nt(q_ref[qrows, :], k_ref[kwin, :]) * scale + bias_ref[cls]
        m = jnp.max(s, axis=-1, keepdims=True)
        p = jnp.exp(s - m)
        den = jnp.sum(p, axis=-1, keepdims=True)
        o = _dot(p.astype(BF16), v_ref[kwin, :]) / den
        o_ref[qrows, :] = o.astype(o_ref.dtype)
        return carry

    lax.fori_loop(0, ng, body, 0, unroll=16)


def _na_group_classes(rows):
    G, U, kh = NA_GROUP, NA_UNION, NA_KH

    def info(r0):
        us = min(max(r0 - kh // 2, 0), rows - U)
        return us - r0, tuple(min(max(r0 + i - kh // 2, 0), rows - kh) - us for i in range(G))

    ng = rows // G
    infos = [info(G * g) for g in range(ng)]
    classes = [infos[0], infos[1], infos[-1]]
    assert rows % G == 0 and ng >= 3 and all(infos[g] == classes[1] for g in range(1, ng - 1))
    assert all(0 <= o and o + kh <= U for c in classes for o in c[1])
    return classes


def _na_bias_table(rpb):
    W = GRID_W
    ndr, ndc = rpb.shape[1], rpb.shape[2]
    col = np.arange(W)
    col_start = np.clip(col - NA_KW // 2, 0, W - NA_KW)
    inside = (col[None, :] >= col_start[:, None]) & (col[None, :] < col_start[:, None] + NA_KW)
    dc = np.clip(col[None, :] - col[:, None] + NA_KW - 1, 0, ndc - 1)
    sel = np.zeros((ndc, W, W), np.float32)
    qq, kk = np.nonzero(inside)
    sel[dc[qq, kk], qq, kk] = 1.0
    off = np.zeros_like(sel)
    sel_left = jnp.asarray(np.concatenate([sel, off], axis=2))
    sel_right = jnp.asarray(np.concatenate([off, sel], axis=2))
    mask = np.where(inside, 0.0, NEG_BIG).astype(np.float32)
    allneg = np.full_like(mask, NEG_BIG)
    neg_pair = np.tile(np.concatenate([mask, mask], axis=1), (ndr, 1, 1))
    neg_pair[ndr - 1, :, W:] = NEG_BIG
    neg_lo = np.concatenate([mask, allneg], axis=1)
    neg_hi = np.concatenate([allneg, mask], axis=1)
    r = rpb.astype(F32)
    r_next = jnp.concatenate([r[:, 1:], jnp.zeros_like(r[:, :1])], axis=1)

    def expand(rows, sel_half):
        return jnp.einsum('hdm,mqk->hdqk', rows, sel_half, precision=HIGHEST)

    pair = expand(r, sel_left) + expand(r_next, sel_right) + neg_pair
    lo = expand(r, sel_left) + neg_lo
    hi = expand(r, sel_right) + neg_hi
    return jnp.stack([pair, lo, hi], axis=1)


def _na(qkv, bias_tbl):
    bsz, seq, _ = qkv.shape
    rows = seq // GRID_W
    H, dh = NA_HEADS, NA_DH
    return pl.pallas_call(
        functools.partial(_na_body, rows=rows),
        grid=(H, bsz),
        in_specs=[
            pl.BlockSpec((None, seq, dh), lambda h, b: (b, 0, h)),
            pl.BlockSpec((None, seq, dh), lambda h, b: (b, 0, H + h)),
            pl.BlockSpec((None, seq, dh), lambda h, b: (b, 0, 2 * H + h)),
            pl.BlockSpec((None,) + bias_tbl.shape[1:], lambda h, b: (h, 0, 0, 0, 0)),
        ],
        out_specs=pl.BlockSpec((None, seq, dh), lambda h, b: (b, 0, h)),
        out_shape=jax.ShapeDtypeStruct((bsz, seq, H * dh), BF16),
        scratch_shapes=[pltpu.VMEM((3, NA_GROUP * GRID_W, NA_UNION * GRID_W), F32)],
        compiler_params=_params(("arbitrary", "arbitrary")),
        name="natten",
    )(qkv, qkv, qkv, bias_tbl)


def _router_body(x_ref, wh_ref, wl_ref, b_ref, e_ref, g_ref):
    x = x_ref[...]
    xh = x.astype(BF16)
    xl = (x - xh.astype(F32)).astype(BF16)
    logits = _dot(xh, wh_ref[...]) + _dot(xl, wh_ref[...]) + _dot(xh, wl_ref[...]) + b_ref[...]
    lane = lax.broadcasted_iota(jnp.int32, logits.shape, 1)
    nl = logits.shape[1]
    m1 = jnp.max(logits, axis=-1, keepdims=True)
    i1 = jnp.min(jnp.where(logits == m1, lane, nl), axis=-1, keepdims=True)
    rest = jnp.where(lane == i1, -jnp.inf, logits)
    m2 = jnp.max(rest, axis=-1, keepdims=True)
    i2 = jnp.min(jnp.where(rest == m2, lane, nl), axis=-1, keepdims=True)
    ex = jnp.exp(m2 - m1)
    g1 = 1.0 / (1.0 + ex)
    g2 = ex / (1.0 + ex)
    e_ref[...] = jnp.where(lane == 0, i1, jnp.where(lane == 1, i2, 0))
    g_ref[...] = jnp.where(lane == 0, g1, jnp.where(lane == 1, g2, 0.0))


def _router(x, w_router, b_router, tm):
    n, d = x.shape
    ne = w_router.shape[1]
    w = jnp.zeros((d, V7X_LANES), F32).at[:, :ne].set(w_router.astype(F32))
    b = jnp.full((1, V7X_LANES), NEG_BIG, F32).at[0, :ne].set(b_router.astype(F32))
    wh = w.astype(BF16)
    wl = (w - wh.astype(F32)).astype(BF16)
    return pl.pallas_call(
        _router_body,
        grid=(n // tm,),
        in_specs=[
            pl.BlockSpec((tm, d), lambda i: (i, 0)),
            pl.BlockSpec((d, V7X_LANES), lambda i: (0, 0)),
            pl.BlockSpec((d, V7X_LANES), lambda i: (0, 0)),
            pl.BlockSpec((1, V7X_LANES), lambda i: (0, 0)),
        ],
        out_specs=[pl.BlockSpec((tm, V7X_LANES), lambda i: (i, 0)), pl.BlockSpec((tm, V7X_LANES), lambda i: (i, 0))],
        out_shape=[jax.ShapeDtypeStruct((n, V7X_LANES), jnp.int32), jax.ShapeDtypeStruct((n, V7X_LANES), F32)],
        compiler_params=_params(("parallel",)),
        name="router_top2",
    )(x, wh, wl, b)


def _pack_bf16_pairs(xb):
    half = xb.shape[1] // 2
    lo = lax.bitcast_convert_type(xb[:, :half], jnp.uint16).astype(jnp.uint32)
    hi = lax.bitcast_convert_type(xb[:, half:], jnp.uint16).astype(jnp.uint32)
    return lo | (hi << 16)


def _moe_body(te_ref, nv_ref, xp_ref, wg_ref, wu_ref, wd_ref, o_ref, xb_ref, wgb_ref, wub_ref, wdb_ref, *, blocks):
    t = pl.program_id(0)
    f = pl.program_id(1)
    nv = nv_ref[t]
    tm, half = xp_ref.shape
    tf = wg_ref.shape[1]
    d = o_ref.shape[1]
    nc = MXU_COLS

    @pl.when(f == 0)
    def _():
        o_ref[...] = jnp.zeros_like(o_ref)
        for start, size in blocks:
            rows = pl.ds(start, size)
            w = xp_ref[rows, :]
            xb_ref[rows, :half] = lax.bitcast_convert_type(w << 16, F32).astype(BF16)
            xb_ref[rows, half:] = lax.bitcast_convert_type(w & jnp.uint32(0xFFFF0000), F32).astype(BF16)

    for sb, (start, size) in enumerate(blocks):
        rows = pl.ds(start, size)

        @pl.when(start < nv)
        def _():
            hs = []
            for c in range(tf // nc):
                cols = pl.ds(c * nc, nc)
                if sb == 0:
                    wg = wg_ref[:, cols].astype(BF16)
                    wu = wu_ref[:, cols].astype(BF16)
                    wgb_ref[:, cols] = wg
                    wub_ref[:, cols] = wu
                else:
                    wg = wgb_ref[:, cols]
                    wu = wub_ref[:, cols]
                g = _dot(xb_ref[rows, :], wg)
                u = _dot(xb_ref[rows, :], wu)
                hs.append((g * jax.nn.sigmoid(g) * u).astype(BF16))
            h = jnp.concatenate(hs, axis=1)
            for c in range(d // nc):
                cols = pl.ds(c * nc, nc)
                if sb == 0:
                    wd = wd_ref[:, cols].astype(BF16)
                    wdb_ref[:, cols] = wd
                else:
                    wd = wdb_ref[:, cols]
                o_ref[rows, cols] += _dot(h, wd)


def _moe_experts(xp, tile_expert, tile_valid, wg, wu, wd, tm, tf, block_sizes):
    rows, half = xp.shape
    d = 2 * half
    n_tiles = rows // tm
    fdim = wg.shape[2]
    nf = fdim // tf
    assert sum(block_sizes) == tm
    blocks = tuple((sum(block_sizes[:i]), s) for i, s in enumerate(block_sizes))

    def fidx(t, f, nv):
        return jnp.where(nv[t] > 0, f, nf - 1)

    return pl.pallas_call(
        functools.partial(_moe_body, blocks=blocks),
        grid_spec=pltpu.PrefetchScalarGridSpec(
            num_scalar_prefetch=2,
            grid=(n_tiles, nf),
            in_specs=[
                pl.BlockSpec((tm, half), lambda t, f, te, nv: (t, 0), pipeline_mode=pl.Buffered(1)),
                pl.BlockSpec((None, d, tf), lambda t, f, te, nv: (te[t], 0, fidx(t, f, nv))),
                pl.BlockSpec((None, d, tf), lambda t, f, te, nv: (te[t], 0, fidx(t, f, nv))),
                pl.BlockSpec((None, tf, d), lambda t, f, te, nv: (te[t], fidx(t, f, nv), 0)),
            ],
            out_specs=pl.BlockSpec((tm, d), lambda t, f, te, nv: (t, 0), pipeline_mode=pl.Buffered(1)),
            scratch_shapes=[pltpu.VMEM((tm, d), BF16), pltpu.VMEM((d, tf), BF16), pltpu.VMEM((d, tf), BF16), pltpu.VMEM((tf, d), BF16)],
        ),
        out_shape=jax.ShapeDtypeStruct((rows, d), F32),
        compiler_params=pltpu.CompilerParams(dimension_semantics=("arbitrary", "arbitrary"), vmem_limit_bytes=MOE_VMEM_LIMIT),
        name="moe_experts",
    )(tile_expert, tile_valid, xp, wg, wu, wd)


def _combine_ln_body(x_ref, y0_ref, y1_ref, g_ref, w_ref, b_ref, p_ref, wpg_ref, wpp_ref, o_ref, ob_ref, xs_ref):
    half = x_ref.shape[0] // 2
    for r in range(2):
        rows = pl.ds(r * half, half)
        g = g_ref[rows, :]
        mix = y0_ref[rows, :] * g[:, 0:1] + y1_ref[rows, :] * g[:, 1:2]
        out = _layer_norm(ALPHA * x_ref[rows, :] + mix, w_ref[...], b_ref[...])
        o_ref[rows, :] = out
        xs_ref[rows, :] = out.astype(BF16)
        _ple_epilogue(p_ref, wpg_ref, wpp_ref, xs_ref, o_ref, ob_ref, rows)


def _combine_ln(x, y0, y1, gates, ln_w, ln_b, ple, tm):
    n, d = x.shape
    p, layer, wpg, wpp = ple
    row_spec = pl.BlockSpec((tm, d), lambda i: (i, 0))
    vec_spec = pl.BlockSpec((1, d), lambda i: (0, 0))

    def whole(arr):
        return pl.BlockSpec(arr.shape, lambda i: (0, 0), pipeline_mode=pl.Buffered(1))

    return pl.pallas_call(
        _combine_ln_body,
        grid=(n // tm,),
        in_specs=[row_spec, row_spec, row_spec, pl.BlockSpec((tm, V7X_LANES), lambda i: (i, 0)), vec_spec, vec_spec,
                  pl.BlockSpec((None, tm, p.shape[2]), lambda i: (layer, i, 0)), whole(wpg), whole(wpp)],
        out_specs=[row_spec, row_spec],
        out_shape=[jax.ShapeDtypeStruct((n, d), F32), jax.ShapeDtypeStruct((n, d), BF16)],
        scratch_shapes=[pltpu.VMEM((tm, d), BF16)],
        compiler_params=_params(("parallel",)),
        name="combine_ln",
    )(x, y0, y1, gates, ln_w.reshape(1, d), ln_b.reshape(1, d), p, wpg, wpp)


def _moe(x, xb, w_router, b_router, wg, wu, wd, ln_w, ln_b, ple):
    n, d = x.shape
    tm = MOE_TILE
    nk = n * TOP_K
    e_out, g_out = _router(x, w_router, b_router, ROUTER_TM)
    experts = jnp.arange(N_EXPERTS, dtype=jnp.int32)[None, :]
    hot = [(e_out[:, k:k + 1] == experts).astype(jnp.int32) for k in range(TOP_K)]
    both = sum(hot)
    csum = jnp.cumsum(both, axis=0)
    counts = csum[-1]
    earlier = csum - both
    padded = (counts + tm - 1) // tm * tm
    pad_end = jnp.cumsum(padded)
    pad_start = pad_end - padded
    dests = [jnp.sum((earlier + pad_start[None, :]) * h, axis=1) for h in hot]
    n_tiles = -(-nk // tm) + N_EXPERTS
    tok = jnp.arange(n, dtype=jnp.int32)
    row_tok = (jnp.arange(n_tiles * tm, dtype=jnp.int32) % n).at[jnp.concatenate(dests)].set(
        jnp.concatenate([tok] * TOP_K), unique_indices=True, mode='promise_in_bounds')
    n_active = (pad_end[-1] // tm).astype(jnp.int32)
    tile_all = jnp.arange(n_tiles, dtype=jnp.int32)
    tile_ids = jnp.minimum(tile_all, n_active - 1)
    tile_expert = jnp.sum((pad_end[None, :] <= (tile_ids * tm)[:, None]).astype(jnp.int32), axis=1)
    tile_expert = jnp.minimum(tile_expert, N_EXPERTS - 1)
    tile_valid = jnp.clip((pad_start + counts)[tile_expert] - tile_ids * tm, 0, tm)
    tile_valid = jnp.where(tile_all < n_active, tile_valid, 0).astype(jnp.int32)
    xs = _pack_bf16_pairs(xb).at[row_tok].get(mode='promise_in_bounds')
    ys = _moe_experts(xs, tile_expert, tile_valid, wg, wu, wd, tm, MOE_TF, MOE_BLOCKS)
    y0, y1 = (ys.at[dk].get(mode='promise_in_bounds', unique_indices=True) for dk in dests)
    return _combine_ln(x, y0, y1, g_out, ln_w, ln_b, ple, COMBINE_TM)


def kernel(x, p, ln_w, ln_b, rec_w_in, mlstm_gate_bias, mlstm_norm_w, hgrn_lb_logits, hgrn_norm_w, rec_w_out, ffn_w_gate, ffn_w_up, ffn_w_down, na_w_qkv, na_rpb, na_w_out, moe_w_router, moe_b_router, moe_w_gate, moe_w_up, moe_w_down, ple_w_gate, ple_w_proj):
    bsz, seq, d = x.shape
    n = bsz * seq
    depth = ln_w.shape[0]
    assert depth == DEPTH, "ALPHA is derived from the layer count"
    xf = x.reshape(n, d).astype(F32)
    xb = None
    H = MLSTM_HEADS
    gate_lo = 2 * H * MLSTM_DK + 2 * A_WIDTH
    gate_hi = gate_lo + 4 * H
    for i in range(depth):
        j = i // 2
        ple = (p.reshape(depth, n, -1), i, ple_w_gate[i].astype(BF16), ple_w_proj[i].astype(BF16))
        if i % 2 == 0:
            w_in = rec_w_in[j]
            qb_lo, ff_lo, ib_lo = gate_hi, gate_hi + B_WIDTH, gate_hi + 3 * B_WIDTH
            w_main = jnp.concatenate([w_in[:, :gate_lo], w_in[:, qb_lo:ff_lo], w_in[:, ib_lo:]], axis=1).astype(BF16)
            w_forget = w_in[:, ff_lo:ib_lo].astype(BF16)
            w_gate = jnp.zeros((d, V7X_LANES), BF16).at[:, :4 * H].set(w_in[:, gate_lo:gate_hi].astype(BF16))
            if xb is None:
                xb, graw = _cast_and_gates(xf, w_gate, ROUTER_TM)
            else:
                graw = _matmul(xb, w_gate, F32, MM_TM, V7X_LANES)
            graw = graw[:, :4 * H].reshape(bsz, seq, 4, H)
            u = _matmul(xb, w_main, BF16, MM_TM, MM_TN).reshape(bsz, seq, -1)
            uf = _matmul(xb, w_forget, F32, MM_TM, MM_TN).reshape(bsz, seq, -1)
            L = MLSTM_CHUNK
            gc = graw.transpose(0, 3, 1, 2).reshape(bsz, H, seq // L, L, 4)
            gr = gc.transpose(0, 1, 2, 4, 3)
            bias = mlstm_gate_bias[j].astype(F32).T
            y_a = _mlstm(u, gc, gr, bias.reshape(H, 1, 4), bias.reshape(H, 4, 1), mlstm_norm_w[j].astype(F32))
            y_b = _hgrn(u, uf, hgrn_lb_logits.astype(F32), hgrn_norm_w[j].astype(F32), j)
            mix_in = (y_a.reshape(n, -1), y_b.reshape(n, -1))
            xf, xb = _matmul_ln(mix_in, rec_w_out[j].astype(BF16), xf, ln_w[i, 0], ln_b[i, 0], LN_TM)
            hid = _swiglu_up(xb, ffn_w_gate[j], ffn_w_up[j], MM_TM, SWIGLU_TF)
            xf, xb = _matmul_ln((hid,), ffn_w_down[j].astype(BF16), xf, ln_w[i, 1], ln_b[i, 1], FFN_DOWN_TM, ple=ple)
        else:
            if xb is None:
                xb = xf.astype(BF16)
            qkv = _matmul(xb, na_w_qkv[j], BF16, MM_TM, MM_TN).reshape(bsz, seq, -1)
            att = _na(qkv, _na_bias_table(na_rpb[j])).reshape(n, -1)
            xf, xb = _matmul_ln((att,), na_w_out[j].astype(BF16), xf, ln_w[i, 0], ln_b[i, 0], LN_TM)
            xf, xb = _moe(xf, xb, moe_w_router[j], moe_b_router[j], moe_w_gate[j], moe_w_up[j], moe_w_down[j], ln_w[i, 1], ln_b[i, 1], ple)
    return xf.reshape(bsz, seq, d)
```

```python
import functools

import jax
import jax.numpy as jnp
import numpy as np
from jax import lax
from jax.experimental import pallas as pl
from jax.experimental.pallas import tpu as pltpu

F32 = jnp.float32
BF16 = jnp.bfloat16
HIGHEST = lax.Precision.HIGHEST

DEPTH = 2
ALPHA = (2 * DEPTH) ** 0.25
LN_EPS = 1e-5
RMS_EPS = 1e-6
GRID_W = 64

MLSTM_HEADS = 4
MLSTM_DK = 128
MLSTM_DV = 256
GATE_CAP = 15.0
HGRN_HEADS = 8
HGRN_DK = 128
A_WIDTH = MLSTM_HEADS * MLSTM_DV
B_WIDTH = HGRN_HEADS * HGRN_DK

NA_DH = 128
NA_HEADS = 16
NA_KH = 8
NA_KW = 16

N_EXPERTS = 8
TOP_K = 2

MLSTM_CHUNK = 256
HGRN_CHUNK = 128
HGRN_SUB = 32
MOE_TILE = 1536
MOE_BLOCKS = (512, 512, 512)
MOE_TF = 512
LN_ROW_GROUP = 128
MM_TM = 1024
MM_TN = 1024
SWIGLU_TF = 512
LN_TM = 512
FFN_DOWN_TM = 256
CAST_TM = 512
COMBINE_TM = 512
NA_GROUP = 4
NA_UNION = NA_KH + NA_GROUP - 1

V7X_LANES = 128
MXU_COLS = 256
VMEM_LIMIT = 56 * 1024 * 1024
MOE_VMEM_LIMIT = 60 * 1024 * 1024
NEG_BIG = -1e30


def _params(sem):
    return pltpu.CompilerParams(dimension_semantics=sem, vmem_limit_bytes=VMEM_LIMIT)


def _dot(a, b, **kw):
    return jnp.dot(a, b, preferred_element_type=F32, **kw)


def _dot_nt(a, b):
    return lax.dot_general(a, b, (((1,), (1,)), ((), ())), preferred_element_type=F32)


def _dot_tn(a, b):
    return lax.dot_general(a, b, (((0,), (0,)), ((), ())), preferred_element_type=F32)


def _layer_norm(y, w, b):
    mu = jnp.mean(y, axis=-1, keepdims=True)
    yc = y - mu
    var = jnp.mean(yc * yc, axis=-1, keepdims=True)
    return yc * lax.rsqrt(var + LN_EPS) * w + b


def _log_sigmoid(z):
    return jnp.minimum(z, 0.0) - jnp.log(1.0 + jnp.exp(-jnp.abs(z)))


def _mm_body(a_ref, w_ref, o_ref, wb_ref):
    @pl.when(pl.program_id(1) == 0)
    def _():
        wb_ref[...] = w_ref[...].astype(BF16)

    o_ref[...] = _dot(a_ref[...], wb_ref[...]).astype(o_ref.dtype)


def _matmul(a, w, out_dtype, tm, tn):
    m, k = a.shape
    n = w.shape[1]
    return pl.pallas_call(
        _mm_body,
        grid=(n // tn, m // tm),
        in_specs=[pl.BlockSpec((tm, k), lambda j, i: (i, 0)), pl.BlockSpec((k, tn), lambda j, i: (0, j))],
        out_specs=pl.BlockSpec((tm, tn), lambda j, i: (i, j)),
        out_shape=jax.ShapeDtypeStruct((m, n), out_dtype),
        scratch_shapes=[pltpu.VMEM((k, tn), BF16)],
        compiler_params=_params(("arbitrary", "arbitrary")),
        name="matmul",
    )(a, w)


def _cast_gates_body(x_ref, w_ref, xb_ref, g_ref):
    xb = x_ref[...].astype(BF16)
    xb_ref[...] = xb
    g_ref[...] = _dot(xb, w_ref[...])


def _cast_and_gates(x, w_gate, tm):
    m, d = x.shape
    n = w_gate.shape[1]
    return pl.pallas_call(
        _cast_gates_body,
        grid=(m // tm,),
        in_specs=[pl.BlockSpec((tm, d), lambda i: (i, 0)), pl.BlockSpec((d, n), lambda i: (0, 0))],
        out_specs=[pl.BlockSpec((tm, d), lambda i: (i, 0)), pl.BlockSpec((tm, n), lambda i: (i, 0))],
        out_shape=[jax.ShapeDtypeStruct((m, d), BF16), jax.ShapeDtypeStruct((m, n), F32)],
        compiler_params=_params(("parallel",)),
        name="cast_gates",
    )(x, w_gate)


def _ple_epilogue(p_ref, wpg_ref, wpp_ref, xs_ref, o_ref, ob_ref, rows=slice(None)):
    for c in range(o_ref.shape[1] // MXU_COLS):
        cols = pl.ds(c * MXU_COLS, MXU_COLS)
        gate = jax.nn.sigmoid(_dot(xs_ref[rows, :], wpg_ref[:, cols]))
        proj = _dot(p_ref[rows, :].astype(BF16), wpp_ref[:, cols])
        new = o_ref[rows, cols] + gate * proj
        o_ref[rows, cols] = new
        ob_ref[rows, cols] = new.astype(BF16)


def _mm_ln_body(*refs, na, ple, route):
    a_refs = refs[:na]
    w_ref, res_ref, g_ref, b_ref = refs[na:na + 4]
    rest = refs[na + 4:]
    if ple:
        p_ref, wpg_ref, wpp_ref, o_ref, ob_ref, xs_ref = rest
    elif route:
        rwh_ref, rwl_ref, rb_ref, o_ref, ob_ref, e_ref, gate_ref = rest
    else:
        o_ref, ob_ref = rest
    part = LN_ROW_GROUP
    for r in range(res_ref.shape[0] // part):
        rows = pl.ds(r * part, part)
        y = ALPHA * res_ref[rows, :]
        k0 = 0
        for a_ref in a_refs:
            ka = a_ref.shape[1]
            y = y + _dot(a_ref[rows, :], w_ref[k0:k0 + ka, :])
            k0 += ka
        out = _layer_norm(y, g_ref[...], b_ref[...])
        o_ref[rows, :] = out
        (xs_ref if ple else ob_ref)[rows, :] = out.astype(BF16)
    if route:
        e_ref[...], gate_ref[...] = _route(o_ref[...], rwh_ref[...], rwl_ref[...], rb_ref[...])
    if ple:
        _ple_epilogue(p_ref, wpg_ref, wpp_ref, xs_ref, o_ref, ob_ref)


def _matmul_ln(a_parts, w, res, ln_w, ln_b, tm, ple=None, route=None):
    m = res.shape[0]
    k, n = w.shape
    assert sum(a.shape[1] for a in a_parts) == k
    row_spec = pl.BlockSpec((tm, n), lambda i: (i, 0))
    vec_spec = pl.BlockSpec((1, n), lambda i: (0, 0))

    def whole(arr):
        return pl.BlockSpec(arr.shape, lambda i: (0, 0), pipeline_mode=pl.Buffered(1))

    in_specs = [pl.BlockSpec((tm, a.shape[1]), lambda i: (i, 0)) for a in a_parts] + [whole(w), row_spec, vec_spec, vec_spec]
    args = [*a_parts, w, res, ln_w.reshape(1, n), ln_b.reshape(1, n)]
    scratch = []
    if ple is not None:
        p, layer, wpg, wpp = ple
        in_specs += [pl.BlockSpec((None, tm, p.shape[2]), lambda i: (layer, i, 0)), whole(wpg), whole(wpp)]
        args += [p, wpg, wpp]
        scratch = [pltpu.VMEM((tm, n), BF16)]
    out_specs = [row_spec, row_spec]
    out_shape = [jax.ShapeDtypeStruct((m, n), F32), jax.ShapeDtypeStruct((m, n), BF16)]
    if route is not None:
        assert ple is None
        in_specs += [whole(r) for r in route]
        args += list(route)
        lane_spec = pl.BlockSpec((tm, V7X_LANES), lambda i: (i, 0))
        out_specs += [lane_spec, lane_spec]
        out_shape += [jax.ShapeDtypeStruct((m, V7X_LANES), jnp.int32), jax.ShapeDtypeStruct((m, V7X_LANES), F32)]
    return pl.pallas_call(
        functools.partial(_mm_ln_body, na=len(a_parts), ple=ple is not None, route=route is not None),
        grid=(m // tm,),
        in_specs=in_specs,
        out_specs=out_specs,
        out_shape=out_shape,
        scratch_shapes=scratch,
        compiler_params=_params(("parallel",)),
        name="matmul_ln",
    )(*args)


def _swiglu_up_body(a_ref, wg_ref, wu_ref, o_ref, wgb_ref, wub_ref):
    @pl.when(pl.program_id(1) == 0)
    def _():
        wgb_ref[...] = wg_ref[...].astype(BF16)
        wub_ref[...] = wu_ref[...].astype(BF16)

    for c in range(o_ref.shape[1] // MXU_COLS):
        cols = pl.ds(c * MXU_COLS, MXU_COLS)
        g = _dot(a_ref[...], wgb_ref[:, cols])
        u = _dot(a_ref[...], wub_ref[:, cols])
        o_ref[:, cols] = (g * jax.nn.sigmoid(g) * u).astype(o_ref.dtype)


def _swiglu_up(a, wg, wu, tm, tf):
    m, k = a.shape
    f = wg.shape[1]
    return pl.pallas_call(
        _swiglu_up_body,
        grid=(f // tf, m // tm),
        in_specs=[
            pl.BlockSpec((tm, k), lambda j, i: (i, 0)),
            pl.BlockSpec((k, tf), lambda j, i: (0, j)),
            pl.BlockSpec((k, tf), lambda j, i: (0, j)),
        ],
        out_specs=pl.BlockSpec((tm, tf), lambda j, i: (i, j)),
        out_shape=jax.ShapeDtypeStruct((m, f), BF16),
        scratch_shapes=[pltpu.VMEM((k, tf), BF16), pltpu.VMEM((k, tf), BF16)],
        compiler_params=_params(("arbitrary", "arbitrary")),
        name="swiglu_up",
    )(a, wg, wu)


def _mlstm_body(q_ref, k_ref, v_ref, og_ref, gc_ref, gr_ref, bc_ref, br_ref, nw_ref, y_ref, hs_ref, c_ref, *, seq):
    L = MLSTM_CHUNK
    nc = seq // L
    row = lax.broadcasted_iota(jnp.int32, (L, L), 0)
    col = lax.broadcasted_iota(jnp.int32, (L, L), 1)
    lower = row >= col
    upper = row <= col
    tril = lower.astype(F32)
    triu = upper.astype(F32)
    hs_ref[...] = jnp.zeros_like(hs_ref)
    c_ref[...] = jnp.zeros_like(c_ref)

    def cap(z):
        return GATE_CAP * jnp.tanh(z / GATE_CAP)

    def one_dir(cidx, d, n, m):
        rows = pl.ds(pl.multiple_of(cidx * L, L), L)
        gcol = cap(gc_ref[cidx] + bc_ref[...])
        grow = cap(gr_ref[cidx] + br_ref[...])
        lcol = _log_sigmoid(gcol)
        lrow = _log_sigmoid(grow)
        if d == 0:
            brow = _dot(lrow, triu, precision=HIGHEST)
            mask = lower
        else:
            brow = _dot(lrow, tril, precision=HIGHEST)
            mask = upper
        li_col = gcol[:, 2 * d:2 * d + 1]
        lf_wide = jnp.broadcast_to(lcol[:, 2 * d + 1:2 * d + 2], (L, V7X_LANES))
        b_col = _cumsum_rows(lf_wide, reverse=(d == 1))[:, 0:1]
        li_row = grow[2 * d:2 * d + 1, :]
        b_row = brow[2 * d + 1:2 * d + 2, :]
        g = b_col[L - 1:L, :] if d == 0 else b_col[0:1, :]

        q = q_ref[rows, :].astype(F32) * (MLSTM_DK ** -0.5)
        k = k_ref[rows, :].astype(F32)
        qb = q.astype(BF16)
        kb = k.astype(BF16)
        vb = v_ref[rows, :].astype(BF16)

        logd = jnp.where(mask, b_col + (li_row - b_row), -jnp.inf)
        m_t = jnp.maximum(jnp.max(logd, axis=1, keepdims=True), b_col + m)
        s = _dot_nt(qb, kb) * jnp.exp(logd - m_t)
        inter_w = jnp.exp(b_col + m - m_t)
        c_old = c_ref[d]
        num = _dot(s.astype(BF16), vb) + inter_w * _dot(qb, c_old.astype(BF16))
        den = jnp.sum(s, axis=1, keepdims=True) + inter_w * jnp.sum(q * n, axis=1, keepdims=True)
        h = num / jnp.maximum(jnp.abs(den), jnp.exp(-m_t))
        hs_ref[rows, :] += h

        a_col = g - b_col + li_col
        m_loc = jnp.max(a_col, axis=0, keepdims=True)
        kw = k * jnp.exp(a_col - m_loc)
        c_loc = _dot_tn(kw.astype(BF16), vb)
        n_loc = jnp.sum(kw, axis=0, keepdims=True)
        m_new = jnp.maximum(g + m, m_loc)
        decay = jnp.exp(g + m - m_new)
        inj = jnp.exp(m_loc - m_new)
        c_ref[d] = decay * c_old + inj * c_loc
        return decay * n + inj * n_loc, m_new

    def body(i, carry):
        nf, mf, nb, mb = carry
        nf, mf = one_dir(i, 0, nf, mf)
        nb, mb = one_dir(nc - 1 - i, 1, nb, mb)
        return nf, mf, nb, mb

    zn = jnp.zeros((1, MLSTM_DK), F32)
    zm = jnp.zeros((1, 1), F32)
    lax.fori_loop(0, nc, body, (zn, zm, zn, zm), unroll=4)

    def fin(i, carry):
        rows = pl.ds(pl.multiple_of(i * L, L), L)
        hh = hs_ref[rows, :]
        r = lax.rsqrt(jnp.mean(hh * hh, axis=-1, keepdims=True) + RMS_EPS)
        y_ref[rows, :] = (hh * r * nw_ref[...] * jax.nn.sigmoid(og_ref[rows, :].astype(F32))).astype(y_ref.dtype)
        return carry

    lax.fori_loop(0, nc, fin, 0)


def _mlstm(u, gc, gr, bias_c, bias_r, norm_w):
    bsz, seq, _ = u.shape
    L = MLSTM_CHUNK
    nc = seq // L
    H, dk, dv = MLSTM_HEADS, MLSTM_DK, MLSTM_DV
    k_off = H * dk // dk
    v_off = 2 * H * dk // dv
    o_off = v_off + H
    return pl.pallas_call(
        functools.partial(_mlstm_body, seq=seq),
        grid=(bsz, H),
        in_specs=[
            pl.BlockSpec((None, seq, dk), lambda b, h: (b, 0, h)),
            pl.BlockSpec((None, seq, dk), lambda b, h: (b, 0, k_off + h)),
            pl.BlockSpec((None, seq, dv), lambda b, h: (b, 0, v_off + h)),
            pl.BlockSpec((None, seq, dv), lambda b, h: (b, 0, o_off + h)),
            pl.BlockSpec((None, None, nc, L, 4), lambda b, h: (b, h, 0, 0, 0)),
            pl.BlockSpec((None, None, nc, 4, L), lambda b, h: (b, h, 0, 0, 0)),
            pl.BlockSpec((None, 1, 4), lambda b, h: (h, 0, 0)),
            pl.BlockSpec((None, 4, 1), lambda b, h: (h, 0, 0)),
            pl.BlockSpec((1, dv), lambda b, h: (0, h)),
        ],
        out_specs=pl.BlockSpec((None, seq, dv), lambda b, h: (b, 0, h)),
        out_shape=jax.ShapeDtypeStruct((bsz, seq, H * dv), BF16),
        scratch_shapes=[pltpu.VMEM((seq, dv), F32), pltpu.VMEM((2, dk, dv), F32)],
        compiler_params=_params(("parallel", "arbitrary")),
        name="mlstm",
    )(u, u, u, u, gc, gr, bias_c, bias_r, norm_w.reshape(1, H * dv))


def _cumsum_rows(x, reverse):
    n = x.shape[0]
    ridx = lax.broadcasted_iota(jnp.int32, x.shape, 0)
    s = 1
    while s < n:
        if reverse:
            x = x + jnp.where(ridx < n - s, pltpu.roll(x, n - s, axis=0), 0.0)
        else:
            x = x + jnp.where(ridx >= s, pltpu.roll(x, s, axis=0), 0.0)
        s *= 2
    return x


def _hgrn_body(q_ref, ff_ref, fb_ref, i_ref, g_ref, lbl_ref, nw_ref, y_ref, os_ref, st_ref, *, seq, layer):
    L = HGRN_CHUNK
    SB = HGRN_SUB
    nc = seq // L
    os_ref[...] = jnp.zeros_like(os_ref)
    st_ref[...] = jnp.zeros_like(st_ref)

    def lower_bound(d):
        lg = lbl_ref[d]
        e = jnp.exp(lg - jnp.max(lg, axis=0, keepdims=True))
        sm = e / jnp.sum(e, axis=0, keepdims=True)
        return jnp.sum(sm[:layer + 1, :], axis=0, keepdims=True)

    lbs = (lower_bound(0), lower_bound(1))

    def one_dir(cidx, d):
        rows = pl.ds(pl.multiple_of(cidx * L, L), L)
        qr = q_ref[rows, :].astype(F32)
        q = qr * jax.nn.sigmoid(qr)
        vb = i_ref[rows, :].astype(BF16)
        fr = (ff_ref if d == 0 else fb_ref)[rows, :]
        lb = lbs[d]
        f = lb + (1.0 - lb) * jax.nn.sigmoid(fr)
        k = 1.0 - f
        lf = jnp.log(f)
        b = _cumsum_rows(lf, reverse=(d == 1))
        g = b[L - 1:L, :] if d == 0 else b[0:1, :]
        qi = (q * jnp.exp(b)).astype(BF16)
        ke = (k * jnp.exp(g - b)).astype(BF16)
        parts = []
        for jb in range(L // SB):
            lo, hi = jb * SB, (jb + 1) * SB
            bm = b[lo + SB // 2:lo + SB // 2 + 1, :]
            qm = (q[lo:hi, :] * jnp.exp(b[lo:hi, :] - bm)).astype(BF16)
            ks = slice(0, hi) if d == 0 else slice(lo, L)
            km = (k[ks, :] * jnp.exp(bm - b[ks, :])).astype(BF16)
            if km.shape[0] < L:
                pad = jnp.zeros((L - km.shape[0], HGRN_DK), BF16)
                km = jnp.concatenate([km, pad] if d == 0 else [pad, km], axis=0)
            tq = lo + lax.broadcasted_iota(jnp.int32, (SB, L), 0)
            ts = lax.broadcasted_iota(jnp.int32, (SB, L), 1)
            keep = (ts <= tq) if d == 0 else (ts >= tq)
            parts.append(jnp.where(keep, _dot_nt(qm, km), 0.0))
        a = jnp.concatenate(parts, axis=0)
        st = st_ref[d]
        o = _dot(a.astype(BF16), vb) + _dot_nt(qi, st.astype(BF16))
        st_ref[d] = st * jnp.exp(g) + _dot_tn(vb, ke)
        os_ref[rows, :] += o

    def body(i, carry):
        one_dir(i, 0)
        one_dir(nc - 1 - i, 1)
        return carry

    lax.fori_loop(0, nc, body, 0, unroll=16)

    FL = 256

    def fin(i, carry):
        rows = pl.ds(pl.multiple_of(i * FL, FL), FL)
        hh = os_ref[rows, :]
        r = lax.rsqrt(jnp.mean(hh * hh, axis=-1, keepdims=True) + RMS_EPS)
        gg = g_ref[rows, :].astype(F32)
        y_ref[rows, :] = (hh * r * nw_ref[...] * (gg * jax.nn.sigmoid(gg))).astype(y_ref.dtype)
        return carry

    lax.fori_loop(0, seq // FL, fin, 0)


def _hgrn(u, uf, lb_logits, norm_w, layer):
    bsz, seq, _ = u.shape
    H, dk = HGRN_HEADS, HGRN_DK
    base = (2 * MLSTM_HEADS * MLSTM_DK + 2 * A_WIDTH) // dk
    slots = lb_logits.shape[1]
    return pl.pallas_call(
        functools.partial(_hgrn_body, seq=seq, layer=layer),
        grid=(bsz, H),
        in_specs=[
            pl.BlockSpec((None, seq, dk), lambda b, h: (b, 0, base + h)),
            pl.BlockSpec((None, seq, dk), lambda b, h: (b, 0, h)),
            pl.BlockSpec((None, seq, dk), lambda b, h: (b, 0, H + h)),
            pl.BlockSpec((None, seq, dk), lambda b, h: (b, 0, base + H + h)),
            pl.BlockSpec((None, seq, dk), lambda b, h: (b, 0, base + 2 * H + h)),
            pl.BlockSpec((2, slots, dk), lambda b, h: (0, 0, h)),
            pl.BlockSpec((1, dk), lambda b, h: (0, h)),
        ],
        out_specs=pl.BlockSpec((None, seq, dk), lambda b, h: (b, 0, h)),
        out_shape=jax.ShapeDtypeStruct((bsz, seq, H * dk), BF16),
        scratch_shapes=[pltpu.VMEM((seq, dk), F32), pltpu.VMEM((2, dk, dk), F32)],
        compiler_params=_params(("parallel", "arbitrary")),
        name="hgrn2",
    )(u, uf, uf, u, u, lb_logits, norm_w.reshape(1, H * dk))


def _na_body(q_ref, k_ref, v_ref, tb_ref, o_ref, bias_ref, *, rows):
    W = GRID_W
    G, U, kh = NA_GROUP, NA_UNION, NA_KH
    ng = rows // G
    scale = NA_DH ** -0.5

    @pl.when(pl.program_id(1) == 0)
    def _():
        for c, (delta, offs) in enumerate(_na_group_classes(rows)):
            for i in range(G):
                qr = pl.ds(i * W, W)
                for k0 in range(0, U, 2):
                    v0 = offs[i] <= k0 < offs[i] + kh
                    v1 = k0 + 1 < U and offs[i] <= k0 + 1 < offs[i] + kh
                    dr = delta + k0 - i + kh - 1
                    if k0 + 1 >= U:
                        blk = tb_ref[1, dr][:, :W] if v0 else jnp.full((W, W), NEG_BIG, F32)
                        bias_ref[c, qr, pl.ds(k0 * W, W)] = blk
                        continue
                    if v0 and v1:
                        blk = tb_ref[0, dr]
                    elif v0:
                        blk = tb_ref[1, dr]
                    elif v1:
                        blk = tb_ref[2, dr + 1]
                    else:
                        blk = jnp.full((W, 2 * W), NEG_BIG, F32)
                    bias_ref[c, qr, pl.ds(k0 * W, 2 * W)] = blk

    def body(gi, carry):
        r0 = gi * G
        us = jnp.clip(r0 - NA_KH // 2, 0, rows - U)
        cls = jnp.where(gi == 0, 0, jnp.where(gi == ng - 1, 2, 1))
        qrows = pl.ds(pl.multiple_of(r0 * W, G * W), G * W)
        kwin = pl.ds(pl.multiple_of(us * W, W), U * W)
        s = _dot_nt(q_ref[qrows, :], k_ref[kwin, :]) * scale + bias_ref[cls]
        m = jnp.max(s, axis=-1, keepdims=True)
        p = jnp.exp(s - m)
        den = jnp.sum(p, axis=-1, keepdims=True)
        o = _dot(p.astype(BF16), v_ref[kwin, :]) / den
        o_ref[qrows, :] = o.astype(o_ref.dtype)
        return carry

    lax.fori_loop(0, ng, body, 0, unroll=16)


def _na_group_classes(rows):
    G, U, kh = NA_GROUP, NA_UNION, NA_KH

    def info(r0):
        us = min(max(r0 - kh // 2, 0), rows - U)
        return us - r0, tuple(min(max(r0 + i - kh // 2, 0), rows - kh) - us for i in range(G))

    ng = rows // G
    infos = [info(G * g) for g in range(ng)]
    classes = [infos[0], infos[1], infos[-1]]
    assert rows % G == 0 and ng >= 3 and all(infos[g] == classes[1] for g in range(1, ng - 1))
    assert all(0 <= o and o + kh <= U for c in classes for o in c[1])
    return classes


def _na_bias_table(rpb):
    W = GRID_W
    ndr, ndc = rpb.shape[1], rpb.shape[2]
    col = np.arange(W)
    col_start = np.clip(col - NA_KW // 2, 0, W - NA_KW)
    inside = (col[None, :] >= col_start[:, None]) & (col[None, :] < col_start[:, None] + NA_KW)
    dc = np.clip(col[None, :] - col[:, None] + NA_KW - 1, 0, ndc - 1)
    sel = np.zeros((ndc, W, W), np.float32)
    qq, kk = np.nonzero(inside)
    sel[dc[qq, kk], qq, kk] = 1.0
    off = np.zeros_like(sel)
    sel_left = jnp.asarray(np.concatenate([sel, off], axis=2))
    sel_right = jnp.asarray(np.concatenate([off, sel], axis=2))
    mask = np.where(inside, 0.0, NEG_BIG).astype(np.float32)
    allneg = np.full_like(mask, NEG_BIG)
    neg_pair = np.tile(np.concatenate([mask, mask], axis=1), (ndr, 1, 1))
    neg_pair[ndr - 1, :, W:] = NEG_BIG
    neg_lo = np.concatenate([mask, allneg], axis=1)
    neg_hi = np.concatenate([allneg, mask], axis=1)
    r = rpb.astype(F32)
    r_next = jnp.concatenate([r[:, 1:], jnp.zeros_like(r[:, :1])], axis=1)

    def expand(rows, sel_half):
        return jnp.einsum('hdm,mqk->hdqk', rows, sel_half, precision=HIGHEST)

    pair = expand(r, sel_left) + expand(r_next, sel_right) + neg_pair
    lo = expand(r, sel_left) + neg_lo
    hi = expand(r, sel_right) + neg_hi
    return jnp.stack([pair, lo, hi], axis=1)


def _na(qkv, bias_tbl):
    bsz, seq, _ = qkv.shape
    rows = seq // GRID_W
    H, dh = NA_HEADS, NA_DH
    return pl.pallas_call(
        functools.partial(_na_body, rows=rows),
        grid=(H, bsz),
        in_specs=[
            pl.BlockSpec((None, seq, dh), lambda h, b: (b, 0, h)),
            pl.BlockSpec((None, seq, dh), lambda h, b: (b, 0, H + h)),
            pl.BlockSpec((None, seq, dh), lambda h, b: (b, 0, 2 * H + h)),
            pl.BlockSpec((None,) + bias_tbl.shape[1:], lambda h, b: (h, 0, 0, 0, 0)),
        ],
        out_specs=pl.BlockSpec((None, seq, dh), lambda h, b: (b, 0, h)),
        out_shape=jax.ShapeDtypeStruct((bsz, seq, H * dh), BF16),
        scratch_shapes=[pltpu.VMEM((3, NA_GROUP * GRID_W, NA_UNION * GRID_W), F32)],
        compiler_params=_params(("arbitrary", "arbitrary")),
        name="natten",
    )(qkv, qkv, qkv, bias_tbl)


def _route(x, wh, wl, b):
    xh = x.astype(BF16)
    xl = (x - xh.astype(F32)).astype(BF16)
    logits = _dot(xh, wh) + _dot(xl, wh) + _dot(xh, wl) + b
    lane = lax.broadcasted_iota(jnp.int32, logits.shape, 1)
    nl = logits.shape[1]
    m1 = jnp.max(logits, axis=-1, keepdims=True)
    i1 = jnp.min(jnp.where(logits == m1, lane, nl), axis=-1, keepdims=True)
    rest = jnp.where(lane == i1, -jnp.inf, logits)
    m2 = jnp.max(rest, axis=-1, keepdims=True)
    i2 = jnp.min(jnp.where(rest == m2, lane, nl), axis=-1, keepdims=True)
    ex = jnp.exp(m2 - m1)
    g1 = 1.0 / (1.0 + ex)
    g2 = ex / (1.0 + ex)
    return (jnp.where(lane == 0, i1, jnp.where(lane == 1, i2, 0)),
            jnp.where(lane == 0, g1, jnp.where(lane == 1, g2, 0.0)))


def _router_operands(w_router, b_router):
    d, ne = w_router.shape
    w = jnp.zeros((d, V7X_LANES), F32).at[:, :ne].set(w_router.astype(F32))
    b = jnp.full((1, V7X_LANES), NEG_BIG, F32).at[0, :ne].set(b_router.astype(F32))
    wh = w.astype(BF16)
    wl = (w - wh.astype(F32)).astype(BF16)
    return wh, wl, b


def _pack_bf16_pairs(xb):
    half = xb.shape[1] // 2
    lo = lax.bitcast_convert_type(xb[:, :half], jnp.uint16).astype(jnp.uint32)
    hi = lax.bitcast_convert_type(xb[:, half:], jnp.uint16).astype(jnp.uint32)
    return lo | (hi << 16)


def _moe_body(te_ref, nv_ref, xp_ref, wg_ref, wu_ref, wd_ref, o_ref, xb_ref, wgb_ref, wub_ref, wdb_ref, *, blocks):
    t = pl.program_id(0)
    f = pl.program_id(1)
    nv = nv_ref[t]
    tm, half = xp_ref.shape
    tf = wg_ref.shape[1]
    d = o_ref.shape[1]
    nc = MXU_COLS

    @pl.when(f == 0)
    def _():
        o_ref[...] = jnp.zeros_like(o_ref)
        for start, size in blocks:
            rows = pl.ds(start, size)
            w = xp_ref[rows, :]
            xb_ref[rows, :half] = lax.bitcast_convert_type(w << 16, F32).astype(BF16)
            xb_ref[rows, half:] = lax.bitcast_convert_type(w & jnp.uint32(0xFFFF0000), F32).astype(BF16)

    for sb, (start, size) in enumerate(blocks):
        rows = pl.ds(start, size)

        @pl.when(start < nv)
        def _():
            hs = []
            for c in range(tf // nc):
                cols = pl.ds(c * nc, nc)
                if sb == 0:
                    wg = wg_ref[:, cols].astype(BF16)
                    wu = wu_ref[:, cols].astype(BF16)
                    wgb_ref[:, cols] = wg
                    wub_ref[:, cols] = wu
                else:
                    wg = wgb_ref[:, cols]
                    wu = wub_ref[:, cols]
                g = _dot(xb_ref[rows, :], wg)
                u = _dot(xb_ref[rows, :], wu)
                hs.append((g * jax.nn.sigmoid(g) * u).astype(BF16))
            h = jnp.concatenate(hs, axis=1)
            for c in range(d // nc):
                cols = pl.ds(c * nc, nc)
                if sb == 0:
                    wd = wd_ref[:, cols].astype(BF16)
                    wdb_ref[:, cols] = wd
                else:
                    wd = wdb_ref[:, cols]
                o_ref[rows, cols] += _dot(h, wd)


def _moe_experts(xp, tile_expert, tile_valid, wg, wu, wd, tm, tf, block_sizes):
    rows, half = xp.shape
    d = 2 * half
    n_tiles = rows // tm
    fdim = wg.shape[2]
    nf = fdim // tf
    assert sum(block_sizes) == tm
    blocks = tuple((sum(block_sizes[:i]), s) for i, s in enumerate(block_sizes))

    def fidx(t, f, nv):
        return jnp.where(nv[t] > 0, f, nf - 1)

    return pl.pallas_call(
        functools.partial(_moe_body, blocks=blocks),
        grid_spec=pltpu.PrefetchScalarGridSpec(
            num_scalar_prefetch=2,
            grid=(n_tiles, nf),
            in_specs=[
                pl.BlockSpec((tm, half), lambda t, f, te, nv: (t, 0), pipeline_mode=pl.Buffered(1)),
                pl.BlockSpec((None, d, tf), lambda t, f, te, nv: (te[t], 0, fidx(t, f, nv))),
                pl.BlockSpec((None, d, tf), lambda t, f, te, nv: (te[t], 0, fidx(t, f, nv))),
                pl.BlockSpec((None, tf, d), lambda t, f, te, nv: (te[t], fidx(t, f, nv), 0)),
            ],
            out_specs=pl.BlockSpec((tm, d), lambda t, f, te, nv: (t, 0), pipeline_mode=pl.Buffered(1)),
            scratch_shapes=[pltpu.VMEM((tm, d), BF16), pltpu.VMEM((d, tf), BF16), pltpu.VMEM((d, tf), BF16), pltpu.VMEM((tf, d), BF16)],
        ),
        out_shape=jax.ShapeDtypeStruct((rows, d), F32),
        compiler_params=pltpu.CompilerParams(dimension_semantics=("arbitrary", "arbitrary"), vmem_limit_bytes=MOE_VMEM_LIMIT),
        name="moe_experts",
    )(tile_expert, tile_valid, xp, wg, wu, wd)


def _combine_ln_body(x_ref, y0_ref, y1_ref, g_ref, w_ref, b_ref, p_ref, wpg_ref, wpp_ref, o_ref, ob_ref, xs_ref):
    half = x_ref.shape[0] // 2
    for r in range(2):
        rows = pl.ds(r * half, half)
        g = g_ref[rows, :]
        mix = y0_ref[rows, :] * g[:, 0:1] + y1_ref[rows, :] * g[:, 1:2]
        out = _layer_norm(ALPHA * x_ref[rows, :] + mix, w_ref[...], b_ref[...])
        o_ref[rows, :] = out
        xs_ref[rows, :] = out.astype(BF16)
        _ple_epilogue(p_ref, wpg_ref, wpp_ref, xs_ref, o_ref, ob_ref, rows)


def _combine_ln(x, y0, y1, gates, ln_w, ln_b, ple, tm):
    n, d = x.shape
    p, layer, wpg, wpp = ple
    row_spec = pl.BlockSpec((tm, d), lambda i: (i, 0))
    vec_spec = pl.BlockSpec((1, d), lambda i: (0, 0))

    def whole(arr):
        return pl.BlockSpec(arr.shape, lambda i: (0, 0), pipeline_mode=pl.Buffered(1))

    return pl.pallas_call(
        _combine_ln_body,
        grid=(n // tm,),
        in_specs=[row_spec, row_spec, row_spec, pl.BlockSpec((tm, V7X_LANES), lambda i: (i, 0)), vec_spec, vec_spec,
                  pl.BlockSpec((None, tm, p.shape[2]), lambda i: (layer, i, 0)), whole(wpg), whole(wpp)],
        out_specs=[row_spec, row_spec],
        out_shape=[jax.ShapeDtypeStruct((n, d), F32), jax.ShapeDtypeStruct((n, d), BF16)],
        scratch_shapes=[pltpu.VMEM((tm, d), BF16)],
        compiler_params=_params(("parallel",)),
        name="combine_ln",
    )(x, y0, y1, gates, ln_w.reshape(1, d), ln_b.reshape(1, d), p, wpg, wpp)


def _moe(x, xb, e_out, g_out, wg, wu, wd, ln_w, ln_b, ple):
    n, d = x.shape
    tm = MOE_TILE
    nk = n * TOP_K
    experts = jnp.arange(N_EXPERTS, dtype=jnp.int32)[None, :]
    hot = [(e_out[:, k:k + 1] == experts).astype(jnp.int32) for k in range(TOP_K)]
    both = sum(hot)
    csum = jnp.cumsum(both, axis=0)
    counts = csum[-1]
    earlier = csum - both
    padded = (counts + tm - 1) // tm * tm
    pad_end = jnp.cumsum(padded)
    pad_start = pad_end - padded
    dests = [jnp.sum((earlier + pad_start[None, :]) * h, axis=1) for h in hot]
    n_tiles = -(-nk // tm) + N_EXPERTS
    tok = jnp.arange(n, dtype=jnp.int32)
    row_tok = (jnp.arange(n_tiles * tm, dtype=jnp.int32) % n).at[jnp.concatenate(dests)].set(
        jnp.concatenate([tok] * TOP_K), unique_indices=True, mode='promise_in_bounds')
    n_active = (pad_end[-1] // tm).astype(jnp.int32)
    tile_all = jnp.arange(n_tiles, dtype=jnp.int32)
    tile_ids = jnp.minimum(tile_all, n_active - 1)
    tile_expert = jnp.sum((pad_end[None, :] <= (tile_ids * tm)[:, None]).astype(jnp.int32), axis=1)
    tile_expert = jnp.minimum(tile_expert, N_EXPERTS - 1)
    tile_valid = jnp.clip((pad_start + counts)[tile_expert] - tile_ids * tm, 0, tm)
    tile_valid = jnp.where(tile_all < n_active, tile_valid, 0).astype(jnp.int32)
    xs = _pack_bf16_pairs(xb).at[row_tok].get(mode='promise_in_bounds')
    ys = _moe_experts(xs, tile_expert, tile_valid, wg, wu, wd, tm, MOE_TF, MOE_BLOCKS)
    y0, y1 = (ys.at[dk].get(mode='promise_in_bounds', unique_indices=True) for dk in dests)
    return _combine_ln(x, y0, y1, g_out, ln_w, ln_b, ple, COMBINE_TM)


def kernel(x, p, ln_w, ln_b, rec_w_in, mlstm_gate_bias, mlstm_norm_w, hgrn_lb_logits, hgrn_norm_w, rec_w_out, ffn_w_gate, ffn_w_up, ffn_w_down, na_w_qkv, na_rpb, na_w_out, moe_w_router, moe_b_router, moe_w_gate, moe_w_up, moe_w_down, ple_w_gate, ple_w_proj):
    bsz, seq, d = x.shape
    n = bsz * seq
    depth = ln_w.shape[0]
    assert depth == DEPTH, "ALPHA is derived from the layer count"
    xf = x.reshape(n, d).astype(F32)
    xb = None
    H = MLSTM_HEADS
    gate_lo = 2 * H * MLSTM_DK + 2 * A_WIDTH
    gate_hi = gate_lo + 4 * H
    for i in range(depth):
        j = i // 2
        ple = (p.reshape(depth, n, -1), i, ple_w_gate[i].astype(BF16), ple_w_proj[i].astype(BF16))
        if i % 2 == 0:
            w_in = rec_w_in[j]
            qb_lo, ff_lo, ib_lo = gate_hi, gate_hi + B_WIDTH, gate_hi + 3 * B_WIDTH
            w_main = jnp.concatenate([w_in[:, :gate_lo], w_in[:, qb_lo:ff_lo], w_in[:, ib_lo:]], axis=1).astype(BF16)
            w_forget = w_in[:, ff_lo:ib_lo].astype(BF16)
            w_gate = jnp.zeros((d, V7X_LANES), BF16).at[:, :4 * H].set(w_in[:, gate_lo:gate_hi].astype(BF16))
            if xb is None:
                xb, graw = _cast_and_gates(xf, w_gate, CAST_TM)
            else:
                graw = _matmul(xb, w_gate, F32, MM_TM, V7X_LANES)
            graw = graw[:, :4 * H].reshape(bsz, seq, 4, H)
            u = _matmul(xb, w_main, BF16, MM_TM, MM_TN).reshape(bsz, seq, -1)
            uf = _matmul(xb, w_forget, F32, MM_TM, MM_TN).reshape(bsz, seq, -1)
            L = MLSTM_CHUNK
            gc = graw.transpose(0, 3, 1, 2).reshape(bsz, H, seq // L, L, 4)
            gr = gc.transpose(0, 1, 2, 4, 3)
            bias = mlstm_gate_bias[j].astype(F32).T
            y_a = _mlstm(u, gc, gr, bias.reshape(H, 1, 4), bias.reshape(H, 4, 1), mlstm_norm_w[j].astype(F32))
            y_b = _hgrn(u, uf, hgrn_lb_logits.astype(F32), hgrn_norm_w[j].astype(F32), j)
            mix_in = (y_a.reshape(n, -1), y_b.reshape(n, -1))
            xf, xb = _matmul_ln(mix_in, rec_w_out[j].astype(BF16), xf, ln_w[i, 0], ln_b[i, 0], LN_TM)
            hid = _swiglu_up(xb, ffn_w_gate[j], ffn_w_up[j], MM_TM, SWIGLU_TF)
            xf, xb = _matmul_ln((hid,), ffn_w_down[j].astype(BF16), xf, ln_w[i, 1], ln_b[i, 1], FFN_DOWN_TM, ple=ple)
        else:
            if xb is None:
                xb = xf.astype(BF16)
            qkv = _matmul(xb, na_w_qkv[j], BF16, MM_TM, MM_TN).reshape(bsz, seq, -1)
            att = _na(qkv, _na_bias_table(na_rpb[j])).reshape(n, -1)
            xf, xb, e_out, g_out = _matmul_ln((att,), na_w_out[j].astype(BF16), xf, ln_w[i, 0], ln_b[i, 0], LN_TM,
                                              route=_router_operands(moe_w_router[j], moe_b_router[j]))
            xf, xb = _moe(xf, xb, e_out, g_out, moe_w_gate[j], moe_w_up[j], moe_w_down[j], ln_w[i, 1], ln_b[i, 1], ple)
    return xf.reshape(bsz, seq, d)
```
